```python
import math
import jax
import jax.numpy as jnp
from jax import lax
import numpy as np

D_MODEL = 1024
BATCH = 8
SEQ = 4096
DEPTH = 4

GRID_W = 64
CTX_LEN = 256
EPS = 1e-6
NEG_INF = -1e30

MIX_W = 256
N_BRANCH = 4
HEAD_DIM = 64
ATTN_SCALE = HEAD_DIM ** -0.5

SSD_HEADS = MIX_W // HEAD_DIM
SSD_GROUPS = 2
SSD_STATE = 128
SSD_CONV = 5
SSD_CHUNK = 128
SSD_XBC = MIX_W + 2 * SSD_GROUPS * SSD_STATE

S5_GROUP = 16
S5_GROUPS = MIX_W // S5_GROUP
S5_STATE = 64

GA_HEADS = MIX_W // HEAD_DIM
GA_KV = 2
Q_BLOCK = 128
ROPE_BASE = 10000.0
ROPE_FREQS = HEAD_DIM // 4

SW_HEADS = MIX_W // HEAD_DIM
SW_KV = 2
WINDOW = 128

FFN_DIM = 2816
N_EXPERTS = 8
TOP_K = 2
EXPERT_DIM = 3584
MOE_BLOCK = 256
N_DENSE = (DEPTH + 1) // 2
N_MOE = DEPTH // 2

IN_NAMES = ('a_z', 'a_xbc', 'a_dt', 'b_u', 'c_q', 'c_k', 'c_v', 'd_q', 'd_k', 'd_v', 'gates')
IN_SIZES = (MIX_W, SSD_XBC, 2 * SSD_HEADS, MIX_W,
            GA_HEADS * HEAD_DIM, GA_KV * HEAD_DIM, GA_KV * HEAD_DIM,
            SW_HEADS * HEAD_DIM, SW_KV * HEAD_DIM, SW_KV * HEAD_DIM,
            N_BRANCH * D_MODEL)
IN_COLS = sum(IN_SIZES)

kernel_name = 'hybrid_ssd_s5_gqa_swa_moe_dit'


def _rms(x, g):
    xf = x.astype(jnp.float32)
    y = xf * lax.rsqrt(jnp.mean(xf * xf, axis=-1, keepdims=True) + EPS)
    return (y * g.astype(jnp.float32)).astype(x.dtype)


def _flip(t):
    return jnp.flip(t, axis=1)


def _split_cols(p):
    points, acc = [], 0
    for s in IN_SIZES[:-1]:
        acc += s
        points.append(acc)
    return jnp.split(p, points, axis=-1)


def _axial_rope_tables(rows):
    pos_r = jnp.repeat(jnp.arange(rows, dtype=jnp.float32), GRID_W)
    pos_c = jnp.tile(jnp.arange(GRID_W, dtype=jnp.float32), rows)
    inv = ROPE_BASE ** (-jnp.arange(ROPE_FREQS, dtype=jnp.float32) / ROPE_FREQS)
    ang = jnp.concatenate([pos_r[:, None] * inv, pos_c[:, None] * inv], axis=-1)
    return jnp.cos(ang), jnp.sin(ang)


def _apply_rope(x, cos, sin):
    f = ROPE_FREQS
    xf = x.astype(jnp.float32)
    x1 = jnp.concatenate([xf[..., 0:f], xf[..., 2 * f:3 * f]], axis=-1)
    x2 = jnp.concatenate([xf[..., f:2 * f], xf[..., 3 * f:4 * f]], axis=-1)
    cs, sn = cos[None, :, None, :], sin[None, :, None, :]
    r1 = x1 * cs - x2 * sn
    r2 = x2 * cs + x1 * sn
    out = jnp.concatenate([r1[..., :f], r2[..., :f], r1[..., f:], r2[..., f:]], axis=-1)
    return out.astype(x.dtype)


def _qkv(q, k, v, n_heads, n_kv, q_gain, k_gain, rope):
    b, n, _ = q.shape
    q = q.reshape(b, n, n_heads, HEAD_DIM)
    k = k.reshape(b, n, n_kv, HEAD_DIM)
    v = v.reshape(b, n, n_kv, HEAD_DIM)
    if q_gain is not None:
        q, k = _rms(q, q_gain), _rms(k, k_gain)
    if rope is not None:
        q, k = _apply_rope(q, *rope), _apply_rope(k, *rope)
    return q.reshape(b, n, n_kv, n_heads // n_kv, HEAD_DIM), k, v


def _attend(q, k, v, sink=None):
    s = jnp.einsum('bqkgd,bskd->bkgqs', q, k, preferred_element_type=jnp.float32) * ATTN_SCALE
    if sink is not None:
        sk = jnp.broadcast_to(sink.astype(jnp.float32)[None, :, :, None, None], s.shape[:-1] + (1,))
        p = jax.nn.softmax(jnp.concatenate([s, sk], axis=-1), axis=-1)[..., :-1]
    else:
        p = jax.nn.softmax(s, axis=-1)
    return jnp.einsum('bkgqs,bskd->bqkgd', p.astype(v.dtype), v)


def _global_attention(q_l, k_l, v_l, q_c, k_c, v_c, ctx_out):
    b, n, kv, g, dh = q_l.shape
    k_all = jnp.concatenate([k_c, k_l], axis=1)
    v_all = jnp.concatenate([v_c, v_l], axis=1)
    qb = q_l.reshape(b, n // Q_BLOCK, Q_BLOCK, kv, g, dh).transpose(1, 0, 2, 3, 4, 5)
    ob = lax.map(lambda qq: _attend(qq, k_all, v_all), qb)
    o_l = ob.transpose(1, 0, 2, 3, 4, 5).reshape(b, n, kv * g * dh)
    o_c = _attend(q_c, k_c, v_c).reshape(b, k_c.shape[1], kv * g * dh) if ctx_out else None
    return o_l, o_c


def _window_attention(q_l, k_l, v_l, q_c, k_c, v_c, sink, ctx_out):
    b, n, kv, g, dh = q_l.shape
    nb = n // WINDOW
    sink_kg = sink.reshape(kv, g)
    qb = q_l.reshape(b, nb, WINDOW, kv, g, dh)

    def band(t):
        z = jnp.zeros((b, WINDOW) + t.shape[2:], t.dtype)
        tp = jnp.concatenate([z, t, z], axis=1).reshape((b, nb + 2, WINDOW) + t.shape[2:])
        return jnp.concatenate([tp[:, :-2], tp[:, 1:-1], tp[:, 2:]], axis=2)

    kw, vw = band(k_l), band(v_l)
    s_w = jnp.einsum('bnqkgd,bnskd->bnkgqs', qb, kw, preferred_element_type=jnp.float32) * ATTN_SCALE
    blk = jnp.arange(nb)[:, None]
    qpos = blk * WINDOW + jnp.arange(WINDOW)[None, :]
    kpos = (blk - 1) * WINDOW + jnp.arange(3 * WINDOW)[None, :]
    mask = (jnp.abs(qpos[:, :, None] - kpos[:, None, :]) <= WINDOW) & ((kpos >= 0) & (kpos < n))[:, None, :]
    s_w = jnp.where(mask[None, :, None, None], s_w, NEG_INF)
    s_c = jnp.einsum('bnqkgd,bskd->bnkgqs', qb, k_c, preferred_element_type=jnp.float32) * ATTN_SCALE
    sk = jnp.broadcast_to(sink_kg.astype(jnp.float32)[None, None, :, :, None, None], s_c.shape[:-1] + (1,))
    p = jax.nn.softmax(jnp.concatenate([s_c, s_w, sk], axis=-1), axis=-1).astype(v_l.dtype)
    nc = k_c.shape[1]
    o = (jnp.einsum('bnkgqs,bskd->bnqkgd', p[..., :nc], v_c)
         + jnp.einsum('bnkgqs,bnskd->bnqkgd', p[..., nc:nc + 3 * WINDOW], vw))
    o_l = o.reshape(b, n, kv * g * dh)
    o_c = _attend(q_c, k_c, v_c, sink_kg).reshape(b, nc, kv * g * dh) if ctx_out else None
    return o_l, o_c


def _dwconv(x, w, bias):
    ch = x.shape[-1]
    y = lax.conv_general_dilated(x, w[:, None, :].astype(x.dtype), window_strides=(1,),
                                 padding=[(SSD_CONV // 2, SSD_CONV // 2)],
                                 dimension_numbers=('NWC', 'WIO', 'NWC'), feature_group_count=ch)
    return y + bias


def _segsum_decay(a_cs):
    q = a_cs.shape[-1]
    diff = a_cs[..., :, None] - a_cs[..., None, :]
    tri = jnp.tril(jnp.ones((q, q), dtype=bool))
    return jnp.where(tri, jnp.exp(jnp.where(tri, diff, 0.0)), 0.0)


def _ssd_scan(xs, dt, a, bm, cm, init):
    b, n, h, p = xs.shape
    nst = bm.shape[-1]
    nc, q = n // SSD_CHUNK, SSD_CHUNK
    la = (dt * a).reshape(b, nc, q, h).transpose(0, 3, 1, 2)
    a_cs = jnp.cumsum(la, axis=-1)
    xdt = (xs.astype(jnp.float32) * dt[..., None]).reshape(b, nc, q, h, p)
    bc = bm.astype(jnp.float32).reshape(b, nc, q, h, nst)
    cc = cm.astype(jnp.float32).reshape(b, nc, q, h, nst)
    scores = jnp.einsum('bclhn,bcshn->bhcls', cc, bc) * _segsum_decay(a_cs)
    y_diag = jnp.einsum('bhcls,bcshp->bclhp', scores, xdt)
    decay_to_end = jnp.exp(a_cs[..., -1:] - a_cs)
    chunk_states = jnp.einsum('bclhn,bhcl,bclhp->bchpn', bc, decay_to_end, xdt)
    chunk_decay = jnp.exp(a_cs[..., -1])

    def carry(state, inp):
        dec, st = inp
        return state * dec[..., None, None] + st, state

    final, entering = lax.scan(carry, init.astype(jnp.float32),
                               (jnp.moveaxis(chunk_decay, 2, 0), jnp.moveaxis(chunk_states, 1, 0)))
    entering = jnp.moveaxis(entering, 0, 1)
    y_off = jnp.einsum('bclhn,bchpn,bhcl->bclhp', cc, entering, jnp.exp(a_cs))
    return (y_diag + y_off).reshape(b, n, h, p), final


def _ssd_mixer(z_l, xbc_l, dt_l, z_c, xbc_c, dt_c, conv_w, conv_b, a_log, dt_bias, d_skip, norm_g, ctx_out):
    a = -jnp.exp(a_log.astype(jnp.float32))
    rep = SSD_HEADS // SSD_GROUPS

    def prep(xbc, dt):
        bsz, n, _ = xbc.shape
        xbc = jax.nn.silu(_dwconv(xbc, conv_w, conv_b))
        xs, bm, cm = jnp.split(xbc, [MIX_W, MIX_W + SSD_GROUPS * SSD_STATE], axis=-1)
        xs = xs.reshape(bsz, n, SSD_HEADS, HEAD_DIM)
        bm = jnp.repeat(bm.reshape(bsz, n, SSD_GROUPS, SSD_STATE), rep, axis=2)
        cm = jnp.repeat(cm.reshape(bsz, n, SSD_GROUPS, SSD_STATE), rep, axis=2)
        dt = jax.nn.softplus(dt.astype(jnp.float32).reshape(bsz, n, 2, SSD_HEADS) + dt_bias.astype(jnp.float32))
        return xs, bm, cm, dt

    xs_c, b_c, c_c, dt_c = prep(xbc_c, dt_c)
    xs_l, b_l, c_l, dt_l = prep(xbc_l, dt_l)
    init = jnp.zeros((xs_l.shape[0], SSD_HEADS, HEAD_DIM, SSD_STATE), jnp.float32)
    yc_f, st_f = _ssd_scan(xs_c, dt_c[:, :, 0], a[0], b_c, c_c, init)
    yl_f, _ = _ssd_scan(xs_l, dt_l[:, :, 0], a[0], b_l, c_l, st_f)
    yc_b, st_b = _ssd_scan(_flip(xs_c), _flip(dt_c[:, :, 1]), a[1], _flip(b_c), _flip(c_c), init)
    yl_b, _ = _ssd_scan(_flip(xs_l), _flip(dt_l[:, :, 1]), a[1], _flip(b_l), _flip(c_l), st_b)

    def finish(y_f, y_b, xs, z):
        bsz, n = xs.shape[:2]
        y = y_f + _flip(y_b) + d_skip.astype(jnp.float32)[:, None] * xs.astype(jnp.float32)
        y = y.reshape(bsz, n, MIX_W) * jax.nn.silu(z.astype(jnp.float32))
        return _rms(y, norm_g).astype(z.dtype)

    o_l = finish(yl_f, yl_b, xs_l, z_l)
    o_c = finish(yc_f, yc_b, xs_c, z_c) if ctx_out else None
    return o_l, o_c


def _s5_discretize(lam_re, lam_im, log_step, b_re, b_im):
    step = jnp.exp(log_step.astype(jnp.float32))[:, None]
    lr = jnp.minimum(lam_re.astype(jnp.float32), -1e-4)
    li = lam_im.astype(jnp.float32)
    mag = jnp.exp(lr * step)
    ang = li * step
    ab_re, ab_im = mag * jnp.cos(ang), mag * jnp.sin(ang)
    den = lr * lr + li * li
    f_re = ((ab_re - 1.0) * lr + ab_im * li) / den
    f_im = (ab_im * lr - (ab_re - 1.0) * li) / den
    br, bi = b_re.astype(jnp.float32), b_im.astype(jnp.float32)
    bb_re = f_re[..., None] * br - f_im[..., None] * bi
    bb_im = f_re[..., None] * bi + f_im[..., None] * br
    return ab_re, ab_im, bb_re, bb_im


def _s5_scan(u, disc, c_re, c_im, init):
    ab_re, ab_im, bb_re, bb_im = disc
    s0_re, s0_im = init
    bu_re = jnp.einsum('blgp,gnp->blgn', u, bb_re)
    bu_im = jnp.einsum('blgp,gnp->blgn', u, bb_im)
    bu_re = bu_re.at[:, 0].add(ab_re * s0_re - ab_im * s0_im)
    bu_im = bu_im.at[:, 0].add(ab_re * s0_im + ab_im * s0_re)
    a_re = jnp.broadcast_to(ab_re, bu_re.shape)
    a_im = jnp.broadcast_to(ab_im, bu_im.shape)

    def combine(e1, e2):
        a1r, a1i, b1r, b1i = e1
        a2r, a2i, b2r, b2i = e2
        return (a1r * a2r - a1i * a2i, a1r * a2i + a1i * a2r,
                a2r * b1r - a2i * b1i + b2r, a2r * b1i + a2i * b1r + b2i)

    _, _, s_re, s_im = lax.associative_scan(combine, (a_re, a_im, bu_re, bu_im), axis=1)
    y = (jnp.einsum('blgn,gpn->blgp', s_re, c_re.astype(jnp.float32))
         - jnp.einsum('blgn,gpn->blgp', s_im, c_im.astype(jnp.float32)))
    return y, (s_re[:, -1], s_im[:, -1])


def _s5_mixer(u_l, u_c, lam_re, lam_im, log_step, b_re, b_im, c_re, c_im, d_skip, glu_w, ctx_out):
    def grouped(u):
        return u.astype(jnp.float32).reshape(u.shape[0], u.shape[1], S5_GROUPS, S5_GROUP)

    ul, uc = grouped(u_l), grouped(u_c)
    zero = jnp.zeros((u_l.shape[0], S5_GROUPS, S5_STATE), jnp.float32)
    disc_f = _s5_discretize(lam_re[0], lam_im[0], log_step[0], b_re[0], b_im[0])
    disc_b = _s5_discretize(lam_re[1], lam_im[1], log_step[1], b_re[1], b_im[1])
    yc_f, st_f = _s5_scan(uc, disc_f, c_re[0], c_im[0], (zero, zero))
    yl_f, _ = _s5_scan(ul, disc_f, c_re[0], c_im[0], st_f)
    yc_b, st_b = _s5_scan(_flip(uc), disc_b, c_re[1], c_im[1], (zero, zero))
    yl_b, _ = _s5_scan(_flip(ul), disc_b, c_re[1], c_im[1], st_b)
    d = d_skip.astype(jnp.float32).reshape(S5_GROUPS, S5_GROUP)

    def finish(y_f, y_b, u, like):
        y = y_f + _flip(y_b) + d * u
        v = jax.nn.gelu(y.reshape(u.shape[0], u.shape[1], MIX_W)).astype(like.dtype)
        val, gate = jnp.split(v @ glu_w, 2, axis=-1)
        return val * jax.nn.sigmoid(gate)

    o_l = finish(yl_f, yl_b, ul, u_l)
    o_c = finish(yc_f, yc_b, uc, u_c) if ctx_out else None
    return o_l, o_c


def _merge(branches, gates, w_branch, w_out):
    gs = jnp.split(gates, N_BRANCH, axis=-1)
    acc = jax.nn.sigmoid(gs[0]) * (branches[0] @ w_branch[0])
    for i in range(1, N_BRANCH):
        acc = acc + jax.nn.sigmoid(gs[i]) * (branches[i] @ w_branch[i])
    return acc @ w_out


def _swiglu(x, w_in, w_out):
    g, u = jnp.split(x @ w_in, 2, axis=-1)
    return (jax.nn.silu(g) * u) @ w_out


def _moe_swiglu(xf, router, w_in, w_out):
    t, d = xf.shape
    logits = (xf @ router).astype(jnp.float32)
    top_v, top_i = lax.top_k(logits, TOP_K)
    gate = jax.nn.softmax(top_v, axis=-1)
    flat_e = top_i.reshape(-1).astype(jnp.int32)
    n_assign = t * TOP_K
    order = jnp.argsort(flat_e)
    sorted_e = flat_e[order]
    counts = jnp.bincount(flat_e, length=N_EXPERTS).astype(jnp.int32)
    starts = jnp.cumsum(counts) - counts
    padded = (counts + MOE_BLOCK - 1) // MOE_BLOCK * MOE_BLOCK
    ends = jnp.cumsum(padded)
    pstarts = ends - padded
    dest_sorted = (pstarts[sorted_e] + jnp.arange(n_assign, dtype=jnp.int32) - starts[sorted_e]).astype(jnp.int32)
    n_blocks = -(-n_assign // MOE_BLOCK) + N_EXPERTS
    n_rows = n_blocks * MOE_BLOCK
    row_tok = jnp.full((n_rows,), t, jnp.int32).at[dest_sorted].set((order // TOP_K).astype(jnp.int32))
    block_e = jnp.minimum(jnp.searchsorted(ends, jnp.arange(n_blocks, dtype=jnp.int32) * MOE_BLOCK, side='right'),
                          N_EXPERTS - 1)
    x_rows = jnp.concatenate([xf, jnp.zeros((1, d), xf.dtype)], axis=0)[row_tok].reshape(n_blocks, MOE_BLOCK, d)

    def expert_block(args):
        xb, e = args
        return _swiglu(xb, w_in[e], w_out[e])

    y_rows = lax.map(expert_block, (x_rows, block_e)).reshape(n_rows, d)
    dest = jnp.zeros((n_assign,), jnp.int32).at[order].set(dest_sorted)
    return jnp.sum(y_rows[dest].reshape(t, TOP_K, d) * gate[..., None].astype(y_rows.dtype), axis=1)


def setup_inputs(seed: int = 0) -> dict:
    key = jax.random.key(seed)
    ks = jax.random.split(key, 40)
    D = D_MODEL

    def nrm(i, shape, scale):
        return scale * jax.random.normal(ks[i], shape, jnp.float32)

    def gain(i, shape, s=0.02):
        return 1.0 + s * jax.random.normal(ks[i], shape, jnp.float32)

    dt0 = jnp.exp(jax.random.uniform(ks[12], (DEPTH, 2, SSD_HEADS), jnp.float32, math.log(1e-3), math.log(1e-1)))
    s5_shape = (DEPTH, 2, S5_GROUPS, S5_STATE)
    return {
        'x': nrm(0, (BATCH, SEQ, D), 1.0),
        'c': nrm(1, (BATCH, D), 1.0),
        'ctx': nrm(2, (BATCH, CTX_LEN, D), 1.0),
        'c_ctx': nrm(3, (D,), 1.0),
        'norm1_g': gain(4, (DEPTH, D)),
        'norm2_g': gain(5, (DEPTH, D)),
        'ada_w': nrm(6, (DEPTH, D, 6 * D), 0.5 * D ** -0.5),
        'ada_b': nrm(7, (DEPTH, 6 * D), 0.02),
        'w_in': nrm(8, (DEPTH, D, IN_COLS), D ** -0.5),
        'ssd_conv_w': nrm(9, (DEPTH, SSD_CONV, SSD_XBC), SSD_CONV ** -0.5),
        'ssd_conv_b': nrm(10, (DEPTH, SSD_XBC), 0.02),
        'ssd_a_log': jnp.log(jax.random.uniform(ks[11], (DEPTH, 2, SSD_HEADS), jnp.float32, 1.0, 16.0)),
        'ssd_dt_bias': dt0 + jnp.log(-jnp.expm1(-dt0)),
        'ssd_d': gain(13, (DEPTH, SSD_HEADS), 0.1),
        'ssd_norm_g': gain(14, (DEPTH, MIX_W)),
        's5_lam_re': -0.5 + nrm(15, s5_shape, 0.01),
        's5_lam_im': jnp.pi * jnp.arange(S5_STATE, dtype=jnp.float32) + nrm(16, s5_shape, 0.01),
        's5_log_step': jax.random.uniform(ks[17], (DEPTH, 2, S5_GROUPS), jnp.float32, math.log(1e-3), math.log(1e-1)),
        's5_b_re': nrm(18, (DEPTH, 2, S5_GROUPS, S5_STATE, S5_GROUP), (2 * S5_GROUP) ** -0.5),
        's5_b_im': nrm(19, (DEPTH, 2, S5_GROUPS, S5_STATE, S5_GROUP), (2 * S5_GROUP) ** -0.5),
        's5_c_re': nrm(20, (DEPTH, 2, S5_GROUPS, S5_GROUP, S5_STATE), (2 * S5_STATE) ** -0.5),
        's5_c_im': nrm(21, (DEPTH, 2, S5_GROUPS, S5_GROUP, S5_STATE), (2 * S5_STATE) ** -0.5),
        's5_d': nrm(22, (DEPTH, MIX_W), 1.0),
        's5_glu_w': nrm(23, (DEPTH, MIX_W, 2 * MIX_W), MIX_W ** -0.5),
        'qk_norm_g': gain(24, (DEPTH, 2, HEAD_DIM)),
        'swa_sink': nrm(25, (DEPTH, SW_HEADS), 0.5),
        'w_branch': nrm(26, (DEPTH, N_BRANCH, MIX_W, D), MIX_W ** -0.5),
        'w_out': nrm(27, (DEPTH, D, D), D ** -0.5),
        'ffn_w_in': nrm(28, (N_DENSE, D, 2 * FFN_DIM), D ** -0.5),
        'ffn_w_out': nrm(29, (N_DENSE, FFN_DIM, D), FFN_DIM ** -0.5),
        'moe_router': nrm(30, (N_MOE, D, N_EXPERTS), D ** -0.5),
        'moe_w_in': nrm(31, (N_MOE, N_EXPERTS, D, 2 * EXPERT_DIM), D ** -0.5),
        'moe_w_out': nrm(32, (N_MOE, N_EXPERTS, EXPERT_DIM, D), EXPERT_DIM ** -0.5),
        'final_norm_g': gain(33, (D,)),
    }


def reference(x, c, ctx, c_ctx, norm1_g, norm2_g, ada_w, ada_b, w_in, ssd_conv_w, ssd_conv_b, ssd_a_log,
              ssd_dt_bias, ssd_d, ssd_norm_g, s5_lam_re, s5_lam_im, s5_log_step, s5_b_re, s5_b_im, s5_c_re,
              s5_c_im, s5_d, s5_glu_w, qk_norm_g, swa_sink, w_branch, w_out, ffn_w_in, ffn_w_out, moe_router,
              moe_w_in, moe_w_out, final_norm_g):
    bsz, n_lat, d_model = x.shape
    rows = n_lat // GRID_W
    rope = _axial_rope_tables(rows)
    c_act = jax.nn.silu(c)
    cc_act = jax.nn.silu(c_ctx)
    xc = ctx
    for l in range(DEPTH):
        ctx_out = l < DEPTH - 1
        mod = jnp.split((c_act @ ada_w[l] + ada_b[l])[:, None, :], 6, axis=-1)
        mod_c = jnp.split(cc_act @ ada_w[l] + ada_b[l], 6, axis=-1)
        h = _rms(x, norm1_g[l]) * (1 + mod[1]) + mod[0]
        hc = _rms(xc, norm1_g[l]) * (1 + mod_c[1]) + mod_c[0]
        pl = dict(zip(IN_NAMES, _split_cols(h @ w_in[l])))
        pc = dict(zip(IN_NAMES, _split_cols(hc @ w_in[l])))

        ya_l, ya_c = _ssd_mixer(pl['a_z'], pl['a_xbc'], pl['a_dt'], pc['a_z'], pc['a_xbc'], pc['a_dt'],
                                ssd_conv_w[l], ssd_conv_b[l], ssd_a_log[l], ssd_dt_bias[l], ssd_d[l],
                                ssd_norm_g[l], ctx_out)
        yb_l, yb_c = _s5_mixer(pl['b_u'], pc['b_u'], s5_lam_re[l], s5_lam_im[l], s5_log_step[l], s5_b_re[l],
                               s5_b_im[l], s5_c_re[l], s5_c_im[l], s5_d[l], s5_glu_w[l], ctx_out)
        qg, kg = qk_norm_g[l, 0], qk_norm_g[l, 1]
        q1, k1, v1 = _qkv(pl['c_q'], pl['c_k'], pl['c_v'], GA_HEADS, GA_KV, qg, kg, rope)
        q1c, k1c, v1c = _qkv(pc['c_q'], pc['c_k'], pc['c_v'], GA_HEADS, GA_KV, qg, kg, None)
        yc_l, yc_c = _global_attention(q1, k1, v1, q1c, k1c, v1c, ctx_out)
        q2, k2, v2 = _qkv(pl['d_q'], pl['d_k'], pl['d_v'], SW_HEADS, SW_KV, None, None, rope)
        q2c, k2c, v2c = _qkv(pc['d_q'], pc['d_k'], pc['d_v'], SW_HEADS, SW_KV, None, None, None)
        yd_l, yd_c = _window_attention(q2, k2, v2, q2c, k2c, v2c, swa_sink[l], ctx_out)

        x = x + mod[2] * _merge([ya_l, yb_l, yc_l, yd_l], pl['gates'], w_branch[l], w_out[l])
        h2 = _rms(x, norm2_g[l]) * (1 + mod[4]) + mod[3]
        if ctx_out:
            xc = xc + mod_c[2] * _merge([ya_c, yb_c, yc_c, yd_c], pc['gates'], w_branch[l], w_out[l])
            h2c = _rms(xc, norm2_g[l]) * (1 + mod_c[4]) + mod_c[3]

        if l % 2 == 0:
            x = x + mod[5] * _swiglu(h2, ffn_w_in[l // 2], ffn_w_out[l // 2])
            if ctx_out:
                xc = xc + mod_c[5] * _swiglu(h2c, ffn_w_in[l // 2], ffn_w_out[l // 2])
        else:
            n_l = bsz * n_lat
            if ctx_out:
                toks = jnp.concatenate([h2.reshape(n_l, d_model), h2c.reshape(-1, d_model)], axis=0)
            else:
                toks = h2.reshape(n_l, d_model)
            f = _moe_swiglu(toks, moe_router[l // 2], moe_w_in[l // 2], moe_w_out[l // 2])
            x = x + mod[5] * f[:n_l].reshape(x.shape)
            if ctx_out:
                xc = xc + mod_c[5] * f[n_l:].reshape(xc.shape)
    return _rms(x, final_norm_g)
```

```python
import functools

import numpy as np
import jax
import jax.numpy as jnp
from jax import lax
from jax.experimental import pallas as pl
from jax.experimental.pallas import tpu as pltpu

F32 = jnp.float32
BF16 = jnp.bfloat16

EPS = 1e-6
NEG_INF = -1e30
GRID_W = 64
MIX_W = 256
HEAD_DIM = 64
N_HEADS = 4
ATTN_SCALE = HEAD_DIM ** -0.5
SSD_STATE = 128
SSD_CONV = 5
CHUNK = 128
HALO = 16
S5_GROUPS = 16
S5_GROUP = 16
S5_STATE = 64
S5_LANES = S5_GROUPS * S5_STATE
S5_STEPS = 64
ROPE_BASE = 10000.0
ROPE_FREQS = 16
WINDOW = 128
N_EXPERTS = 8
MOE_TILE = 512
VMEM_LIMIT = 56 * 1024 * 1024

C_GATES, C_Z, C_BU, C_CQ, C_DQ = 0, 4096, 4352, 4608, 4864
C_CK, C_CV, C_DK, C_DV = 5120, 5248, 5376, 5504
C_AX, C_AB, C_AC = 5632, 5888, 6144
P_COLS = 6400
PROJ_TN = 640


def _cp(*sem):
    return pltpu.CompilerParams(dimension_semantics=sem, vmem_limit_bytes=VMEM_LIMIT)


def _pow2_tile(cap, *dims):
    t = 1
    while t * 2 <= cap and all(d % (t * 2) == 0 for d in dims):
        t *= 2
    return t


def _dot(a, b):
    return jnp.dot(a, b, preferred_element_type=F32)


def _dot_nt(a, b):
    return lax.dot_general(a, b, (((1,), (1,)), ((), ())), preferred_element_type=F32)


def _dot_tn(a, b):
    return lax.dot_general(a, b, (((0,), (0,)), ((), ())), preferred_element_type=F32)


def _split3(x):
    hi = x.astype(BF16)
    r1 = x - hi.astype(F32)
    mid = r1.astype(BF16)
    lo = (r1 - mid.astype(F32)).astype(BF16)
    return hi, mid, lo


def _silu(x):
    return x * jax.nn.sigmoid(x)


def _adaln_kernel(c_ref, w_ref, b_ref, o_ref):
    c = c_ref[...]
    o_ref[0] = jnp.dot(_silu(c), w_ref[0], preferred_element_type=F32,
                       precision=lax.Precision.HIGHEST) + b_ref[0]


def _adaln(cvec, ada_w, ada_b):
    depth, d, n = ada_w.shape
    tn = 1024
    return pl.pallas_call(
        _adaln_kernel,
        out_shape=jax.ShapeDtypeStruct((depth, 16, n), F32),
        grid=(depth, n // tn),
        in_specs=[pl.BlockSpec((16, d), lambda l, j: (0, 0)),
                  pl.BlockSpec((1, d, tn), lambda l, j: (l, 0, j)),
                  pl.BlockSpec((1, 1, tn), lambda l, j: (l, 0, j))],
        out_specs=pl.BlockSpec((1, 16, tn), lambda l, j: (l, 0, j)),
        compiler_params=_cp("parallel", "parallel"),
        name="adaln",
    )(cvec, ada_w, ada_b.reshape(depth, 1, n))


def _mod_group(row0, t_lat, seq):
    return jnp.where(row0 >= t_lat, 0, 1 + row0 // seq)


def _inproj_kernel(x_ref, g_ref, mod_ref, w_ref, wdt_ref, o_ref, odt_ref, h_ref):
    @pl.when(pl.program_id(1) == 0)
    def _():
        x = x_ref[...]
        ms = jnp.mean(x * x, axis=-1, keepdims=True)
        y = x * lax.rsqrt(ms + EPS) * g_ref[...]
        h = (y * (1.0 + mod_ref[0, 1:2, :]) + mod_ref[0, 0:1, :]).astype(BF16)
        h_ref[...] = h
        odt_ref[...] = _dot(h, wdt_ref[...])

    o_ref[...] = _dot(h_ref[...], w_ref[...]).astype(BF16)


def _inproj(x, g, mod, w, wdt, t_lat, seq, tm):
    t, d = x.shape
    n = w.shape[1]
    tn = PROJ_TN
    return pl.pallas_call(
        _inproj_kernel,
        out_shape=(jax.ShapeDtypeStruct((t, n), BF16), jax.ShapeDtypeStruct((t, 128), F32)),
        grid=(t // tm, n // tn),
        in_specs=[pl.BlockSpec((tm, d), lambda i, j: (i, 0)),
                  pl.BlockSpec((1, d), lambda i, j: (0, 0)),
                  pl.BlockSpec((1, 6, d), lambda i, j: (_mod_group(i * tm, t_lat, seq), 0, 0)),
                  pl.BlockSpec((d, tn), lambda i, j: (0, j)),
                  pl.BlockSpec((d, 128), lambda i, j: (0, 0))],
        out_specs=(pl.BlockSpec((tm, tn), lambda i, j: (i, j)),
                   pl.BlockSpec((tm, 128), lambda i, j: (i, 0))),
        scratch_shapes=[pltpu.VMEM((tm, d), BF16)],
        compiler_params=_cp("parallel", "arbitrary"),
        name="inproj",
    )(x, g.reshape(1, d), mod, w, wdt)


def _ssd_consts():
    r = np.arange(CHUNK)
    tri_l = (r[None, :] <= r[:, None]).astype(np.float32)
    tri_u = tri_l.T.copy()
    shifts = np.zeros((4, CHUNK, CHUNK + 2 * HALO), np.float32)
    for n, k in enumerate((0, 1, 3, 4)):
        shifts[n, r, r + HALO + k - 2] = 1.0
    return (jnp.asarray(np.stack([tri_l, tri_u]), BF16), jnp.asarray(shifts, BF16))


def _ssd_kernel(zl_ref, xl_ref, bl_ref, cl_ref, dtl_ref, zc_ref, xc_ref, bc_ref, cc_ref, dtc_ref,
                cw_ref, cb_ref, an_ref, bias_ref, dsk_ref, ng_ref, tri_ref, sh_ref,
                yl_ref, yc_ref, act_l, act_c, yf_l, yf_c, st_ref, *, n_lat, n_ctx):
    q = CHUNK
    lane128 = lax.broadcasted_iota(jnp.int32, (q, 128), 1)
    lane256 = lax.broadcasted_iota(jnp.int32, (q, MIX_W), 1)
    lane256r = lax.broadcasted_iota(jnp.int32, (1, MIX_W), 1)
    row_i = lax.broadcasted_iota(jnp.int32, (q, q), 0)
    col_i = lax.broadcasted_iota(jnp.int32, (q, q), 1)
    head_masks = [(lane256 >= HEAD_DIM * h) & (lane256 < HEAD_DIM * (h + 1)) for h in range(N_HEADS)]
    head_masks_r = [(lane256r >= HEAD_DIM * h) & (lane256r < HEAD_DIM * (h + 1)) for h in range(N_HEADS)]
    group_masks = [lane256 < 128, lane256 >= 128]

    def conv_act(x_ref, b_ref, c_ref, n_chunks, c):
        def rows(ref, start, size):
            return ref[pl.ds(start, size), :]

        start = c * q
        if isinstance(c, int):
            p0, n0 = max(start - HALO, 0), min(start + q, n_chunks * q - HALO)
            pf, nf = float(c > 0), float(c < n_chunks - 1)
        else:
            start = pl.multiple_of(start, q)
            p0 = pl.multiple_of(jnp.maximum(start - HALO, 0), HALO)
            n0 = pl.multiple_of(jnp.minimum(start + q, n_chunks * q - HALO), HALO)
            pf, nf = (c > 0).astype(F32), (c < n_chunks - 1).astype(F32)
        parts = []
        for ref in (x_ref, b_ref, c_ref):
            prev = (rows(ref, p0, HALO).astype(F32) * pf).astype(BF16)
            nxt = (rows(ref, n0, HALO).astype(F32) * nf).astype(BF16)
            parts.append(jnp.concatenate([prev, rows(ref, start, q), nxt], axis=0))
        ext = jnp.concatenate(parts, axis=1)
        cur = ext[HALO:HALO + q].astype(F32)
        acc = cur * cw_ref[2:3, :] + cb_ref[...]
        for n, k in enumerate((0, 1, 3, 4)):
            acc = acc + _dot(sh_ref[n], ext) * cw_ref[k:k + 1, :]
        return _silu(acc)

    def chunk(seg, c, d):
        z_ref, x_ref, b_ref, c_ref, dt_ref, act_ref, yf_ref, y_ref, n_chunks = seg
        start = c * q if isinstance(c, int) else pl.multiple_of(c * q, q)
        if d == 0:
            act = conv_act(x_ref, b_ref, c_ref, n_chunks, c)
            act_ref[pl.ds(start, q), :] = act.astype(BF16)
        act = act_ref[pl.ds(start, q), :]
        xs = act[:, 0:MIX_W].astype(F32)
        bm = act[:, MIX_W:2 * MIX_W]
        cm = act[:, 2 * MIX_W:3 * MIX_W]

        dt_n = jax.nn.softplus(dt_ref[pl.ds(start, q), :] + bias_ref[...])
        la_n = dt_n * an_ref[...]
        hi, mid, lo = _split3(la_n)
        tri = tri_ref[d]
        cs_n = _dot(tri, hi) + _dot(tri, mid) + _dot(tri, lo)
        cs_t = cs_n.T
        edge = q - 1 if d == 0 else 0
        tri_mask = (col_i <= row_i) if d == 0 else (col_i >= row_i)

        dt_full = jnp.zeros((q, MIX_W), F32)
        dte_full = jnp.zeros((q, MIX_W), F32)
        ecs_full = jnp.zeros((q, MIX_W), F32)
        tot_full = jnp.zeros((1, MIX_W), F32)
        decay = []
        for h in range(N_HEADS):
            sel = lane128 == (N_HEADS * d + h)
            cs_col = jnp.sum(jnp.where(sel, cs_n, 0.0), axis=-1, keepdims=True)
            dt_col = jnp.sum(jnp.where(sel, dt_n, 0.0), axis=-1, keepdims=True)
            tot = cs_col[edge:edge + 1, :]
            cs_row = cs_t[N_HEADS * d + h:N_HEADS * d + h + 1, :]
            decay.append(jnp.where(tri_mask, jnp.exp(cs_col - cs_row), 0.0))
            dt_full = jnp.where(head_masks[h], dt_col, dt_full)
            dte_full = jnp.where(head_masks[h], jnp.exp(tot - cs_col), dte_full)
            ecs_full = jnp.where(head_masks[h], jnp.exp(cs_col), ecs_full)
            tot_full = jnp.where(head_masks_r[h], jnp.exp(tot), tot_full)

        xdt = xs * dt_full
        state = st_ref[...]
        y = jnp.zeros((q, MIX_W), F32)
        y_off = jnp.zeros((q, MIX_W), F32)
        upd = jnp.zeros((SSD_STATE, MIX_W), F32)
        xdte = xdt * dte_full
        for g in range(2):
            bg = bm[:, 128 * g:128 * (g + 1)]
            cg = cm[:, 128 * g:128 * (g + 1)]
            cb = _dot_nt(cg, bg)
            for h in (2 * g, 2 * g + 1):
                m = (cb * decay[h]).astype(BF16)
                y = y + _dot(m, jnp.where(head_masks[h], xdt, 0.0).astype(BF16))
            y_off = y_off + _dot(cg, jnp.where(group_masks[g], state, 0.0).astype(BF16))
            upd = upd + _dot_tn(bg, jnp.where(group_masks[g], xdte, 0.0).astype(BF16))
        y = y + y_off * ecs_full
        st_ref[...] = state * tot_full + upd

        if d == 0:
            yf_ref[pl.ds(start, q), :] = y
        else:
            y = y + yf_ref[pl.ds(start, q), :] + dsk_ref[...] * xs
            y = y * _silu(z_ref[pl.ds(start, q), :].astype(F32))
            ms = jnp.mean(y * y, axis=-1, keepdims=True)
            y_ref[pl.ds(start, q), :] = (y * lax.rsqrt(ms + EPS) * ng_ref[...]).astype(BF16)

    seg_l = (zl_ref, xl_ref, bl_ref, cl_ref, dtl_ref, act_l, yf_l, yl_ref, n_lat)
    seg_c = (zc_ref, xc_ref, bc_ref, cc_ref, dtc_ref, act_c, yf_c, yc_ref, n_ctx)

    for d in range(2):
        st_ref[...] = jnp.zeros_like(st_ref)
        order = range(n_ctx) if d == 0 else range(n_ctx - 1, -1, -1)
        for c in order:
            chunk(seg_c, c, d)

        def body(i, carry, d=d):
            chunk(seg_l, i if d == 0 else n_lat - 1 - i, d)
            return carry

        lax.fori_loop(0, n_lat, body, 0)


def _ssd(p, pdt, conv_w, conv_b, a_log, dt_bias, d_skip, norm_g, bsz, seq, ctx):
    t = p.shape[0]
    n_lat, n_ctx = seq // CHUNK, ctx // CHUNK
    cb0 = (bsz * seq) // ctx
    tri, shifts = _ssd_consts()
    cw = jnp.zeros((8, 3 * MIX_W), F32).at[:SSD_CONV].set(conv_w)
    a_n = jnp.zeros((1, 128), F32).at[0, :8].set(-jnp.exp(a_log.astype(F32)).reshape(8))
    bias_n = jnp.zeros((1, 128), F32).at[0, :8].set(dt_bias.astype(F32).reshape(8))
    dsk = jnp.repeat(d_skip.astype(F32), HEAD_DIM).reshape(1, MIX_W)

    def lat(col):
        return pl.BlockSpec((seq, MIX_W), lambda b, col=col: (b, col // MIX_W))

    def cx(col):
        return pl.BlockSpec((ctx, MIX_W), lambda b, col=col: (cb0 + b, col // MIX_W))

    def full(shape):
        return pl.BlockSpec(shape, lambda b: (0,) * len(shape))

    kern = functools.partial(_ssd_kernel, n_lat=n_lat, n_ctx=n_ctx)
    yl, yc = pl.pallas_call(
        kern,
        out_shape=(jax.ShapeDtypeStruct((bsz * seq, MIX_W), BF16),
                   jax.ShapeDtypeStruct((bsz * ctx, MIX_W), BF16)),
        grid=(bsz,),
        in_specs=[lat(C_Z), lat(C_AX), lat(C_AB), lat(C_AC),
                  pl.BlockSpec((seq, 128), lambda b: (b, 0)),
                  cx(C_Z), cx(C_AX), cx(C_AB), cx(C_AC),
                  pl.BlockSpec((ctx, 128), lambda b: (cb0 + b, 0)),
                  full((8, 3 * MIX_W)), full((1, 3 * MIX_W)), full((1, 128)), full((1, 128)),
                  full((1, MIX_W)), full((1, MIX_W)), full((2, CHUNK, CHUNK)),
                  full((4, CHUNK, CHUNK + 2 * HALO))],
        out_specs=(pl.BlockSpec((seq, MIX_W), lambda b: (b, 0)),
                   pl.BlockSpec((ctx, MIX_W), lambda b: (b, 0))),
        scratch_shapes=[pltpu.VMEM((seq, 3 * MIX_W), BF16), pltpu.VMEM((ctx, 3 * MIX_W), BF16),
                        pltpu.VMEM((seq, MIX_W), F32), pltpu.VMEM((ctx, MIX_W), F32),
                        pltpu.VMEM((SSD_STATE, MIX_W), F32)],
        compiler_params=_cp("parallel"),
        name="ssd",
    )(p, p, p, p, pdt, p, p, p, p, pdt, cw, conv_b.reshape(1, -1).astype(F32), a_n, bias_n,
      dsk, norm_g.reshape(1, MIX_W).astype(F32), tri, shifts)
    return jnp.concatenate([yl, yc], axis=0)


def _s5_discretize(lam_re, lam_im, log_step, b_re, b_im):
    step = jnp.exp(log_step.astype(F32))[:, None]
    lr = jnp.minimum(lam_re.astype(F32), -1e-4)
    li = lam_im.astype(F32)
    mag = jnp.exp(lr * step)
    ang = li * step
    ab_re, ab_im = mag * jnp.cos(ang), mag * jnp.sin(ang)
    den = lr * lr + li * li
    f_re = ((ab_re - 1.0) * lr + ab_im * li) / den
    f_im = (ab_im * lr - (ab_re - 1.0) * li) / den
    br, bi = b_re.astype(F32), b_im.astype(F32)
    bb_re = f_re[..., None] * br - f_im[..., None] * bi
    bb_im = f_re[..., None] * bi + f_im[..., None] * br
    return ab_re, ab_im, bb_re, bb_im


def _s5_mats(lam_re, lam_im, log_step, b_re, b_im, c_re, c_im):
    eye = jnp.eye(S5_GROUPS, dtype=F32)
    a_all, b_all, c_all = [], [], []
    for d in range(2):
        ab_re, ab_im, bb_re, bb_im = _s5_discretize(lam_re[d], lam_im[d], log_step[d], b_re[d], b_im[d])
        bm = [jnp.einsum('gnp,gh->gphn', m, eye).reshape(MIX_W, S5_LANES) for m in (bb_re, bb_im)]
        cm = [jnp.einsum('gpn,gh->gnhp', m.astype(F32), eye).reshape(S5_LANES, MIX_W)
              for m in (c_re[d], c_im[d])]
        b_all.append(jnp.concatenate(bm, axis=1))
        c_all.append(jnp.concatenate([cm[0], -cm[1]], axis=0))
        a_all.append(jnp.concatenate([ab_re.reshape(1, S5_LANES), ab_im.reshape(1, S5_LANES)], axis=1))
    a = jnp.broadcast_to(jnp.stack(a_all), (2, 8, 2 * S5_LANES))
    return a, jnp.stack(b_all).astype(BF16), jnp.stack(c_all).astype(BF16)


def _s5_kernel(u_ref, a_ref, b_ref, c_ref, y_ref, buf_ref, s_ref):
    d = pl.program_id(0)
    n = S5_LANES

    @pl.when(pl.program_id(1) == 0)
    def _():
        s_ref[...] = jnp.zeros_like(s_ref)

    buf_ref[...] = _dot(u_ref[...], b_ref[0])
    a_re = a_ref[0, :, 0:n]
    a_im = a_ref[0, :, n:2 * n]

    def step(j, carry):
        s_re, s_im = carry
        jj = jnp.where(d == 0, j, S5_STEPS - 1 - j)
        rows = pl.ds(pl.multiple_of(jj * 8, 8), 8)
        n_re = a_re * s_re - a_im * s_im + buf_ref[rows, 0:n]
        n_im = a_re * s_im + a_im * s_re + buf_ref[rows, n:2 * n]
        buf_ref[rows, 0:n] = n_re
        buf_ref[rows, n:2 * n] = n_im
        return n_re, n_im

    s_re, s_im = lax.fori_loop(0, S5_STEPS, step, (s_ref[:, 0:n], s_ref[:, n:2 * n]), unroll=4)
    s_ref[:, 0:n] = s_re
    s_ref[:, n:2 * n] = s_im
    y_ref[0] = _dot(buf_ref[...].astype(BF16), c_ref[0])


def _s5_finish_kernel(yf_ref, yb_ref, u_ref, d_ref, w_ref, o_ref):
    y = yf_ref[0] + yb_ref[0] + d_ref[...] * u_ref[...].astype(F32)
    v = jax.nn.gelu(y, approximate=True).astype(BF16)
    r = _dot(v, w_ref[...])
    o_ref[...] = (r[:, 0:MIX_W] * jax.nn.sigmoid(r[:, MIX_W:2 * MIX_W])).astype(BF16)


def _s5(p, lam_re, lam_im, log_step, b_re, b_im, c_re, c_im, d_skip, glu_w, bsz, seq, ctx):
    assert bsz <= 8
    t_lat = bsz * seq
    u = p[:, C_BU:C_BU + MIX_W]
    u_l = jnp.transpose(u[:t_lat].reshape(bsz, seq, MIX_W), (1, 0, 2))
    u_c = jnp.transpose(u[t_lat:].reshape(bsz, ctx, MIX_W), (1, 0, 2))
    u_tm = jnp.concatenate([u_c, u_l], axis=0)
    if bsz < 8:
        u_tm = jnp.pad(u_tm, ((0, 0), (0, 8 - bsz), (0, 0)))
    steps = seq + ctx
    u_tm = u_tm.reshape(steps * 8, MIX_W)
    a, bmat, cmat = _s5_mats(lam_re, lam_im, log_step, b_re, b_im, c_re, c_im)
    rows = S5_STEPS * 8
    nc, ncc = steps // S5_STEPS, ctx // S5_STEPS

    def chunk_of(d, i):
        return jnp.where(d == 0, i, jnp.where(i < ncc, ncc - 1 - i, nc + ncc - 1 - i))

    y = pl.pallas_call(
        _s5_kernel,
        out_shape=jax.ShapeDtypeStruct((2, steps * 8, MIX_W), F32),
        grid=(2, nc),
        in_specs=[pl.BlockSpec((rows, MIX_W), lambda d, i: (chunk_of(d, i), 0)),
                  pl.BlockSpec((1, 8, 2 * S5_LANES), lambda d, i: (d, 0, 0)),
                  pl.BlockSpec((1, MIX_W, 2 * S5_LANES), lambda d, i: (d, 0, 0)),
                  pl.BlockSpec((1, 2 * S5_LANES, MIX_W), lambda d, i: (d, 0, 0))],
        out_specs=pl.BlockSpec((1, rows, MIX_W), lambda d, i: (d, chunk_of(d, i), 0)),
        scratch_shapes=[pltpu.VMEM((rows, 2 * S5_LANES), F32), pltpu.VMEM((8, 2 * S5_LANES), F32)],
        compiler_params=_cp("arbitrary", "arbitrary"),
        name="s5_scan",
    )(u_tm, a, bmat, cmat)

    tmf = _pow2_tile(2048, steps * 8)
    o = pl.pallas_call(
        _s5_finish_kernel,
        out_shape=jax.ShapeDtypeStruct((steps * 8, MIX_W), BF16),
        grid=(steps * 8 // tmf,),
        in_specs=[pl.BlockSpec((1, tmf, MIX_W), lambda i: (0, i, 0)),
                  pl.BlockSpec((1, tmf, MIX_W), lambda i: (1, i, 0)),
                  pl.BlockSpec((tmf, MIX_W), lambda i: (i, 0)),
                  pl.BlockSpec((1, MIX_W), lambda i: (0, 0)),
                  pl.BlockSpec((MIX_W, 2 * MIX_W), lambda i: (0, 0))],
        out_specs=pl.BlockSpec((tmf, MIX_W), lambda i: (i, 0)),
        compiler_params=_cp("parallel"),
        name="s5_finish",
    )(y, y, u_tm, d_skip.reshape(1, MIX_W).astype(F32), glu_w.astype(BF16))
    o = o.reshape(steps, 8, MIX_W)[:, :bsz]
    o_c = jnp.transpose(o[:ctx], (1, 0, 2)).reshape(bsz * ctx, MIX_W)
    o_l = jnp.transpose(o[ctx:], (1, 0, 2)).reshape(t_lat, MIX_W)
    return jnp.concatenate([o_l, o_c], axis=0)


def _rope_tables(seq, tm):
    rows = seq // GRID_W
    pos_r = jnp.repeat(jnp.arange(rows, dtype=F32), GRID_W)
    pos_c = jnp.tile(jnp.arange(GRID_W, dtype=F32), rows)
    inv = ROPE_BASE ** (-jnp.arange(ROPE_FREQS, dtype=F32) / ROPE_FREQS)
    ar, ac = pos_r[:, None] * inv, pos_c[:, None] * inv
    cos = jnp.concatenate([jnp.cos(ar), jnp.cos(ar), jnp.cos(ac), jnp.cos(ac)], axis=-1)
    sin = jnp.concatenate([-jnp.sin(ar), jnp.sin(ar), -jnp.sin(ac), jnp.sin(ac)], axis=-1)
    cos = jnp.concatenate([jnp.tile(cos, (1, 2)), jnp.ones((tm, 128), F32)], axis=0)
    sin = jnp.concatenate([jnp.tile(sin, (1, 2)), jnp.zeros((tm, 128), F32)], axis=0)
    return cos, sin


def _prep_consts():
    i = np.arange(MIX_W)
    bd = ((i[:, None] // HEAD_DIM) == (i[None, :] // HEAD_DIM)).astype(np.float32) / HEAD_DIM
    pm = (i[:, None] == (i[None, :] ^ ROPE_FREQS)).astype(np.float32)
    return jnp.asarray(bd, BF16), jnp.asarray(pm, BF16)


def _prep_kernel(cq_ref, dq_ref, ck_ref, dk_ref, cos_ref, sin_ref, qg_ref, kg_ref, bd_ref, pm_ref,
                 q1_ref, q2_ref, k1_ref, k2_ref):
    cos, sin = cos_ref[...], sin_ref[...]
    cos2 = jnp.concatenate([cos, cos], axis=-1)
    sin2 = jnp.concatenate([sin, sin], axis=-1)
    bd, pm = bd_ref[...], pm_ref[...]
    tm = cos.shape[0]
    lane = lax.broadcasted_iota(jnp.int32, (tm, 128), 1)

    def rms(x, g, n):
        ms = _dot((x * x).astype(BF16), bd[:n, :n])
        return x * lax.rsqrt(ms + EPS) * g

    def rope(y, c, s, n):
        return y * c + _dot(y.astype(BF16), pm[:n, :n]) * s

    def store_q(q, ref):
        q = q * ATTN_SCALE
        for kv in range(2):
            for g in range(2):
                half = q[:, 128 * kv:128 * (kv + 1)]
                if g != kv:
                    half = pltpu.roll(half, HEAD_DIM, 1)
                keep = (lane >= HEAD_DIM * kv) & (lane < HEAD_DIM * (kv + 1))
                ref[2 * kv + g] = jnp.where(keep, half, 0.0).astype(BF16)

    store_q(rope(rms(cq_ref[...].astype(F32), qg_ref[...], MIX_W), cos2, sin2, MIX_W), q1_ref)
    store_q(rope(dq_ref[...].astype(F32), cos2, sin2, MIX_W), q2_ref)
    k1_ref[...] = rope(rms(ck_ref[...].astype(F32), kg_ref[...], 128), cos, sin, 128).astype(BF16)
    k2_ref[...] = rope(dk_ref[...].astype(F32), cos, sin, 128).astype(BF16)


def _prep(p, qk_gain, t_lat, seq, tm):
    t = p.shape[0]
    cos, sin = _rope_tables(seq, tm)
    bd, pm = _prep_consts()
    qg = jnp.tile(qk_gain[0].astype(F32), N_HEADS).reshape(1, MIX_W)
    kg = jnp.tile(qk_gain[1].astype(F32), 2).reshape(1, 128)
    nt = seq // tm

    def tab(i):
        return (jnp.where(i * tm >= t_lat, nt, i % nt), 0)

    def const(shape):
        return pl.BlockSpec(shape, lambda i: (0,) * len(shape))

    qshape = jax.ShapeDtypeStruct((N_HEADS, t, 128), BF16)
    kshape = jax.ShapeDtypeStruct((t, 128), BF16)
    return pl.pallas_call(
        _prep_kernel,
        out_shape=(qshape, qshape, kshape, kshape),
        grid=(t // tm,),
        in_specs=[pl.BlockSpec((tm, MIX_W), lambda i: (i, C_CQ // MIX_W)),
                  pl.BlockSpec((tm, MIX_W), lambda i: (i, C_DQ // MIX_W)),
                  pl.BlockSpec((tm, 128), lambda i: (i, C_CK // 128)),
                  pl.BlockSpec((tm, 128), lambda i: (i, C_DK // 128)),
                  pl.BlockSpec((tm, 128), tab), pl.BlockSpec((tm, 128), tab),
                  const((1, MIX_W)), const((1, 128)), const((MIX_W, MIX_W)), const((MIX_W, MIX_W))],
        out_specs=(pl.BlockSpec((N_HEADS, tm, 128), lambda i: (0, i, 0)),
                   pl.BlockSpec((N_HEADS, tm, 128), lambda i: (0, i, 0)),
                   pl.BlockSpec((tm, 128), lambda i: (i, 0)),
                   pl.BlockSpec((tm, 128), lambda i: (i, 0))),
        compiler_params=_cp("parallel"),
        name="qk_prep",
    )(p, p, p, p, cos, sin, qg, kg, bd, pm)


def _pack_heads(o, tq):
    lane = lax.broadcasted_iota(jnp.int32, (tq, 128), 1)
    left = lane < HEAD_DIM
    o00, o01, o10, o11 = [o[h * tq:(h + 1) * tq] for h in range(N_HEADS)]
    out0 = jnp.where(left, o00, pltpu.roll(o01, HEAD_DIM, 1))
    out1 = jnp.where(left, pltpu.roll(o10, HEAD_DIM, 1), o11)
    return jnp.concatenate([out0, out1], axis=-1)


def _gattn_kernel(q_ref, kl_ref, vl_ref, kc_ref, vc_ref, o_ref, m_ref, l_ref, acc_ref,
                  *, tq, kvb, n_qb_lat, n_kvb):
    qb = pl.program_id(1)
    q = q_ref[...].reshape(N_HEADS * tq, 128)

    def update(k, v, first):
        s = _dot_nt(q, k)
        m_blk = jnp.max(s, axis=-1, keepdims=True)
        if first:
            m_new = m_blk
            p = jnp.exp(s - m_new)
            l_ref[...] = jnp.broadcast_to(jnp.sum(p, axis=-1, keepdims=True), l_ref.shape)
            acc_ref[...] = _dot(p.astype(BF16), v)
        else:
            m_old = m_ref[:, 0:1]
            m_new = jnp.maximum(m_old, m_blk)
            alpha = jnp.exp(m_old - m_new)
            p = jnp.exp(s - m_new)
            l_ref[...] = alpha * l_ref[...] + jnp.sum(p, axis=-1, keepdims=True)
            acc_ref[...] = alpha * acc_ref[...] + _dot(p.astype(BF16), v)
        m_ref[...] = jnp.broadcast_to(m_new, m_ref.shape)

    update(kc_ref[...], vc_ref[...], True)

    def body(j, carry):
        rows = pl.ds(pl.multiple_of(j * kvb, kvb), kvb)
        update(kl_ref[rows, :], vl_ref[rows, :], False)
        return carry

    lax.fori_loop(0, jnp.where(qb < n_qb_lat, n_kvb, 0), body, 0)
    o = acc_ref[...] / l_ref[...]
    o_ref[...] = _pack_heads(o, tq).astype(BF16)


def _gattn(qp, k, p, bsz, seq, ctx, with_ctx):
    t_lat = bsz * seq
    tq = ctx
    kvb = _pow2_tile(512, seq)
    n_qb_lat = seq // tq
    n_qb = n_qb_lat + (1 if with_ctx else 0)
    cb0 = t_lat // ctx
    t_out = t_lat + (bsz * ctx if with_ctx else 0)

    def qrow(b, i):
        return jnp.where(i < n_qb_lat, b * n_qb_lat + i, cb0 + b)

    kern = functools.partial(_gattn_kernel, tq=tq, kvb=kvb, n_qb_lat=n_qb_lat, n_kvb=seq // kvb)
    return pl.pallas_call(
        kern,
        out_shape=jax.ShapeDtypeStruct((t_out, MIX_W), BF16),
        grid=(bsz, n_qb),
        in_specs=[pl.BlockSpec((N_HEADS, tq, 128), lambda b, i: (0, qrow(b, i), 0)),
                  pl.BlockSpec((seq, 128), lambda b, i: (b, 0)),
                  pl.BlockSpec((seq, 128), lambda b, i: (b, C_CV // 128)),
                  pl.BlockSpec((ctx, 128), lambda b, i: (cb0 + b, 0)),
                  pl.BlockSpec((ctx, 128), lambda b, i: (cb0 + b, C_CV // 128))],
        out_specs=pl.BlockSpec((tq, MIX_W), lambda b, i: (qrow(b, i), 0)),
        scratch_shapes=[pltpu.VMEM((N_HEADS * tq, 128), F32)] * 3,
        compiler_params=_cp("parallel", "arbitrary"),
        name="global_attn",
    )(qp, k, p, k, p)


def _wattn_kernel(q_ref, kl_ref, vl_ref, kc_ref, vc_ref, sink_ref, o_ref, *, nb):
    w = WINDOW
    n = pl.program_id(1)
    is_lat = n < nb
    q = q_ref[...].reshape(N_HEADS * w, 128)
    rows = N_HEADS * w
    qi = lax.broadcasted_iota(jnp.int32, (rows, w), 0) & (w - 1)
    kj = lax.broadcasted_iota(jnp.int32, (rows, w), 1)

    def blk(ref, idx):
        return ref[pl.ds(pl.multiple_of(idx * w, w), w), :]

    i_prev = jnp.clip(n - 1, 0, nb - 1)
    i_cur = jnp.clip(n, 0, nb - 1)
    i_next = jnp.clip(n + 1, 0, nb - 1)
    off_prev = jnp.where(is_lat & (n >= 1), 0, w)
    off_cur = jnp.where(is_lat, 0, w)
    off_next = jnp.where(n + 1 < nb, 0, w)
    masks = [kj >= qi + off_prev, kj >= off_cur, kj <= qi - off_next]
    s_c = _dot_nt(q, kc_ref[...])
    s_w = [jnp.where(mk, _dot_nt(q, blk(kl_ref, i)), NEG_INF)
           for mk, i in zip(masks, (i_prev, i_cur, i_next))]
    sink = sink_ref[:, 0:1]
    m = jnp.maximum(jnp.max(s_c, axis=-1, keepdims=True), sink)
    for s in s_w:
        m = jnp.maximum(m, jnp.max(s, axis=-1, keepdims=True))
    p_c = jnp.exp(s_c - m)
    den = jnp.sum(p_c, axis=-1, keepdims=True) + jnp.exp(sink - m)
    acc = _dot(p_c.astype(BF16), vc_ref[...])
    for s, i in zip(s_w, (i_prev, i_cur, i_next)):
        p = jnp.exp(s - m)
        den = den + jnp.sum(p, axis=-1, keepdims=True)
        acc = acc + _dot(p.astype(BF16), blk(vl_ref, i))
    o_ref[...] = _pack_heads(acc / den, w).astype(BF16)


def _wattn(qp, k, p, sink, bsz, seq, ctx, with_ctx):
    t_lat = bsz * seq
    w = WINDOW
    nb = seq // w
    ncb = ctx // w
    n_qb = nb + (ncb if with_ctx else 0)
    cq0 = t_lat // w
    cb0 = t_lat // ctx
    t_out = t_lat + (bsz * ctx if with_ctx else 0)
    sink_rows = jnp.broadcast_to(jnp.repeat(sink.astype(F32), w)[:, None], (N_HEADS * w, 128))

    def qrow(b, i):
        return jnp.where(i < nb, b * nb + i, cq0 + b * ncb + (i - nb))

    return pl.pallas_call(
        functools.partial(_wattn_kernel, nb=nb),
        out_shape=jax.ShapeDtypeStruct((t_out, MIX_W), BF16),
        grid=(bsz, n_qb),
        in_specs=[pl.BlockSpec((N_HEADS, w, 128), lambda b, i: (0, qrow(b, i), 0)),
                  pl.BlockSpec((seq, 128), lambda b, i: (b, 0)),
                  pl.BlockSpec((seq, 128), lambda b, i: (b, C_DV // 128)),
                  pl.BlockSpec((ctx, 128), lambda b, i: (cb0 + b, 0)),
                  pl.BlockSpec((ctx, 128), lambda b, i: (cb0 + b, C_DV // 128)),
                  pl.BlockSpec((N_HEADS * w, 128), lambda b, i: (0, 0))],
        out_specs=pl.BlockSpec((w, MIX_W), lambda b, i: (qrow(b, i), 0)),
        compiler_params=_cp("parallel", "arbitrary"),
        name="window_attn",
    )(qp, k, p, k, p, sink_rows)


def _merge_kernel(ya_ref, yb_ref, yc_ref, yd_ref, gate_ref, x_ref, mod_ref, wbr_ref, wout_ref,
                  g2_ref, xo_ref, h2_ref):
    d = x_ref.shape[1]
    acc = None
    for n, ref in enumerate((ya_ref, yb_ref, yc_ref, yd_ref)):
        gate = jax.nn.sigmoid(gate_ref[:, n * d:(n + 1) * d].astype(F32))
        term = gate * _dot(ref[...], wbr_ref[n])
        acc = term if acc is None else acc + term
    x = x_ref[...] + mod_ref[0, 2:3, :] * _dot(acc.astype(BF16), wout_ref[...])
    xo_ref[...] = x
    ms = jnp.mean(x * x, axis=-1, keepdims=True)
    y = x * lax.rsqrt(ms + EPS) * g2_ref[...]
    h2_ref[...] = y * (1.0 + mod_ref[0, 4:5, :]) + mod_ref[0, 3:4, :]


def _merge(ys, p, x, mod, wbr, wout, g2, t_out, t_lat, seq, tm):
    d = x.shape[1]

    def row(width):
        return pl.BlockSpec((tm, width), lambda i: (i, 0))

    def const(shape):
        return pl.BlockSpec(shape, lambda i: (0,) * len(shape))

    return pl.pallas_call(
        _merge_kernel,
        out_shape=(jax.ShapeDtypeStruct((t_out, d), F32), jax.ShapeDtypeStruct((t_out, d), F32)),
        grid=(t_out // tm,),
        in_specs=[row(MIX_W)] * 4 + [row(4 * d), row(d),
                  pl.BlockSpec((1, 6, d), lambda i: (_mod_group(i * tm, t_lat, seq), 0, 0)),
                  const((4, MIX_W, d)), const((d, d)), const((1, d))],
        out_specs=(row(d), row(d)),
        compiler_params=_cp("parallel"),
        name="merge",
    )(*ys, p, x, mod, wbr, wout, g2.reshape(1, d))


def _ffn_dense_kernel(h_ref, x_ref, mod_ref, wg_ref, wu_ref, wo_ref, o_ref, hb_ref, acc_ref):
    j = pl.program_id(1)

    @pl.when(j == 0)
    def _():
        hb_ref[...] = h_ref[...].astype(BF16)
        acc_ref[...] = jnp.zeros_like(acc_ref)

    h = hb_ref[...]
    a = _silu(_dot(h, wg_ref[...])) * _dot(h, wu_ref[...])
    acc_ref[...] += _dot(a.astype(BF16), wo_ref[...])

    @pl.when(j == pl.num_programs(1) - 1)
    def _():
        o_ref[...] = x_ref[...] + mod_ref[0, 5:6, :] * acc_ref[...]


def _ffn_dense(h2, x, mod, w_in, w_out, t_lat, seq, tm):
    t, d = x.shape
    f = w_out.shape[0]
    tf = 256
    nf = f // tf
    return pl.pallas_call(
        _ffn_dense_kernel,
        out_shape=jax.ShapeDtypeStruct((t, d), F32),
        grid=(t // tm, nf),
        in_specs=[pl.BlockSpec((tm, d), lambda i, j: (i, 0)),
                  pl.BlockSpec((tm, d), lambda i, j: (i, 0)),
                  pl.BlockSpec((1, 6, d), lambda i, j: (_mod_group(i * tm, t_lat, seq), 0, 0)),
                  pl.BlockSpec((d, tf), lambda i, j: (0, j)),
                  pl.BlockSpec((d, tf), lambda i, j: (0, nf + j)),
                  pl.BlockSpec((tf, d), lambda i, j: (j, 0))],
        out_specs=pl.BlockSpec((tm, d), lambda i, j: (i, 0)),
        scratch_shapes=[pltpu.VMEM((tm, d), BF16), pltpu.VMEM((tm, d), F32)],
        compiler_params=_cp("parallel", "arbitrary"),
        name="ffn_dense",
    )(h2, x, mod, w_in, w_in, w_out)


def _router_kernel(h_ref, w_ref, e_ref, g1_ref, g2_ref):
    h = h_ref[...]
    h_hi = h.astype(BF16)
    h_lo = (h - h_hi.astype(F32)).astype(BF16)
    logits = _dot(h_hi, w_ref[0]) + _dot(h_lo, w_ref[0]) + _dot(h_hi, w_ref[1])
    lane = lax.broadcasted_iota(jnp.int32, logits.shape, 1)
    lane_f = lane.astype(F32)
    logits = jnp.where(lane < N_EXPERTS, logits, -jnp.inf)
    m1 = jnp.max(logits, axis=-1, keepdims=True)
    i1 = jnp.min(jnp.where(logits == m1, lane_f, 128.0), axis=-1, keepdims=True)
    rest = jnp.where(lane_f == i1, -jnp.inf, logits)
    m2 = jnp.max(rest, axis=-1, keepdims=True)
    i2 = jnp.min(jnp.where(rest == m2, lane_f, 128.0), axis=-1, keepdims=True)
    e2 = jnp.exp(m2 - m1)
    g1 = 1.0 / (1.0 + e2)
    e_ref[...] = jnp.where(lane == 0, i1, jnp.where(lane == 1, i2, 0.0)).astype(jnp.int32)
    g1_ref[...] = jnp.broadcast_to(g1, g1_ref.shape)
    g2_ref[...] = jnp.broadcast_to(e2 * g1, g2_ref.shape)


def _router(h2, router, tm):
    t, d = h2.shape
    r = jnp.zeros((d, 128), F32).at[:, :N_EXPERTS].set(router.astype(F32))
    r_hi = r.astype(BF16)
    r_lo = (r - r_hi.astype(F32)).astype(BF16)
    shp = jax.ShapeDtypeStruct((t, 128), F32)
    return pl.pallas_call(
        _router_kernel,
        out_shape=(jax.ShapeDtypeStruct((t, 128), jnp.int32), shp, shp),
        grid=(t // tm,),
        in_specs=[pl.BlockSpec((tm, d), lambda i: (i, 0)),
                  pl.BlockSpec((2, d, 128), lambda i: (0, 0, 0))],
        out_specs=(pl.BlockSpec((tm, 128), lambda i: (i, 0)),) * 3,
        compiler_params=_cp("parallel"),
        name="router",
    )(h2, jnp.stack([r_hi, r_lo]))


def _gather_rows_kernel(idx_ref, src_ref, o_ref, sem, *, tm):
    def issue(r, carry):
        pltpu.make_async_copy(src_ref.at[pl.ds(idx_ref[0, 0, r], 1)], o_ref.at[pl.ds(r, 1)], sem).start()
        return carry

    def drain(r, carry):
        pltpu.make_async_copy(src_ref.at[pl.ds(0, 1)], o_ref.at[pl.ds(r, 1)], sem).wait()
        return carry

    lax.fori_loop(0, tm, issue, 0)
    lax.fori_loop(0, tm, drain, 0)


def _gather_rows(src, idx, tm):
    n = idx.shape[0]
    d = src.shape[1]
    return pl.pallas_call(
        functools.partial(_gather_rows_kernel, tm=tm),
        out_shape=jax.ShapeDtypeStruct((n, d), src.dtype),
        grid=(n // tm,),
        in_specs=[pl.BlockSpec((1, 1, tm), lambda i: (i, 0, 0), memory_space=pltpu.SMEM),
                  pl.BlockSpec(memory_space=pl.ANY)],
        out_specs=pl.BlockSpec((tm, d), lambda i: (i, 0)),
        scratch_shapes=[pltpu.SemaphoreType.DMA(())],
        compiler_params=_cp("arbitrary"),
        name="gather_rows",
    )(idx.reshape(n // tm, 1, tm), src)


def _experts_kernel(te_ref, nu_ref, x_ref, wg_ref, wu_ref, wo_ref, o_ref, xb_ref, acc_ref):
    i, j = pl.program_id(0), pl.program_id(1)

    @pl.when(i < nu_ref[0])
    def _():
        @pl.when(j == 0)
        def _():
            xb_ref[...] = x_ref[...].astype(BF16)
            acc_ref[...] = jnp.zeros_like(acc_ref)

        x = xb_ref[...]
        a = _silu(_dot(x, wg_ref[0])) * _dot(x, wu_ref[0])
        acc_ref[...] += _dot(a.astype(BF16), wo_ref[0])

        @pl.when(j == pl.num_programs(1) - 1)
        def _():
            o_ref[...] = acc_ref[...]

    @pl.when((i >= nu_ref[0]) & (j == 0))
    def _():
        o_ref[...] = jnp.zeros_like(o_ref)


def _experts(x_rows, tile_e, n_used, w_in, w_out):
    n, d = x_rows.shape
    f = w_out.shape[1]
    tf = 512
    nf = f // tf
    tm = MOE_TILE
    grid_spec = pltpu.PrefetchScalarGridSpec(
        num_scalar_prefetch=2,
        grid=(n // tm, nf),
        in_specs=[pl.BlockSpec((tm, d), lambda i, j, te, nu: (i, 0)),
                  pl.BlockSpec((1, d, tf), lambda i, j, te, nu: (te[i], 0, j)),
                  pl.BlockSpec((1, d, tf), lambda i, j, te, nu: (te[i], 0, nf + j)),
                  pl.BlockSpec((1, tf, d), lambda i, j, te, nu: (te[i], j, 0))],
        out_specs=pl.BlockSpec((tm, d), lambda i, j, te, nu: (i, 0)),
        scratch_shapes=[pltpu.VMEM((tm, d), BF16), pltpu.VMEM((tm, d), F32)])
    return pl.pallas_call(
        _experts_kernel,
        out_shape=jax.ShapeDtypeStruct((n, d), F32),
        grid_spec=grid_spec,
        compiler_params=_cp("arbitrary", "arbitrary"),
        name="experts",
    )(tile_e, n_used, x_rows, w_in, w_in, w_out)


def _combine_kernel(idx_ref, y_ref, x_ref, mod_ref, g1_ref, g2_ref, o_ref, b1_ref, b2_ref, sem, *, tm):
    def issue(r, carry):
        pltpu.make_async_copy(y_ref.at[pl.ds(idx_ref[0, 0, r], 1)], b1_ref.at[pl.ds(r, 1)], sem).start()
        pltpu.make_async_copy(y_ref.at[pl.ds(idx_ref[0, 1, r], 1)], b2_ref.at[pl.ds(r, 1)], sem).start()
        return carry

    def drain(r, carry):
        pltpu.make_async_copy(y_ref.at[pl.ds(0, 1)], b1_ref.at[pl.ds(r, 1)], sem).wait()
        pltpu.make_async_copy(y_ref.at[pl.ds(0, 1)], b2_ref.at[pl.ds(r, 1)], sem).wait()
        return carry

    lax.fori_loop(0, tm, issue, 0)
    lax.fori_loop(0, tm, drain, 0)
    d = x_ref.shape[1]
    g1 = jnp.concatenate([g1_ref[...]] * (d // 128), axis=-1)
    g2 = jnp.concatenate([g2_ref[...]] * (d // 128), axis=-1)
    o_ref[...] = x_ref[...] + mod_ref[0, 5:6, :] * (g1 * b1_ref[...] + g2 * b2_ref[...])


def _combine(y_rows, dest, x, mod, g1, g2, t_lat, seq, tm):
    t, d = x.shape
    return pl.pallas_call(
        functools.partial(_combine_kernel, tm=tm),
        out_shape=jax.ShapeDtypeStruct((t, d), F32),
        grid=(t // tm,),
        in_specs=[pl.BlockSpec((1, 2, tm), lambda i: (i, 0, 0), memory_space=pltpu.SMEM),
                  pl.BlockSpec(memory_space=pl.ANY),
                  pl.BlockSpec((tm, d), lambda i: (i, 0)),
                  pl.BlockSpec((1, 6, d), lambda i: (_mod_group(i * tm, t_lat, seq), 0, 0)),
                  pl.BlockSpec((tm, 128), lambda i: (i, 0)),
                  pl.BlockSpec((tm, 128), lambda i: (i, 0))],
        out_specs=pl.BlockSpec((tm, d), lambda i: (i, 0)),
        scratch_shapes=[pltpu.VMEM((tm, d), F32), pltpu.VMEM((tm, d), F32), pltpu.SemaphoreType.DMA(())],
        compiler_params=_cp("arbitrary"),
        name="moe_combine",
    )(jnp.transpose(dest.reshape(t // tm, tm, 2), (0, 2, 1)), y_rows, x, mod, g1, g2)


def _moe(h2, x, mod, router, w_in, w_out, t_lat, seq, tm):
    t, d = h2.shape
    e_idx, g1, g2 = _router(h2, router, tm)
    e_flat = e_idx[:, :2].reshape(-1)
    onehot = (e_flat[:, None] == jnp.arange(N_EXPERTS, dtype=jnp.int32)[None, :]).astype(jnp.int32)
    csum = jnp.cumsum(onehot, axis=0)
    counts = csum[-1]
    padded = (counts + MOE_TILE - 1) // MOE_TILE * MOE_TILE
    ends = jnp.cumsum(padded)
    pstarts = ends - padded
    dest = jnp.sum(onehot * (pstarts[None, :] + csum - 1), axis=1).astype(jnp.int32)
    n_tiles = (2 * t + MOE_TILE - 1) // MOE_TILE + N_EXPERTS
    n_rows = n_tiles * MOE_TILE
    row_tok = jnp.zeros((n_rows,), jnp.int32).at[dest].set(jnp.arange(2 * t, dtype=jnp.int32) // 2)
    tile_start = jnp.arange(n_tiles, dtype=jnp.int32) * MOE_TILE
    tile_e = jnp.minimum(jnp.searchsorted(ends, tile_start, side='right'), N_EXPERTS - 1).astype(jnp.int32)
    n_used = (ends[-1:] // MOE_TILE).astype(jnp.int32)

    x_rows = _gather_rows(h2, row_tok, MOE_TILE)
    y_rows = _experts(x_rows, tile_e, n_used, w_in, w_out)
    return _combine(y_rows, dest, x, mod, g1, g2, t_lat, seq, tm)


def _final_norm_kernel(x_ref, g_ref, o_ref):
    x = x_ref[...]
    ms = jnp.mean(x * x, axis=-1, keepdims=True)
    o_ref[...] = x * lax.rsqrt(ms + EPS) * g_ref[...]


def _final_norm(x, g, tm):
    t, d = x.shape
    return pl.pallas_call(
        _final_norm_kernel,
        out_shape=jax.ShapeDtypeStruct((t, d), F32),
        grid=(t // tm,),
        in_specs=[pl.BlockSpec((tm, d), lambda i: (i, 0)), pl.BlockSpec((1, d), lambda i: (0, 0))],
        out_specs=pl.BlockSpec((tm, d), lambda i: (i, 0)),
        compiler_params=_cp("parallel"),
        name="final_norm",
    )(x, g.reshape(1, d))


def _proj_weights(w_in):
    o = {}
    acc = 0
    for name, size in (('a_z', 256), ('a_x', 256), ('a_b', 256), ('a_c', 256), ('a_dt', 8), ('b_u', 256),
                       ('c_q', 256), ('c_k', 128), ('c_v', 128), ('d_q', 256), ('d_k', 128), ('d_v', 128),
                       ('gates', 4096)):
        o[name] = (acc, size)
        acc += size
    order = ('gates', 'a_z', 'b_u', 'c_q', 'd_q', 'c_k', 'c_v', 'd_k', 'd_v', 'a_x', 'a_b', 'a_c')
    perm = np.concatenate([np.arange(o[n][0], o[n][0] + o[n][1]) for n in order])
    w = w_in[:, :, perm].astype(BF16)
    dt0 = o['a_dt'][0]
    wdt = jnp.pad(w_in[:, :, dt0:dt0 + 8], ((0, 0), (0, 0), (0, 120))).astype(BF16)
    return w, wdt


def kernel(x, c, ctx, c_ctx, norm1_g, norm2_g, ada_w, ada_b, w_in, ssd_conv_w, ssd_conv_b, ssd_a_log,
           ssd_dt_bias, ssd_d, ssd_norm_g, s5_lam_re, s5_lam_im, s5_log_step, s5_b_re, s5_b_im, s5_c_re,
           s5_c_im, s5_d, s5_glu_w, qk_norm_g, swa_sink, w_branch, w_out, ffn_w_in, ffn_w_out, moe_router,
           moe_w_in, moe_w_out, final_norm_g):
    bsz, seq, d = x.shape
    n_ctx = ctx.shape[1]
    depth = w_in.shape[0]
    t_lat, t_ctx = bsz * seq, bsz * n_ctx
    tm = _pow2_tile(1024, seq, t_ctx)
    tm_small = _pow2_tile(512, seq, t_ctx)

    cvec = jnp.zeros((16, d), F32).at[0].set(c_ctx).at[1:1 + bsz].set(c)
    mod = _adaln(cvec, ada_w, ada_b).reshape(depth, 16, 6, d)
    wp, wdt = _proj_weights(w_in)
    wbr = w_branch.astype(BF16)
    wo = w_out.astype(BF16)
    ffn_in, ffn_out = ffn_w_in.astype(BF16), ffn_w_out.astype(BF16)
    moe_in, moe_out = moe_w_in.astype(BF16), moe_w_out.astype(BF16)

    xx = jnp.concatenate([x.reshape(t_lat, d), ctx.reshape(t_ctx, d)], axis=0)
    for l in range(depth):
        with_ctx = l < depth - 1
        t_out = t_lat + (t_ctx if with_ctx else 0)
        p, pdt = _inproj(xx, norm1_g[l], mod[l], wp[l], wdt[l], t_lat, seq, tm)
        ya = _ssd(p, pdt, ssd_conv_w[l], ssd_conv_b[l], ssd_a_log[l], ssd_dt_bias[l], ssd_d[l],
                  ssd_norm_g[l], bsz, seq, n_ctx)
        yb = _s5(p, s5_lam_re[l], s5_lam_im[l], s5_log_step[l], s5_b_re[l], s5_b_im[l], s5_c_re[l],
                 s5_c_im[l], s5_d[l], s5_glu_w[l], bsz, seq, n_ctx)
        q1, q2, k1, k2 = _prep(p, qk_norm_g[l], t_lat, seq, tm)
        yc = _gattn(q1, k1, p, bsz, seq, n_ctx, with_ctx)
        yd = _wattn(q2, k2, p, swa_sink[l], bsz, seq, n_ctx, with_ctx)
        xx, h2 = _merge((ya, yb, yc, yd), p, xx, mod[l], wbr[l], wo[l], norm2_g[l], t_out, t_lat, seq,
                        tm_small)
        if l % 2 == 0:
            xx = _ffn_dense(h2, xx, mod[l], ffn_in[l // 2], ffn_out[l // 2], t_lat, seq, tm)
        else:
            xx = _moe(h2, xx, mod[l], moe_router[l // 2], moe_in[l // 2], moe_out[l // 2], t_lat, seq,
                      tm_small)
    out = _final_norm(xx[:t_lat], final_norm_g, tm)
    return out.reshape(bsz, seq, d)
```

```python
import functools

import numpy as np
import jax
import jax.numpy as jnp
from jax import lax
from jax.experimental import pallas as pl
from jax.experimental.pallas import tpu as pltpu

F32 = jnp.float32
BF16 = jnp.bfloat16

EPS = 1e-6
NEG_INF = -1e30
GRID_W = 64
MIX_W = 256
HEAD_DIM = 64
N_HEADS = 4
ATTN_SCALE = HEAD_DIM ** -0.5
SSD_STATE = 128
SSD_CONV = 5
CHUNK = 128
HALO = 16
S5_GROUPS = 16
S5_GROUP = 16
S5_STATE = 64
S5_LANES = S5_GROUPS * S5_STATE
S5_STEPS = 64
ROPE_BASE = 10000.0
ROPE_FREQS = 16
WINDOW = 128
KT_UNIT = 256
N_EXPERTS = 8
MOE_TILE = 512
VMEM_LIMIT = 56 * 1024 * 1024

C_GATES, C_Z, C_BU, C_CQ, C_DQ = 0, 4096, 4352, 4608, 4864
C_CK, C_CV, C_DK, C_DV = 5120, 5248, 5376, 5504
C_AX, C_AB, C_AC = 5632, 5888, 6144
P_COLS = 6400
PROJ_TN = 1280


def _cp(*sem):
    return pltpu.CompilerParams(dimension_semantics=sem, vmem_limit_bytes=VMEM_LIMIT)


def _pow2_tile(cap, *dims):
    t = 1
    while t * 2 <= cap and all(d % (t * 2) == 0 for d in dims):
        t *= 2
    return t


def _dot(a, b):
    return jnp.dot(a, b, preferred_element_type=F32)


def _dot_nt(a, b):
    return lax.dot_general(a, b, (((1,), (1,)), ((), ())), preferred_element_type=F32)


def _dot_tn(a, b):
    return lax.dot_general(a, b, (((0,), (0,)), ((), ())), preferred_element_type=F32)


def _split3(x):
    hi = x.astype(BF16)
    r1 = x - hi.astype(F32)
    mid = r1.astype(BF16)
    lo = (r1 - mid.astype(F32)).astype(BF16)
    return hi, mid, lo


def _silu(x):
    return x * jax.nn.sigmoid(x)


def _adaln_kernel(c_ref, w_ref, b_ref, o_ref):
    c = c_ref[...]
    o_ref[0] = jnp.dot(_silu(c), w_ref[0], preferred_element_type=F32,
                       precision=lax.Precision.HIGHEST) + b_ref[0]


def _adaln(cvec, ada_w, ada_b):
    depth, d, n = ada_w.shape
    tn = 1024
    return pl.pallas_call(
        _adaln_kernel,
        out_shape=jax.ShapeDtypeStruct((depth, 16, n), F32),
        grid=(depth, n // tn),
        in_specs=[pl.BlockSpec((16, d), lambda l, j: (0, 0)),
                  pl.BlockSpec((1, d, tn), lambda l, j: (l, 0, j)),
                  pl.BlockSpec((1, 1, tn), lambda l, j: (l, 0, j))],
        out_specs=pl.BlockSpec((1, 16, tn), lambda l, j: (l, 0, j)),
        compiler_params=_cp("parallel", "parallel"),
        name="adaln",
    )(cvec, ada_w, ada_b.reshape(depth, 1, n))


def _mod_group(row0, t_lat, seq):
    return jnp.where(row0 >= t_lat, 0, 1 + row0 // seq)


def _inproj_kernel(x_ref, g_ref, mod_ref, w_ref, wdt_ref, o_ref, odt_ref, h_ref):
    @pl.when(pl.program_id(1) == 0)
    def _():
        x = x_ref[...]
        ms = jnp.mean(x * x, axis=-1, keepdims=True)
        y = x * lax.rsqrt(ms + EPS) * g_ref[...]
        h = (y * (1.0 + mod_ref[0, 1:2, :]) + mod_ref[0, 0:1, :]).astype(BF16)
        h_ref[...] = h
        odt_ref[...] = _dot(h, wdt_ref[...])

    o_ref[...] = _dot(h_ref[...], w_ref[...]).astype(BF16)


def _inproj(x, g, mod, w, wdt, t_lat, seq, tm):
    t, d = x.shape
    n = w.shape[1]
    tn = PROJ_TN
    return pl.pallas_call(
        _inproj_kernel,
        out_shape=(jax.ShapeDtypeStruct((t, n), BF16), jax.ShapeDtypeStruct((t, 128), F32)),
        grid=(t // tm, n // tn),
        in_specs=[pl.BlockSpec((tm, d), lambda i, j: (i, 0)),
                  pl.BlockSpec((1, d), lambda i, j: (0, 0)),
                  pl.BlockSpec((1, 6, d), lambda i, j: (_mod_group(i * tm, t_lat, seq), 0, 0)),
                  pl.BlockSpec((d, tn), lambda i, j: (0, j)),
                  pl.BlockSpec((d, 128), lambda i, j: (0, 0))],
        out_specs=(pl.BlockSpec((tm, tn), lambda i, j: (i, j)),
                   pl.BlockSpec((tm, 128), lambda i, j: (i, 0))),
        scratch_shapes=[pltpu.VMEM((tm, d), BF16)],
        compiler_params=_cp("parallel", "arbitrary"),
        name="inproj",
    )(x, g.reshape(1, d), mod, w, wdt)


def _ssd_consts():
    r = np.arange(CHUNK)
    tri_l = (r[None, :] <= r[:, None]).astype(np.float32)
    tri_u = tri_l.T.copy()
    shifts = np.zeros((4, CHUNK, CHUNK + 2 * HALO), np.float32)
    for n, k in enumerate((0, 1, 3, 4)):
        shifts[n, r, r + HALO + k - 2] = 1.0
    return (jnp.asarray(np.stack([tri_l, tri_u]), BF16), jnp.asarray(shifts, BF16))


def _ssd_kernel(zl_ref, xl_ref, bl_ref, cl_ref, dtl_ref, zc_ref, xc_ref, bc_ref, cc_ref, dtc_ref,
                cw_ref, cb_ref, an_ref, bias_ref, dsk_ref, ng_ref, tri_ref, sh_ref,
                yl_ref, yc_ref, act_l, act_c, yf_l, yf_c, st_ref, *, n_lat, n_ctx):
    q = CHUNK
    lane128 = lax.broadcasted_iota(jnp.int32, (q, 128), 1)
    lane256 = lax.broadcasted_iota(jnp.int32, (q, MIX_W), 1)
    lane256r = lax.broadcasted_iota(jnp.int32, (1, MIX_W), 1)
    row_i = lax.broadcasted_iota(jnp.int32, (q, q), 0)
    col_i = lax.broadcasted_iota(jnp.int32, (q, q), 1)
    head_masks = [(lane256 >= HEAD_DIM * h) & (lane256 < HEAD_DIM * (h + 1)) for h in range(N_HEADS)]
    head_masks_r = [(lane256r >= HEAD_DIM * h) & (lane256r < HEAD_DIM * (h + 1)) for h in range(N_HEADS)]
    group_masks = [lane256 < 128, lane256 >= 128]

    def conv_act(x_ref, b_ref, c_ref, n_chunks, c):
        def rows(ref, start, size):
            return ref[pl.ds(start, size), :]

        start = c * q
        if isinstance(c, int):
            p0, n0 = max(start - HALO, 0), min(start + q, n_chunks * q - HALO)
            pf, nf = float(c > 0), float(c < n_chunks - 1)
        else:
            start = pl.multiple_of(start, q)
            p0 = pl.multiple_of(jnp.maximum(start - HALO, 0), HALO)
            n0 = pl.multiple_of(jnp.minimum(start + q, n_chunks * q - HALO), HALO)
            pf, nf = (c > 0).astype(F32), (c < n_chunks - 1).astype(F32)
        parts = []
        for ref in (x_ref, b_ref, c_ref):
            prev = (rows(ref, p0, HALO).astype(F32) * pf).astype(BF16)
            nxt = (rows(ref, n0, HALO).astype(F32) * nf).astype(BF16)
            parts.append(jnp.concatenate([prev, rows(ref, start, q), nxt], axis=0))
        ext = jnp.concatenate(parts, axis=1)
        cur = ext[HALO:HALO + q].astype(F32)
        acc = cur * cw_ref[2:3, :] + cb_ref[...]
        for n, k in enumerate((0, 1, 3, 4)):
            acc = acc + _dot(sh_ref[n], ext) * cw_ref[k:k + 1, :]
        return _silu(acc)

    def chunk(seg, c, d):
        z_ref, x_ref, b_ref, c_ref, dt_ref, act_ref, yf_ref, y_ref, n_chunks = seg
        start = c * q if isinstance(c, int) else pl.multiple_of(c * q, q)
        if d == 0:
            act = conv_act(x_ref, b_ref, c_ref, n_chunks, c)
            act_ref[pl.ds(start, q), :] = act.astype(BF16)
        act = act_ref[pl.ds(start, q), :]
        xs = act[:, 0:MIX_W].astype(F32)
        bm = act[:, MIX_W:2 * MIX_W]
        cm = act[:, 2 * MIX_W:3 * MIX_W]

        dt_n = jax.nn.softplus(dt_ref[pl.ds(start, q), :] + bias_ref[...])
        la_n = dt_n * an_ref[...]
        hi, mid, lo = _split3(la_n)
        tri = tri_ref[d]
        cs_n = _dot(tri, hi) + _dot(tri, mid) + _dot(tri, lo)
        cs_t = cs_n.T
        edge = q - 1 if d == 0 else 0
        tri_mask = (col_i <= row_i) if d == 0 else (col_i >= row_i)

        dt_full = jnp.zeros((q, MIX_W), F32)
        dte_full = jnp.zeros((q, MIX_W), F32)
        ecs_full = jnp.zeros((q, MIX_W), F32)
        tot_full = jnp.zeros((1, MIX_W), F32)
        decay = []
        for h in range(N_HEADS):
            sel = lane128 == (N_HEADS * d + h)
            cs_col = jnp.sum(jnp.where(sel, cs_n, 0.0), axis=-1, keepdims=True)
            dt_col = jnp.sum(jnp.where(sel, dt_n, 0.0), axis=-1, keepdims=True)
            tot = cs_col[edge:edge + 1, :]
            cs_row = cs_t[N_HEADS * d + h:N_HEADS * d + h + 1, :]
            decay.append(jnp.where(tri_mask, jnp.exp(cs_col - cs_row), 0.0))
            dt_full = jnp.where(head_masks[h], dt_col, dt_full)
            dte_full = jnp.where(head_masks[h], jnp.exp(tot - cs_col), dte_full)
            ecs_full = jnp.where(head_masks[h], jnp.exp(cs_col), ecs_full)
            tot_full = jnp.where(head_masks_r[h], jnp.exp(tot), tot_full)

        xdt = xs * dt_full
        state = st_ref[...]
        y = jnp.zeros((q, MIX_W), F32)
        y_off = jnp.zeros((q, MIX_W), F32)
        upd = jnp.zeros((SSD_STATE, MIX_W), F32)
        xdte = xdt * dte_full
        for g in range(2):
            bg = bm[:, 128 * g:128 * (g + 1)]
            cg = cm[:, 128 * g:128 * (g + 1)]
            cb = _dot_nt(cg, bg)
            for h in (2 * g, 2 * g + 1):
                m = (cb * decay[h]).astype(BF16)
                y = y + _dot(m, jnp.where(head_masks[h], xdt, 0.0).astype(BF16))
            y_off = y_off + _dot(cg, jnp.where(group_masks[g], state, 0.0).astype(BF16))
            upd = upd + _dot_tn(bg, jnp.where(group_masks[g], xdte, 0.0).astype(BF16))
        y = y + y_off * ecs_full
        st_ref[...] = state * tot_full + upd

        if d == 0:
            yf_ref[pl.ds(start, q), :] = y
        else:
            y = y + yf_ref[pl.ds(start, q), :] + dsk_ref[...] * xs
            y = y * _silu(z_ref[pl.ds(start, q), :].astype(F32))
            ms = jnp.mean(y * y, axis=-1, keepdims=True)
            y_ref[pl.ds(start, q), :] = (y * lax.rsqrt(ms + EPS) * ng_ref[...]).astype(BF16)

    seg_l = (zl_ref, xl_ref, bl_ref, cl_ref, dtl_ref, act_l, yf_l, yl_ref, n_lat)
    seg_c = (zc_ref, xc_ref, bc_ref, cc_ref, dtc_ref, act_c, yf_c, yc_ref, n_ctx)

    for d in range(2):
        st_ref[...] = jnp.zeros_like(st_ref)
        order = range(n_ctx) if d == 0 else range(n_ctx - 1, -1, -1)
        for c in order:
            chunk(seg_c, c, d)

        def body(i, carry, d=d):
            chunk(seg_l, i if d == 0 else n_lat - 1 - i, d)
            return carry

        lax.fori_loop(0, n_lat, body, 0)


def _ssd(p, pdt, conv_w, conv_b, a_log, dt_bias, d_skip, norm_g, bsz, seq, ctx):
    t = p.shape[0]
    n_lat, n_ctx = seq // CHUNK, ctx // CHUNK
    cb0 = (bsz * seq) // ctx
    tri, shifts = _ssd_consts()
    cw = jnp.zeros((8, 3 * MIX_W), F32).at[:SSD_CONV].set(conv_w)
    a_n = jnp.zeros((1, 128), F32).at[0, :8].set(-jnp.exp(a_log.astype(F32)).reshape(8))
    bias_n = jnp.zeros((1, 128), F32).at[0, :8].set(dt_bias.astype(F32).reshape(8))
    dsk = jnp.repeat(d_skip.astype(F32), HEAD_DIM).reshape(1, MIX_W)

    def lat(col):
        return pl.BlockSpec((seq, MIX_W), lambda b, col=col: (b, col // MIX_W))

    def cx(col):
        return pl.BlockSpec((ctx, MIX_W), lambda b, col=col: (cb0 + b, col // MIX_W))

    def full(shape):
        return pl.BlockSpec(shape, lambda b: (0,) * len(shape))

    kern = functools.partial(_ssd_kernel, n_lat=n_lat, n_ctx=n_ctx)
    yl, yc = pl.pallas_call(
        kern,
        out_shape=(jax.ShapeDtypeStruct((bsz * seq, MIX_W), BF16),
                   jax.ShapeDtypeStruct((bsz * ctx, MIX_W), BF16)),
        grid=(bsz,),
        in_specs=[lat(C_Z), lat(C_AX), lat(C_AB), lat(C_AC),
                  pl.BlockSpec((seq, 128), lambda b: (b, 0)),
                  cx(C_Z), cx(C_AX), cx(C_AB), cx(C_AC),
                  pl.BlockSpec((ctx, 128), lambda b: (cb0 + b, 0)),
                  full((8, 3 * MIX_W)), full((1, 3 * MIX_W)), full((1, 128)), full((1, 128)),
                  full((1, MIX_W)), full((1, MIX_W)), full((2, CHUNK, CHUNK)),
                  full((4, CHUNK, CHUNK + 2 * HALO))],
        out_specs=(pl.BlockSpec((seq, MIX_W), lambda b: (b, 0)),
                   pl.BlockSpec((ctx, MIX_W), lambda b: (b, 0))),
        scratch_shapes=[pltpu.VMEM((seq, 3 * MIX_W), BF16), pltpu.VMEM((ctx, 3 * MIX_W), BF16),
                        pltpu.VMEM((seq, MIX_W), F32), pltpu.VMEM((ctx, MIX_W), F32),
                        pltpu.VMEM((SSD_STATE, MIX_W), F32)],
        compiler_params=_cp("parallel"),
        name="ssd",
    )(p, p, p, p, pdt, p, p, p, p, pdt, cw, conv_b.reshape(1, -1).astype(F32), a_n, bias_n,
      dsk, norm_g.reshape(1, MIX_W).astype(F32), tri, shifts)
    return jnp.concatenate([yl, yc], axis=0)


def _s5_discretize(lam_re, lam_im, log_step, b_re, b_im):
    step = jnp.exp(log_step.astype(F32))[:, None]
    lr = jnp.minimum(lam_re.astype(F32), -1e-4)
    li = lam_im.astype(F32)
    mag = jnp.exp(lr * step)
    ang = li * step
    ab_re, ab_im = mag * jnp.cos(ang), mag * jnp.sin(ang)
    den = lr * lr + li * li
    f_re = ((ab_re - 1.0) * lr + ab_im * li) / den
    f_im = (ab_im * lr - (ab_re - 1.0) * li) / den
    br, bi = b_re.astype(F32), b_im.astype(F32)
    bb_re = f_re[..., None] * br - f_im[..., None] * bi
    bb_im = f_re[..., None] * bi + f_im[..., None] * br
    return ab_re, ab_im, bb_re, bb_im


def _s5_mats(lam_re, lam_im, log_step, b_re, b_im, c_re, c_im):
    eye = jnp.eye(S5_GROUPS, dtype=F32)
    a_all, b_all, c_all = [], [], []
    for d in range(2):
        ab_re, ab_im, bb_re, bb_im = _s5_discretize(lam_re[d], lam_im[d], log_step[d], b_re[d], b_im[d])
        bm = [jnp.einsum('gnp,gh->gphn', m, eye).reshape(MIX_W, S5_LANES) for m in (bb_re, bb_im)]
        cm = [jnp.einsum('gpn,gh->gnhp', m.astype(F32), eye).reshape(S5_LANES, MIX_W)
              for m in (c_re[d], c_im[d])]
        b_all.append(jnp.concatenate(bm, axis=1))
        c_all.append(jnp.concatenate([cm[0], -cm[1]], axis=0))
        a_all.append(jnp.concatenate([ab_re.reshape(1, S5_LANES), ab_im.reshape(1, S5_LANES)], axis=1))
    a = jnp.broadcast_to(jnp.stack(a_all), (2, 8, 2 * S5_LANES))
    return a, jnp.stack(b_all).astype(BF16), jnp.stack(c_all).astype(BF16)


def _s5_kernel(u_ref, a_ref, b_ref, c_ref, y_ref, buf_ref, s_ref):
    d = pl.program_id(0)
    n = S5_LANES

    @pl.when(pl.program_id(1) == 0)
    def _():
        s_ref[...] = jnp.zeros_like(s_ref)

    buf_ref[...] = _dot(u_ref[...], b_ref[0])
    a_re = a_ref[0, :, 0:n]
    a_im = a_ref[0, :, n:2 * n]

    def step(j, carry):
        s_re, s_im = carry
        jj = jnp.where(d == 0, j, S5_STEPS - 1 - j)
        rows = pl.ds(pl.multiple_of(jj * 8, 8), 8)
        n_re = a_re * s_re - a_im * s_im + buf_ref[rows, 0:n]
        n_im = a_re * s_im + a_im * s_re + buf_ref[rows, n:2 * n]
        buf_ref[rows, 0:n] = n_re
        buf_ref[rows, n:2 * n] = n_im
        return n_re, n_im

    s_re, s_im = lax.fori_loop(0, S5_STEPS, step, (s_ref[:, 0:n], s_ref[:, n:2 * n]), unroll=4)
    s_ref[:, 0:n] = s_re
    s_ref[:, n:2 * n] = s_im
    y_ref[0] = _dot(buf_ref[...].astype(BF16), c_ref[0])


def _s5_finish_kernel(yf_ref, yb_ref, u_ref, d_ref, w_ref, o_ref):
    y = yf_ref[0] + yb_ref[0] + d_ref[...] * u_ref[...].astype(F32)
    v = jax.nn.gelu(y, approximate=True).astype(BF16)
    r = _dot(v, w_ref[...])
    o_ref[...] = (r[:, 0:MIX_W] * jax.nn.sigmoid(r[:, MIX_W:2 * MIX_W])).astype(BF16)


def _s5(p, lam_re, lam_im, log_step, b_re, b_im, c_re, c_im, d_skip, glu_w, bsz, seq, ctx):
    assert bsz <= 8
    t_lat = bsz * seq
    u = p[:, C_BU:C_BU + MIX_W]
    u_l = jnp.transpose(u[:t_lat].reshape(bsz, seq, MIX_W), (1, 0, 2))
    u_c = jnp.transpose(u[t_lat:].reshape(bsz, ctx, MIX_W), (1, 0, 2))
    u_tm = jnp.concatenate([u_c, u_l], axis=0)
    if bsz < 8:
        u_tm = jnp.pad(u_tm, ((0, 0), (0, 8 - bsz), (0, 0)))
    steps = seq + ctx
    u_tm = u_tm.reshape(steps * 8, MIX_W)
    a, bmat, cmat = _s5_mats(lam_re, lam_im, log_step, b_re, b_im, c_re, c_im)
    rows = S5_STEPS * 8
    nc, ncc = steps // S5_STEPS, ctx // S5_STEPS

    def chunk_of(d, i):
        return jnp.where(d == 0, i, jnp.where(i < ncc, ncc - 1 - i, nc + ncc - 1 - i))

    y = pl.pallas_call(
        _s5_kernel,
        out_shape=jax.ShapeDtypeStruct((2, steps * 8, MIX_W), F32),
        grid=(2, nc),
        in_specs=[pl.BlockSpec((rows, MIX_W), lambda d, i: (chunk_of(d, i), 0)),
                  pl.BlockSpec((1, 8, 2 * S5_LANES), lambda d, i: (d, 0, 0)),
                  pl.BlockSpec((1, MIX_W, 2 * S5_LANES), lambda d, i: (d, 0, 0)),
                  pl.BlockSpec((1, 2 * S5_LANES, MIX_W), lambda d, i: (d, 0, 0))],
        out_specs=pl.BlockSpec((1, rows, MIX_W), lambda d, i: (d, chunk_of(d, i), 0)),
        scratch_shapes=[pltpu.VMEM((rows, 2 * S5_LANES), F32), pltpu.VMEM((8, 2 * S5_LANES), F32)],
        compiler_params=_cp("arbitrary", "arbitrary"),
        name="s5_scan",
    )(u_tm, a, bmat, cmat)

    tmf = _pow2_tile(2048, steps * 8)
    o = pl.pallas_call(
        _s5_finish_kernel,
        out_shape=jax.ShapeDtypeStruct((steps * 8, MIX_W), BF16),
        grid=(steps * 8 // tmf,),
        in_specs=[pl.BlockSpec((1, tmf, MIX_W), lambda i: (0, i, 0)),
                  pl.BlockSpec((1, tmf, MIX_W), lambda i: (1, i, 0)),
                  pl.BlockSpec((tmf, MIX_W), lambda i: (i, 0)),
                  pl.BlockSpec((1, MIX_W), lambda i: (0, 0)),
                  pl.BlockSpec((MIX_W, 2 * MIX_W), lambda i: (0, 0))],
        out_specs=pl.BlockSpec((tmf, MIX_W), lambda i: (i, 0)),
        compiler_params=_cp("parallel"),
        name="s5_finish",
    )(y, y, u_tm, d_skip.reshape(1, MIX_W).astype(F32), glu_w.astype(BF16))
    o = o.reshape(steps, 8, MIX_W)[:, :bsz]
    o_c = jnp.transpose(o[:ctx], (1, 0, 2)).reshape(bsz * ctx, MIX_W)
    o_l = jnp.transpose(o[ctx:], (1, 0, 2)).reshape(t_lat, MIX_W)
    return jnp.concatenate([o_l, o_c], axis=0)


def _rope_tables(seq, tm):
    rows = seq // GRID_W
    pos_r = jnp.repeat(jnp.arange(rows, dtype=F32), GRID_W)
    pos_c = jnp.tile(jnp.arange(GRID_W, dtype=F32), rows)
    inv = ROPE_BASE ** (-jnp.arange(ROPE_FREQS, dtype=F32) / ROPE_FREQS)
    ar, ac = pos_r[:, None] * inv, pos_c[:, None] * inv
    cos = jnp.concatenate([jnp.cos(ar), jnp.cos(ar), jnp.cos(ac), jnp.cos(ac)], axis=-1)
    sin = jnp.concatenate([-jnp.sin(ar), jnp.sin(ar), -jnp.sin(ac), jnp.sin(ac)], axis=-1)
    cos = jnp.concatenate([jnp.tile(cos, (1, 2)), jnp.ones((tm, 128), F32)], axis=0)
    sin = jnp.concatenate([jnp.tile(sin, (1, 2)), jnp.zeros((tm, 128), F32)], axis=0)
    return cos, sin


def _prep_consts():
    i = np.arange(MIX_W)
    bd = ((i[:, None] // HEAD_DIM) == (i[None, :] // HEAD_DIM)).astype(np.float32) / HEAD_DIM
    pm = (i[:, None] == (i[None, :] ^ ROPE_FREQS)).astype(np.float32)
    return jnp.asarray(bd, BF16), jnp.asarray(pm, BF16)


def _prep_kernel(cq_ref, dq_ref, ck_ref, dk_ref, cos_ref, sin_ref, qg_ref, kg_ref, bd_ref, pm_ref,
                 q1_ref, q2_ref, k1_ref, k2_ref):
    cos, sin = cos_ref[...], sin_ref[...]
    cos2 = jnp.concatenate([cos, cos], axis=-1)
    sin2 = jnp.concatenate([sin, sin], axis=-1)
    bd, pm = bd_ref[...], pm_ref[...]
    tm = cos.shape[0]
    lane = lax.broadcasted_iota(jnp.int32, (tm, 128), 1)

    def rms(x, g, n):
        ms = _dot((x * x).astype(BF16), bd[:n, :n])
        return x * lax.rsqrt(ms + EPS) * g

    def rope(y, c, s, n):
        return y * c + _dot(y.astype(BF16), pm[:n, :n]) * s

    def store_q(q, ref):
        q = q * ATTN_SCALE
        for kv in range(2):
            for g in range(2):
                half = q[:, 128 * kv:128 * (kv + 1)]
                if g != kv:
                    half = pltpu.roll(half, HEAD_DIM, 1)
                keep = (lane >= HEAD_DIM * kv) & (lane < HEAD_DIM * (kv + 1))
                ref[2 * kv + g] = jnp.where(keep, half, 0.0).astype(BF16)

    store_q(rope(rms(cq_ref[...].astype(F32), qg_ref[...], MIX_W), cos2, sin2, MIX_W), q1_ref)
    store_q(rope(dq_ref[...].astype(F32), cos2, sin2, MIX_W), q2_ref)
    k1 = rope(rms(ck_ref[...].astype(F32), kg_ref[...], 128), cos, sin, 128)
    for u in range(tm // KT_UNIT):
        k1_ref[u] = k1[u * KT_UNIT:(u + 1) * KT_UNIT].T.astype(BF16)
    k2 = rope(dk_ref[...].astype(F32), cos, sin, 128)
    for u in range(tm // WINDOW):
        k2_ref[u] = k2[u * WINDOW:(u + 1) * WINDOW].T.astype(BF16)


def _prep(p, qk_gain, t_lat, seq, tm):
    t = p.shape[0]
    cos, sin = _rope_tables(seq, tm)
    bd, pm = _prep_consts()
    qg = jnp.tile(qk_gain[0].astype(F32), N_HEADS).reshape(1, MIX_W)
    kg = jnp.tile(qk_gain[1].astype(F32), 2).reshape(1, 128)
    nt = seq // tm

    def tab(i):
        return (jnp.where(i * tm >= t_lat, nt, i % nt), 0)

    def const(shape):
        return pl.BlockSpec(shape, lambda i: (0,) * len(shape))

    qshape = jax.ShapeDtypeStruct((N_HEADS, t, 128), BF16)
    ktshape = jax.ShapeDtypeStruct((t // KT_UNIT, 128, KT_UNIT), BF16)
    kt2shape = jax.ShapeDtypeStruct((t // WINDOW, 128, WINDOW), BF16)
    return pl.pallas_call(
        _prep_kernel,
        out_shape=(qshape, qshape, ktshape, kt2shape),
        grid=(t // tm,),
        in_specs=[pl.BlockSpec((tm, MIX_W), lambda i: (i, C_CQ // MIX_W)),
                  pl.BlockSpec((tm, MIX_W), lambda i: (i, C_DQ // MIX_W)),
                  pl.BlockSpec((tm, 128), lambda i: (i, C_CK // 128)),
                  pl.BlockSpec((tm, 128), lambda i: (i, C_DK // 128)),
                  pl.BlockSpec((tm, 128), tab), pl.BlockSpec((tm, 128), tab),
                  const((1, MIX_W)), const((1, 128)), const((MIX_W, MIX_W)), const((MIX_W, MIX_W))],
        out_specs=(pl.BlockSpec((N_HEADS, tm, 128), lambda i: (0, i, 0)),
                   pl.BlockSpec((N_HEADS, tm, 128), lambda i: (0, i, 0)),
                   pl.BlockSpec((tm // KT_UNIT, 128, KT_UNIT), lambda i: (i, 0, 0)),
                   pl.BlockSpec((tm // WINDOW, 128, WINDOW), lambda i: (i, 0, 0))),
        compiler_params=_cp("parallel"),
        name="qk_prep",
    )(p, p, p, p, cos, sin, qg, kg, bd, pm)


def _pack_heads(o, tq):
    lane = lax.broadcasted_iota(jnp.int32, (tq, 128), 1)
    left = lane < HEAD_DIM
    o00, o01, o10, o11 = [o[h * tq:(h + 1) * tq] for h in range(N_HEADS)]
    out0 = jnp.where(left, o00, pltpu.roll(o01, HEAD_DIM, 1))
    out1 = jnp.where(left, pltpu.roll(o10, HEAD_DIM, 1), o11)
    return jnp.concatenate([out0, out1], axis=-1)


def _gattn_kernel(q_ref, kl_ref, vl_ref, kc_ref, vc_ref, o_ref, m_ref, l_ref, acc_ref,
                  *, tq, units, n_qb_lat, n_kvb):
    qb = pl.program_id(1)

    def tree(op, xs):
        while len(xs) > 1:
            xs = [op(xs[i], xs[i + 1]) for i in range(0, len(xs) - 1, 2)] + ([xs[-1]] if len(xs) % 2 else [])
        return xs[0]

    def scores(h, kts):
        q = q_ref[h]
        cols = []
        for kt in kts:
            s = _dot(q, kt)
            cols += [s[:, 128 * c:128 * (c + 1)] for c in range(s.shape[1] // 128)]
        return cols

    def softmax(h, cols, first):
        m_blk = jnp.max(tree(jnp.maximum, cols), axis=-1, keepdims=True)
        if first:
            m_new = jnp.broadcast_to(m_blk, (tq, 128))
            alpha = None
            ps = [jnp.exp(c - m_new) for c in cols]
            l_ref[h] = tree(jnp.add, ps)
        else:
            m_old = m_ref[h]
            m_new = jnp.maximum(m_old, m_blk)
            alpha = jnp.exp(m_old - m_new)
            ps = [jnp.exp(c - m_new) for c in cols]
            l_ref[h] = alpha * l_ref[h] + tree(jnp.add, ps)
        m_ref[h] = m_new
        return alpha, jnp.concatenate(ps, axis=-1).astype(BF16)

    def weighted(h, alpha, p, v):
        pv = _dot(p, v)
        acc_ref[h] = pv if alpha is None else alpha * acc_ref[h] + pv

    def block(kts, v, first):
        cols = scores(0, kts)
        for h in range(N_HEADS):
            nxt = scores(h + 1, kts) if h + 1 < N_HEADS else None
            alpha, p = softmax(h, cols, first)
            weighted(h, alpha, p, v)
            cols = nxt

    block([kc_ref[u] for u in range(kc_ref.shape[0])], vc_ref[...], True)

    def body(j, carry):
        rows = pl.ds(pl.multiple_of(j * (units * KT_UNIT), units * KT_UNIT), units * KT_UNIT)
        block([kl_ref[j * units + u] for u in range(units)], vl_ref[rows, :], False)
        return carry

    lax.fori_loop(0, jnp.where(qb < n_qb_lat, n_kvb, 0), body, 0)
    o = [acc_ref[h] / jnp.sum(l_ref[h], axis=-1, keepdims=True) for h in range(N_HEADS)]
    o_ref[...] = _pack_heads(jnp.concatenate(o, axis=0), tq).astype(BF16)


def _gattn(qp, kt, p, bsz, seq, ctx, with_ctx):
    t_lat = bsz * seq
    tq = ctx
    units = 4
    n_qb_lat = seq // tq
    n_qb = n_qb_lat + (1 if with_ctx else 0)
    cb0 = t_lat // ctx
    t_out = t_lat + (bsz * ctx if with_ctx else 0)
    assert ctx % KT_UNIT == 0 and seq % (units * KT_UNIT) == 0

    def qrow(b, i):
        return jnp.where(i < n_qb_lat, b * n_qb_lat + i, cb0 + b)

    kern = functools.partial(_gattn_kernel, tq=tq, units=units, n_qb_lat=n_qb_lat,
                             n_kvb=seq // (units * KT_UNIT))
    return pl.pallas_call(
        kern,
        out_shape=jax.ShapeDtypeStruct((t_out, MIX_W), BF16),
        grid=(bsz, n_qb),
        in_specs=[pl.BlockSpec((N_HEADS, tq, 128), lambda b, i: (0, qrow(b, i), 0)),
                  pl.BlockSpec((seq // KT_UNIT, 128, KT_UNIT), lambda b, i: (b, 0, 0)),
                  pl.BlockSpec((seq, 128), lambda b, i: (b, C_CV // 128)),
                  pl.BlockSpec((ctx // KT_UNIT, 128, KT_UNIT), lambda b, i: (cb0 + b, 0, 0)),
                  pl.BlockSpec((ctx, 128), lambda b, i: (cb0 + b, C_CV // 128))],
        out_specs=pl.BlockSpec((tq, MIX_W), lambda b, i: (qrow(b, i), 0)),
        scratch_shapes=[pltpu.VMEM((N_HEADS, tq, 128), F32)] * 3,
        compiler_params=_cp("parallel", "arbitrary"),
        name="global_attn",
    )(qp, kt, p, kt, p)


def _wattn_kernel(q_ref, kl_ref, vl_ref, kc_ref, vc_ref, sink_ref, o_ref, *, nb):
    w = WINDOW
    n = pl.program_id(1)
    is_lat = n < nb
    rows = 2 * w
    qi = lax.broadcasted_iota(jnp.int32, (rows, w), 0) & (w - 1)
    kj = lax.broadcasted_iota(jnp.int32, (rows, w), 1)
    band = (jnp.clip(n - 1, 0, nb - 1), jnp.clip(n, 0, nb - 1), jnp.clip(n + 1, 0, nb - 1))
    off_prev = jnp.where(is_lat & (n >= 1), 0, w)
    off_cur = jnp.where(is_lat, 0, w)
    off_next = jnp.where(n + 1 < nb, 0, w)
    masks = [kj >= qi + off_prev, kj >= off_cur, kj <= qi - off_next]
    kts = [kc_ref[u] for u in range(kc_ref.shape[0])] + [kl_ref[i] for i in band]
    n_ctx_tiles = kc_ref.shape[0]
    v_all = jnp.concatenate([vc_ref[...]] + [vl_ref[pl.ds(pl.multiple_of(i * w, w), w), :] for i in band],
                            axis=0)

    def tree(op, xs):
        while len(xs) > 1:
            xs = [op(xs[i], xs[i + 1]) for i in range(0, len(xs) - 1, 2)] + ([xs[-1]] if len(xs) % 2 else [])
        return xs[0]

    def scores(c):
        q = q_ref[2 * c:2 * c + 2].reshape(rows, 128)
        tiles = [_dot(q, kt) for kt in kts]
        return tiles[:n_ctx_tiles] + [jnp.where(mk, t, NEG_INF) for mk, t in zip(masks, tiles[n_ctx_tiles:])]

    def finish(c, tiles):
        sink = sink_ref[c * rows:(c + 1) * rows, :]
        m = jnp.maximum(jnp.max(tree(jnp.maximum, tiles), axis=-1, keepdims=True), sink)
        ps = [jnp.exp(t - m) for t in tiles]
        den = jnp.sum(tree(jnp.add, ps), axis=-1, keepdims=True) + jnp.exp(sink - m)
        return _dot(jnp.concatenate(ps, axis=-1).astype(BF16), v_all) / den

    tiles = scores(0)
    nxt = scores(1)
    o = [finish(0, tiles), finish(1, nxt)]
    o_ref[...] = _pack_heads(jnp.concatenate(o, axis=0), w).astype(BF16)


def _wattn(qp, k, p, sink, bsz, seq, ctx, with_ctx):
    t_lat = bsz * seq
    w = WINDOW
    nb = seq // w
    ncb = ctx // w
    n_qb = nb + (ncb if with_ctx else 0)
    cq0 = t_lat // w
    cb0 = t_lat // ctx
    t_out = t_lat + (bsz * ctx if with_ctx else 0)
    sink_rows = jnp.broadcast_to(jnp.repeat(sink.astype(F32), w)[:, None], (N_HEADS * w, 128))

    def qrow(b, i):
        return jnp.where(i < nb, b * nb + i, cq0 + b * ncb + (i - nb))

    return pl.pallas_call(
        functools.partial(_wattn_kernel, nb=nb),
        out_shape=jax.ShapeDtypeStruct((t_out, MIX_W), BF16),
        grid=(bsz, n_qb),
        in_specs=[pl.BlockSpec((N_HEADS, w, 128), lambda b, i: (0, qrow(b, i), 0)),
                  pl.BlockSpec((seq // w, 128, w), lambda b, i: (b, 0, 0)),
                  pl.BlockSpec((seq, 128), lambda b, i: (b, C_DV // 128)),
                  pl.BlockSpec((ctx // w, 128, w), lambda b, i: (cb0 + b, 0, 0)),
                  pl.BlockSpec((ctx, 128), lambda b, i: (cb0 + b, C_DV // 128)),
                  pl.BlockSpec((N_HEADS * w, 128), lambda b, i: (0, 0))],
        out_specs=pl.BlockSpec((w, MIX_W), lambda b, i: (qrow(b, i), 0)),
        compiler_params=_cp("parallel", "arbitrary"),
        name="window_attn",
    )(qp, k, p, k, p, sink_rows)


def _merge_kernel(ya_ref, yb_ref, yc_ref, yd_ref, gate_ref, x_ref, mod_ref, wbr_ref, wout_ref,
                  g2_ref, xo_ref, h2_ref):
    d = x_ref.shape[1]
    acc = None
    for n, ref in enumerate((ya_ref, yb_ref, yc_ref, yd_ref)):
        gate = jax.nn.sigmoid(gate_ref[:, n * d:(n + 1) * d].astype(F32))
        term = gate * _dot(ref[...], wbr_ref[n])
        acc = term if acc is None else acc + term
    x = x_ref[...] + mod_ref[0, 2:3, :] * _dot(acc.astype(BF16), wout_ref[...])
    xo_ref[...] = x
    ms = jnp.mean(x * x, axis=-1, keepdims=True)
    y = x * lax.rsqrt(ms + EPS) * g2_ref[...]
    h2_ref[...] = y * (1.0 + mod_ref[0, 4:5, :]) + mod_ref[0, 3:4, :]


def _merge(ys, p, x, mod, wbr, wout, g2, t_out, t_lat, seq, tm):
    d = x.shape[1]

    def row(width):
        return pl.BlockSpec((tm, width), lambda i: (i, 0))

    def const(shape):
        return pl.BlockSpec(shape, lambda i: (0,) * len(shape))

    return pl.pallas_call(
        _merge_kernel,
        out_shape=(jax.ShapeDtypeStruct((t_out, d), F32), jax.ShapeDtypeStruct((t_out, d), F32)),
        grid=(t_out // tm,),
        in_specs=[row(MIX_W)] * 4 + [row(4 * d), row(d),
                  pl.BlockSpec((1, 6, d), lambda i: (_mod_group(i * tm, t_lat, seq), 0, 0)),
                  const((4, MIX_W, d)), const((d, d)), const((1, d))],
        out_specs=(row(d), row(d)),
        compiler_params=_cp("parallel"),
        name="merge",
    )(*ys, p, x, mod, wbr, wout, g2.reshape(1, d))


def _ffn_dense_kernel(h_ref, x_ref, mod_ref, wg_ref, wu_ref, wo_ref, o_ref, hb_ref, acc_ref):
    j = pl.program_id(1)

    @pl.when(j == 0)
    def _():
        hb_ref[...] = h_ref[...].astype(BF16)
        acc_ref[...] = jnp.zeros_like(acc_ref)

    h = hb_ref[...]
    a = _silu(_dot(h, wg_ref[...])) * _dot(h, wu_ref[...])
    acc_ref[...] += _dot(a.astype(BF16), wo_ref[...])

    @pl.when(j == pl.num_programs(1) - 1)
    def _():
        o_ref[...] = x_ref[...] + mod_ref[0, 5:6, :] * acc_ref[...]


def _ffn_dense(h2, x, mod, w_in, w_out, t_lat, seq, tm):
    t, d = x.shape
    f = w_out.shape[0]
    tf = 256
    nf = f // tf
    return pl.pallas_call(
        _ffn_dense_kernel,
        out_shape=jax.ShapeDtypeStruct((t, d), F32),
        grid=(t // tm, nf),
        in_specs=[pl.BlockSpec((tm, d), lambda i, j: (i, 0)),
                  pl.BlockSpec((tm, d), lambda i, j: (i, 0)),
                  pl.BlockSpec((1, 6, d), lambda i, j: (_mod_group(i * tm, t_lat, seq), 0, 0)),
                  pl.BlockSpec((d, tf), lambda i, j: (0, j)),
                  pl.BlockSpec((d, tf), lambda i, j: (0, nf + j)),
                  pl.BlockSpec((tf, d), lambda i, j: (j, 0))],
        out_specs=pl.BlockSpec((tm, d), lambda i, j: (i, 0)),
        scratch_shapes=[pltpu.VMEM((tm, d), BF16), pltpu.VMEM((tm, d), F32)],
        compiler_params=_cp("parallel", "arbitrary"),
        name="ffn_dense",
    )(h2, x, mod, w_in, w_in, w_out)


def _router_kernel(h_ref, w_ref, e_ref, g1_ref, g2_ref):
    h = h_ref[...]
    h_hi = h.astype(BF16)
    h_lo = (h - h_hi.astype(F32)).astype(BF16)
    logits = _dot(h_hi, w_ref[0]) + _dot(h_lo, w_ref[0]) + _dot(h_hi, w_ref[1])
    lane = lax.broadcasted_iota(jnp.int32, logits.shape, 1)
    lane_f = lane.astype(F32)
    logits = jnp.where(lane < N_EXPERTS, logits, -jnp.inf)
    m1 = jnp.max(logits, axis=-1, keepdims=True)
    i1 = jnp.min(jnp.where(logits == m1, lane_f, 128.0), axis=-1, keepdims=True)
    rest = jnp.where(lane_f == i1, -jnp.inf, logits)
    m2 = jnp.max(rest, axis=-1, keepdims=True)
    i2 = jnp.min(jnp.where(rest == m2, lane_f, 128.0), axis=-1, keepdims=True)
    e2 = jnp.exp(m2 - m1)
    g1 = 1.0 / (1.0 + e2)
    e_ref[...] = jnp.where(lane == 0, i1, jnp.where(lane == 1, i2, 0.0)).astype(jnp.int32)
    g1_ref[...] = jnp.broadcast_to(g1, g1_ref.shape)
    g2_ref[...] = jnp.broadcast_to(e2 * g1, g2_ref.shape)


def _router(h2, router, tm):
    t, d = h2.shape
    r = jnp.zeros((d, 128), F32).at[:, :N_EXPERTS].set(router.astype(F32))
    r_hi = r.astype(BF16)
    r_lo = (r - r_hi.astype(F32)).astype(BF16)
    shp = jax.ShapeDtypeStruct((t, 128), F32)
    return pl.pallas_call(
        _router_kernel,
        out_shape=(jax.ShapeDtypeStruct((t, 128), jnp.int32), shp, shp),
        grid=(t // tm,),
        in_specs=[pl.BlockSpec((tm, d), lambda i: (i, 0)),
                  pl.BlockSpec((2, d, 128), lambda i: (0, 0, 0))],
        out_specs=(pl.BlockSpec((tm, 128), lambda i: (i, 0)),) * 3,
        compiler_params=_cp("parallel"),
        name="router",
    )(h2, jnp.stack([r_hi, r_lo]))


def _gather_rows_kernel(idx_ref, src_ref, o_ref, sem, *, tm):
    def issue(r, carry):
        pltpu.make_async_copy(src_ref.at[pl.ds(idx_ref[0, 0, r], 1)], o_ref.at[pl.ds(r, 1)], sem).start()
        return carry

    def drain(r, carry):
        pltpu.make_async_copy(src_ref.at[pl.ds(0, 1)], o_ref.at[pl.ds(r, 1)], sem).wait()
        return carry

    lax.fori_loop(0, tm, issue, 0)
    lax.fori_loop(0, tm, drain, 0)


def _gather_rows(src, idx, tm):
    n = idx.shape[0]
    d = src.shape[1]
    return pl.pallas_call(
        functools.partial(_gather_rows_kernel, tm=tm),
        out_shape=jax.ShapeDtypeStruct((n, d), src.dtype),
        grid=(n // tm,),
        in_specs=[pl.BlockSpec((1, 1, tm), lambda i: (i, 0, 0), memory_space=pltpu.SMEM),
                  pl.BlockSpec(memory_space=pl.ANY)],
        out_specs=pl.BlockSpec((tm, d), lambda i: (i, 0)),
        scratch_shapes=[pltpu.SemaphoreType.DMA(())],
        compiler_params=_cp("arbitrary"),
        name="gather_rows",
    )(idx.reshape(n // tm, 1, tm), src)


def _experts_kernel(te_ref, nu_ref, x_ref, wg_ref, wu_ref, wo_ref, o_ref, xb_ref, acc_ref):
    i, j = pl.program_id(0), pl.program_id(1)

    @pl.when(i < nu_ref[0])
    def _():
        @pl.when(j == 0)
        def _():
            xb_ref[...] = x_ref[...].astype(BF16)
            acc_ref[...] = jnp.zeros_like(acc_ref)

        x = xb_ref[...]
        a = _silu(_dot(x, wg_ref[0])) * _dot(x, wu_ref[0])
        acc_ref[...] += _dot(a.astype(BF16), wo_ref[0])

        @pl.when(j == pl.num_programs(1) - 1)
        def _():
            o_ref[...] = acc_ref[...]

    @pl.when((i >= nu_ref[0]) & (j == 0))
    def _():
        o_ref[...] = jnp.zeros_like(o_ref)


def _experts(x_rows, tile_e, n_used, w_in, w_out):
    n, d = x_rows.shape
    f = w_out.shape[1]
    tf = 512
    nf = f // tf
    tm = MOE_TILE
    grid_spec = pltpu.PrefetchScalarGridSpec(
        num_scalar_prefetch=2,
        grid=(n // tm, nf),
        in_specs=[pl.BlockSpec((tm, d), lambda i, j, te, nu: (i, 0)),
                  pl.BlockSpec((1, d, tf), lambda i, j, te, nu: (te[i], 0, j)),
                  pl.BlockSpec((1, d, tf), lambda i, j, te, nu: (te[i], 0, nf + j)),
                  pl.BlockSpec((1, tf, d), lambda i, j, te, nu: (te[i], j, 0))],
        out_specs=pl.BlockSpec((tm, d), lambda i, j, te, nu: (i, 0)),
        scratch_shapes=[pltpu.VMEM((tm, d), BF16), pltpu.VMEM((tm, d), F32)])
    return pl.pallas_call(
        _experts_kernel,
        out_shape=jax.ShapeDtypeStruct((n, d), F32),
        grid_spec=grid_spec,
        compiler_params=_cp("arbitrary", "arbitrary"),
        name="experts",
    )(tile_e, n_used, x_rows, w_in, w_in, w_out)


def _combine_kernel(idx_ref, y_ref, x_ref, mod_ref, g1_ref, g2_ref, o_ref, b1_ref, b2_ref, sem, *, tm):
    def issue(r, carry):
        pltpu.make_async_copy(y_ref.at[pl.ds(idx_ref[0, 0, r], 1)], b1_ref.at[pl.ds(r, 1)], sem).start()
        pltpu.make_async_copy(y_ref.at[pl.ds(idx_ref[0, 1, r], 1)], b2_ref.at[pl.ds(r, 1)], sem).start()
        return carry

    def drain(r, carry):
        pltpu.make_async_copy(y_ref.at[pl.ds(0, 1)], b1_ref.at[pl.ds(r, 1)], sem).wait()
        pltpu.make_async_copy(y_ref.at[pl.ds(0, 1)], b2_ref.at[pl.ds(r, 1)], sem).wait()
        return carry

    lax.fori_loop(0, tm, issue, 0)
    lax.fori_loop(0, tm, drain, 0)
    d = x_ref.shape[1]
    g1 = jnp.concatenate([g1_ref[...]] * (d // 128), axis=-1)
    g2 = jnp.concatenate([g2_ref[...]] * (d // 128), axis=-1)
    o_ref[...] = x_ref[...] + mod_ref[0, 5:6, :] * (g1 * b1_ref[...] + g2 * b2_ref[...])


def _combine(y_rows, dest, x, mod, g1, g2, t_lat, seq, tm):
    t, d = x.shape
    return pl.pallas_call(
        functools.partial(_combine_kernel, tm=tm),
        out_shape=jax.ShapeDtypeStruct((t, d), F32),
        grid=(t // tm,),
        in_specs=[pl.BlockSpec((1, 2, tm), lambda i: (i, 0, 0), memory_space=pltpu.SMEM),
                  pl.BlockSpec(memory_space=pl.ANY),
                  pl.BlockSpec((tm, d), lambda i: (i, 0)),
                  pl.BlockSpec((1, 6, d), lambda i: (_mod_group(i * tm, t_lat, seq), 0, 0)),
                  pl.BlockSpec((tm, 128), lambda i: (i, 0)),
                  pl.BlockSpec((tm, 128), lambda i: (i, 0))],
        out_specs=pl.BlockSpec((tm, d), lambda i: (i, 0)),
        scratch_shapes=[pltpu.VMEM((tm, d), F32), pltpu.VMEM((tm, d), F32), pltpu.SemaphoreType.DMA(())],
        compiler_params=_cp("arbitrary"),
        name="moe_combine",
    )(jnp.transpose(dest.reshape(t // tm, tm, 2), (0, 2, 1)), y_rows, x, mod, g1, g2)


def _moe(h2, x, mod, router, w_in, w_out, t_lat, seq, tm):
    t, d = h2.shape
    e_idx, g1, g2 = _router(h2, router, tm)
    e_flat = e_idx[:, :2].reshape(-1)
    onehot = (e_flat[:, None] == jnp.arange(N_EXPERTS, dtype=jnp.int32)[None, :]).astype(jnp.int32)
    csum = jnp.cumsum(onehot, axis=0)
    counts = csum[-1]
    padded = (counts + MOE_TILE - 1) // MOE_TILE * MOE_TILE
    ends = jnp.cumsum(padded)
    pstarts = ends - padded
    dest = jnp.sum(onehot * (pstarts[None, :] + csum - 1), axis=1).astype(jnp.int32)
    n_tiles = (2 * t + MOE_TILE - 1) // MOE_TILE + N_EXPERTS
    n_rows = n_tiles * MOE_TILE
    row_tok = jnp.zeros((n_rows,), jnp.int32).at[dest].set(jnp.arange(2 * t, dtype=jnp.int32) // 2)
    tile_start = jnp.arange(n_tiles, dtype=jnp.int32) * MOE_TILE
    tile_e = jnp.minimum(jnp.searchsorted(ends, tile_start, side='right'), N_EXPERTS - 1).astype(jnp.int32)
    n_used = (ends[-1:] // MOE_TILE).astype(jnp.int32)

    x_rows = _gather_rows(h2, row_tok, MOE_TILE)
    y_rows = _experts(x_rows, tile_e, n_used, w_in, w_out)
    return _combine(y_rows, dest, x, mod, g1, g2, t_lat, seq, tm)


def _final_norm_kernel(x_ref, g_ref, o_ref):
    x = x_ref[...]
    ms = jnp.mean(x * x, axis=-1, keepdims=True)
    o_ref[...] = x * lax.rsqrt(ms + EPS) * g_ref[...]


def _final_norm(x, g, tm):
    t, d = x.shape
    return pl.pallas_call(
        _final_norm_kernel,
        out_shape=jax.ShapeDtypeStruct((t, d), F32),
        grid=(t // tm,),
        in_specs=[pl.BlockSpec((tm, d), lambda i: (i, 0)), pl.BlockSpec((1, d), lambda i: (0, 0))],
        out_specs=pl.BlockSpec((tm, d), lambda i: (i, 0)),
        compiler_params=_cp("parallel"),
        name="final_norm",
    )(x, g.reshape(1, d))


def _proj_weights(w_in):
    o = {}
    acc = 0
    for name, size in (('a_z', 256), ('a_x', 256), ('a_b', 256), ('a_c', 256), ('a_dt', 8), ('b_u', 256),
                       ('c_q', 256), ('c_k', 128), ('c_v', 128), ('d_q', 256), ('d_k', 128), ('d_v', 128),
                       ('gates', 4096)):
        o[name] = (acc, size)
        acc += size
    order = ('gates', 'a_z', 'b_u', 'c_q', 'd_q', 'c_k', 'c_v', 'd_k', 'd_v', 'a_x', 'a_b', 'a_c')
    perm = np.concatenate([np.arange(o[n][0], o[n][0] + o[n][1]) for n in order])
    w = w_in[:, :, perm].astype(BF16)
    dt0 = o['a_dt'][0]
    wdt = jnp.pad(w_in[:, :, dt0:dt0 + 8], ((0, 0), (0, 0), (0, 120))).astype(BF16)
    return w, wdt


def kernel(x, c, ctx, c_ctx, norm1_g, norm2_g, ada_w, ada_b, w_in, ssd_conv_w, ssd_conv_b, ssd_a_log,
           ssd_dt_bias, ssd_d, ssd_norm_g, s5_lam_re, s5_lam_im, s5_log_step, s5_b_re, s5_b_im, s5_c_re,
           s5_c_im, s5_d, s5_glu_w, qk_norm_g, swa_sink, w_branch, w_out, ffn_w_in, ffn_w_out, moe_router,
           moe_w_in, moe_w_out, final_norm_g):
    bsz, seq, d = x.shape
    n_ctx = ctx.shape[1]
    depth = w_in.shape[0]
    t_lat, t_ctx = bsz * seq, bsz * n_ctx
    tm = _pow2_tile(1024, seq, t_ctx)
    tm_small = _pow2_tile(512, seq, t_ctx)

    cvec = jnp.zeros((16, d), F32).at[0].set(c_ctx).at[1:1 + bsz].set(c)
    mod = _adaln(cvec, ada_w, ada_b).reshape(depth, 16, 6, d)
    wp, wdt = _proj_weights(w_in)
    wbr = w_branch.astype(BF16)
    wo = w_out.astype(BF16)
    ffn_in, ffn_out = ffn_w_in.astype(BF16), ffn_w_out.astype(BF16)
    moe_in, moe_out = moe_w_in.astype(BF16), moe_w_out.astype(BF16)

    xx = jnp.concatenate([x.reshape(t_lat, d), ctx.reshape(t_ctx, d)], axis=0)
    for l in range(depth):
        with_ctx = l < depth - 1
        t_out = t_lat + (t_ctx if with_ctx else 0)
        p, pdt = _inproj(xx, norm1_g[l], mod[l], wp[l], wdt[l], t_lat, seq, tm)
        ya = _ssd(p, pdt, ssd_conv_w[l], ssd_conv_b[l], ssd_a_log[l], ssd_dt_bias[l], ssd_d[l],
                  ssd_norm_g[l], bsz, seq, n_ctx)
        yb = _s5(p, s5_lam_re[l], s5_lam_im[l], s5_log_step[l], s5_b_re[l], s5_b_im[l], s5_c_re[l],
                 s5_c_im[l], s5_d[l], s5_glu_w[l], bsz, seq, n_ctx)
        q1, q2, k1, k2 = _prep(p, qk_norm_g[l], t_lat, seq, tm)
        yc = _gattn(q1, k1, p, bsz, seq, n_ctx, with_ctx)
        yd = _wattn(q2, k2, p, swa_sink[l], bsz, seq, n_ctx, with_ctx)
        xx, h2 = _merge((ya, yb, yc, yd), p, xx, mod[l], wbr[l], wo[l], norm2_g[l], t_out, t_lat, seq,
                        tm_small)
        if l % 2 == 0:
            xx = _ffn_dense(h2, xx, mod[l], ffn_in[l // 2], ffn_out[l // 2], t_lat, seq, tm)
        else:
            xx = _moe(h2, xx, mod[l], moe_router[l // 2], moe_in[l // 2], moe_out[l // 2], t_lat, seq,
                      tm_small)
    out = _final_norm(xx[:t_lat], final_norm_g, tm)
    return out.reshape(bsz, seq, d)
```

```python
import functools

import numpy as np
import jax
import jax.numpy as jnp
from jax import lax
from jax.experimental import pallas as pl
from jax.experimental.pallas import tpu as pltpu

F32 = jnp.float32
BF16 = jnp.bfloat16

EPS = 1e-6
NEG_INF = -1e30
GRID_W = 64
MIX_W = 256
HEAD_DIM = 64
N_HEADS = 4
ATTN_SCALE = HEAD_DIM ** -0.5
SSD_STATE = 128
SSD_CONV = 5
CHUNK = 128
HALO = 16
S5_GROUPS = 16
S5_GROUP = 16
S5_STATE = 64
S5_LANES = S5_GROUPS * S5_STATE
S5_STEPS = 64
ROPE_BASE = 10000.0
ROPE_FREQS = 16
WINDOW = 128
KT_UNIT = 256
N_EXPERTS = 8
MOE_TILE = 512
MOE_STEPS = 7
MOE_WAITS = 64
VMEM_LIMIT = 56 * 1024 * 1024

C_GATES, C_Z, C_BU, C_CQ, C_DQ = 0, 4096, 4352, 4608, 4864
C_CK, C_CV, C_DK, C_DV = 5120, 5248, 5376, 5504
C_AX, C_AB, C_AC = 5632, 5888, 6144
P_COLS = 6400
PROJ_TN = 1280


def _cp(*sem):
    return pltpu.CompilerParams(dimension_semantics=sem, vmem_limit_bytes=VMEM_LIMIT)


def _pow2_tile(cap, *dims):
    t = 1
    while t * 2 <= cap and all(d % (t * 2) == 0 for d in dims):
        t *= 2
    return t


def _dot(a, b):
    return jnp.dot(a, b, preferred_element_type=F32)


def _dot_nt(a, b):
    return lax.dot_general(a, b, (((1,), (1,)), ((), ())), preferred_element_type=F32)


def _dot_tn(a, b):
    return lax.dot_general(a, b, (((0,), (0,)), ((), ())), preferred_element_type=F32)


def _split3(x):
    hi = x.astype(BF16)
    r1 = x - hi.astype(F32)
    mid = r1.astype(BF16)
    lo = (r1 - mid.astype(F32)).astype(BF16)
    return hi, mid, lo


def _silu(x):
    return x * jax.nn.sigmoid(x)


def _adaln_kernel(c_ref, w_ref, b_ref, o_ref):
    c = c_ref[...]
    o_ref[0] = jnp.dot(_silu(c), w_ref[0], preferred_element_type=F32,
                       precision=lax.Precision.HIGHEST) + b_ref[0]


def _adaln(cvec, ada_w, ada_b):
    depth, d, n = ada_w.shape
    tn = 1024
    return pl.pallas_call(
        _adaln_kernel,
        out_shape=jax.ShapeDtypeStruct((depth, 16, n), F32),
        grid=(depth, n // tn),
        in_specs=[pl.BlockSpec((16, d), lambda l, j: (0, 0)),
                  pl.BlockSpec((1, d, tn), lambda l, j: (l, 0, j)),
                  pl.BlockSpec((1, 1, tn), lambda l, j: (l, 0, j))],
        out_specs=pl.BlockSpec((1, 16, tn), lambda l, j: (l, 0, j)),
        compiler_params=_cp("parallel", "parallel"),
        name="adaln",
    )(cvec, ada_w, ada_b.reshape(depth, 1, n))


def _mod_group(row0, t_lat, seq):
    return jnp.where(row0 >= t_lat, 0, 1 + row0 // seq)


def _inproj_kernel(x_ref, g_ref, mod_ref, w_ref, wdt_ref, o_ref, odt_ref, h_ref):
    @pl.when(pl.program_id(1) == 0)
    def _():
        x = x_ref[...]
        ms = jnp.mean(x * x, axis=-1, keepdims=True)
        y = x * lax.rsqrt(ms + EPS) * g_ref[...]
        h = (y * (1.0 + mod_ref[0, 1:2, :]) + mod_ref[0, 0:1, :]).astype(BF16)
        h_ref[...] = h
        odt_ref[...] = _dot(h, wdt_ref[...])

    o_ref[...] = _dot(h_ref[...], w_ref[...]).astype(BF16)


def _inproj(x, g, mod, w, wdt, t_lat, seq, tm):
    t, d = x.shape
    n = w.shape[1]
    tn = PROJ_TN
    return pl.pallas_call(
        _inproj_kernel,
        out_shape=(jax.ShapeDtypeStruct((t, n), BF16), jax.ShapeDtypeStruct((t, 128), F32)),
        grid=(t // tm, n // tn),
        in_specs=[pl.BlockSpec((tm, d), lambda i, j: (i, 0)),
                  pl.BlockSpec((1, d), lambda i, j: (0, 0)),
                  pl.BlockSpec((1, 6, d), lambda i, j: (_mod_group(i * tm, t_lat, seq), 0, 0)),
                  pl.BlockSpec((d, tn), lambda i, j: (0, j)),
                  pl.BlockSpec((d, 128), lambda i, j: (0, 0))],
        out_specs=(pl.BlockSpec((tm, tn), lambda i, j: (i, j)),
                   pl.BlockSpec((tm, 128), lambda i, j: (i, 0))),
        scratch_shapes=[pltpu.VMEM((tm, d), BF16)],
        compiler_params=_cp("parallel", "arbitrary"),
        name="inproj",
    )(x, g.reshape(1, d), mod, w, wdt)


def _ssd_consts():
    r = np.arange(CHUNK)
    tri_l = (r[None, :] <= r[:, None]).astype(np.float32)
    tri_u = tri_l.T.copy()
    shifts = np.zeros((4, CHUNK, CHUNK + 2 * HALO), np.float32)
    for n, k in enumerate((0, 1, 3, 4)):
        shifts[n, r, r + HALO + k - 2] = 1.0
    return (jnp.asarray(np.stack([tri_l, tri_u]), BF16), jnp.asarray(shifts, BF16))


def _ssd_kernel(zl_ref, xl_ref, bl_ref, cl_ref, dtl_ref, zc_ref, xc_ref, bc_ref, cc_ref, dtc_ref,
                cw_ref, cb_ref, an_ref, bias_ref, dsk_ref, ng_ref, tri_ref, sh_ref,
                yl_ref, yc_ref, act_l, act_c, yf_l, yf_c, stf_ref, stb_ref, *, n_lat, n_ctx):
    q = CHUNK
    lane128 = lax.broadcasted_iota(jnp.int32, (q, 128), 1)
    lane256 = lax.broadcasted_iota(jnp.int32, (q, MIX_W), 1)
    lane256r = lax.broadcasted_iota(jnp.int32, (1, MIX_W), 1)
    row_i = lax.broadcasted_iota(jnp.int32, (q, q), 0)
    col_i = lax.broadcasted_iota(jnp.int32, (q, q), 1)
    head_masks = [(lane256 >= HEAD_DIM * h) & (lane256 < HEAD_DIM * (h + 1)) for h in range(N_HEADS)]
    head_masks_r = [(lane256r >= HEAD_DIM * h) & (lane256r < HEAD_DIM * (h + 1)) for h in range(N_HEADS)]
    group_masks = [lane256 < 128, lane256 >= 128]

    def conv_act(x_ref, b_ref, c_ref, n_chunks, c):
        def rows(ref, start, size):
            return ref[pl.ds(start, size), :]

        start = c * q
        if isinstance(c, int):
            p0, n0 = max(start - HALO, 0), min(start + q, n_chunks * q - HALO)
            pf, nf = float(c > 0), float(c < n_chunks - 1)
        else:
            start = pl.multiple_of(start, q)
            p0 = pl.multiple_of(jnp.maximum(start - HALO, 0), HALO)
            n0 = pl.multiple_of(jnp.minimum(start + q, n_chunks * q - HALO), HALO)
            pf, nf = (c > 0).astype(F32), (c < n_chunks - 1).astype(F32)
        parts = []
        for ref in (x_ref, b_ref, c_ref):
            prev = (rows(ref, p0, HALO).astype(F32) * pf).astype(BF16)
            nxt = (rows(ref, n0, HALO).astype(F32) * nf).astype(BF16)
            parts.append(jnp.concatenate([prev, rows(ref, start, q), nxt], axis=0))
        ext = jnp.concatenate(parts, axis=1)
        cur = ext[HALO:HALO + q].astype(F32)
        acc = cur * cw_ref[2:3, :] + cb_ref[...]
        for n, k in enumerate((0, 1, 3, 4)):
            acc = acc + _dot(sh_ref[n], ext) * cw_ref[k:k + 1, :]
        return _silu(acc)

    def conv_chunk(seg, c):
        _, x_ref, b_ref, c_ref, _, act_ref, _, _, n_chunks = seg
        start = c * q if isinstance(c, int) else pl.multiple_of(c * q, q)
        act_ref[pl.ds(start, q), :] = conv_act(x_ref, b_ref, c_ref, n_chunks, c).astype(BF16)

    def chunk(seg, c, d, second):
        z_ref, x_ref, b_ref, c_ref, dt_ref, act_ref, yf_ref, y_ref, n_chunks = seg
        st_ref = st_refs[d]
        start = c * q if isinstance(c, int) else pl.multiple_of(c * q, q)
        act = act_ref[pl.ds(start, q), :]
        xs = act[:, 0:MIX_W].astype(F32)
        bm = act[:, MIX_W:2 * MIX_W]
        cm = act[:, 2 * MIX_W:3 * MIX_W]

        dt_n = jax.nn.softplus(dt_ref[pl.ds(start, q), :] + bias_ref[...])
        la_n = dt_n * an_ref[...]
        hi, mid, lo = _split3(la_n)
        tri = tri_ref[d]
        cs_n = _dot(tri, hi) + _dot(tri, mid) + _dot(tri, lo)
        cs_t = cs_n.T
        edge = q - 1 if d == 0 else 0
        tri_mask = (col_i <= row_i) if d == 0 else (col_i >= row_i)

        dt_full = jnp.zeros((q, MIX_W), F32)
        dte_full = jnp.zeros((q, MIX_W), F32)
        ecs_full = jnp.zeros((q, MIX_W), F32)
        tot_full = jnp.zeros((1, MIX_W), F32)
        decay = []
        for h in range(N_HEADS):
            sel = lane128 == (N_HEADS * d + h)
            cs_col = jnp.sum(jnp.where(sel, cs_n, 0.0), axis=-1, keepdims=True)
            dt_col = jnp.sum(jnp.where(sel, dt_n, 0.0), axis=-1, keepdims=True)
            tot = cs_col[edge:edge + 1, :]
            cs_row = cs_t[N_HEADS * d + h:N_HEADS * d + h + 1, :]
            decay.append(jnp.where(tri_mask, jnp.exp(cs_col - cs_row), 0.0))
            dt_full = jnp.where(head_masks[h], dt_col, dt_full)
            dte_full = jnp.where(head_masks[h], jnp.exp(tot - cs_col), dte_full)
            ecs_full = jnp.where(head_masks[h], jnp.exp(cs_col), ecs_full)
            tot_full = jnp.where(head_masks_r[h], jnp.exp(tot), tot_full)

        xdt = xs * dt_full
        state = st_ref[...]
        y = jnp.zeros((q, MIX_W), F32)
        y_off = jnp.zeros((q, MIX_W), F32)
        upd = jnp.zeros((SSD_STATE, MIX_W), F32)
        xdte = xdt * dte_full
        for g in range(2):
            bg = bm[:, 128 * g:128 * (g + 1)]
            cg = cm[:, 128 * g:128 * (g + 1)]
            cb = _dot_nt(cg, bg)
            for h in (2 * g, 2 * g + 1):
                m = (cb * decay[h]).astype(BF16)
                y = y + _dot(m, jnp.where(head_masks[h], xdt, 0.0).astype(BF16))
            y_off = y_off + _dot(cg, jnp.where(group_masks[g], state, 0.0).astype(BF16))
            upd = upd + _dot_tn(bg, jnp.where(group_masks[g], xdte, 0.0).astype(BF16))
        y = y + y_off * ecs_full
        st_ref[...] = state * tot_full + upd

        if not second:
            yf_ref[pl.ds(start, q), :] = y
        else:
            y = y + yf_ref[pl.ds(start, q), :] + dsk_ref[...] * xs
            y = y * _silu(z_ref[pl.ds(start, q), :].astype(F32))
            ms = jnp.mean(y * y, axis=-1, keepdims=True)
            y_ref[pl.ds(start, q), :] = (y * lax.rsqrt(ms + EPS) * ng_ref[...]).astype(BF16)

    seg_l = (zl_ref, xl_ref, bl_ref, cl_ref, dtl_ref, act_l, yf_l, yl_ref, n_lat)
    seg_c = (zc_ref, xc_ref, bc_ref, cc_ref, dtc_ref, act_c, yf_c, yc_ref, n_ctx)
    st_refs = (stf_ref, stb_ref)
    stf_ref[...] = jnp.zeros_like(stf_ref)
    stb_ref[...] = jnp.zeros_like(stb_ref)

    for c in range(n_ctx):
        conv_chunk(seg_c, c)

    def conv_body(i, carry):
        conv_chunk(seg_l, i)
        return carry

    lax.fori_loop(0, n_lat, conv_body, 0)

    for i in range(n_ctx):
        chunk(seg_c, i, 0, 2 * i >= n_ctx)
        chunk(seg_c, n_ctx - 1 - i, 1, 2 * i >= n_ctx)

    def pair(second):
        def body(i, carry):
            chunk(seg_l, i, 0, second)
            chunk(seg_l, n_lat - 1 - i, 1, second)
            return carry
        return body

    lax.fori_loop(0, n_lat // 2, pair(False), 0)
    lax.fori_loop(n_lat // 2, n_lat, pair(True), 0)


def _ssd(p, pdt, conv_w, conv_b, a_log, dt_bias, d_skip, norm_g, bsz, seq, ctx):
    t = p.shape[0]
    n_lat, n_ctx = seq // CHUNK, ctx // CHUNK
    assert n_lat % 2 == 0 and n_ctx % 2 == 0
    cb0 = (bsz * seq) // ctx
    tri, shifts = _ssd_consts()
    cw = jnp.zeros((8, 3 * MIX_W), F32).at[:SSD_CONV].set(conv_w)
    a_n = jnp.zeros((1, 128), F32).at[0, :8].set(-jnp.exp(a_log.astype(F32)).reshape(8))
    bias_n = jnp.zeros((1, 128), F32).at[0, :8].set(dt_bias.astype(F32).reshape(8))
    dsk = jnp.repeat(d_skip.astype(F32), HEAD_DIM).reshape(1, MIX_W)

    def lat(col):
        return pl.BlockSpec((seq, MIX_W), lambda b, col=col: (b, col // MIX_W))

    def cx(col):
        return pl.BlockSpec((ctx, MIX_W), lambda b, col=col: (cb0 + b, col // MIX_W))

    def full(shape):
        return pl.BlockSpec(shape, lambda b: (0,) * len(shape))

    kern = functools.partial(_ssd_kernel, n_lat=n_lat, n_ctx=n_ctx)
    yl, yc = pl.pallas_call(
        kern,
        out_shape=(jax.ShapeDtypeStruct((bsz * seq, MIX_W), BF16),
                   jax.ShapeDtypeStruct((bsz * ctx, MIX_W), BF16)),
        grid=(bsz,),
        in_specs=[lat(C_Z), lat(C_AX), lat(C_AB), lat(C_AC),
                  pl.BlockSpec((seq, 128), lambda b: (b, 0)),
                  cx(C_Z), cx(C_AX), cx(C_AB), cx(C_AC),
                  pl.BlockSpec((ctx, 128), lambda b: (cb0 + b, 0)),
                  full((8, 3 * MIX_W)), full((1, 3 * MIX_W)), full((1, 128)), full((1, 128)),
                  full((1, MIX_W)), full((1, MIX_W)), full((2, CHUNK, CHUNK)),
                  full((4, CHUNK, CHUNK + 2 * HALO))],
        out_specs=(pl.BlockSpec((seq, MIX_W), lambda b: (b, 0)),
                   pl.BlockSpec((ctx, MIX_W), lambda b: (b, 0))),
        scratch_shapes=[pltpu.VMEM((seq, 3 * MIX_W), BF16), pltpu.VMEM((ctx, 3 * MIX_W), BF16),
                        pltpu.VMEM((seq, MIX_W), F32), pltpu.VMEM((ctx, MIX_W), F32),
                        pltpu.VMEM((SSD_STATE, MIX_W), F32), pltpu.VMEM((SSD_STATE, MIX_W), F32)],
        compiler_params=_cp("parallel"),
        name="ssd",
    )(p, p, p, p, pdt, p, p, p, p, pdt, cw, conv_b.reshape(1, -1).astype(F32), a_n, bias_n,
      dsk, norm_g.reshape(1, MIX_W).astype(F32), tri, shifts)
    return jnp.concatenate([yl, yc], axis=0)


def _s5_discretize(lam_re, lam_im, log_step, b_re, b_im):
    step = jnp.exp(log_step.astype(F32))[:, None]
    lr = jnp.minimum(lam_re.astype(F32), -1e-4)
    li = lam_im.astype(F32)
    mag = jnp.exp(lr * step)
    ang = li * step
    ab_re, ab_im = mag * jnp.cos(ang), mag * jnp.sin(ang)
    den = lr * lr + li * li
    f_re = ((ab_re - 1.0) * lr + ab_im * li) / den
    f_im = (ab_im * lr - (ab_re - 1.0) * li) / den
    br, bi = b_re.astype(F32), b_im.astype(F32)
    bb_re = f_re[..., None] * br - f_im[..., None] * bi
    bb_im = f_re[..., None] * bi + f_im[..., None] * br
    return ab_re, ab_im, bb_re, bb_im


def _s5_mats(lam_re, lam_im, log_step, b_re, b_im, c_re, c_im):
    eye = jnp.eye(S5_GROUPS, dtype=F32)
    a_all, b_all, c_all = [], [], []
    for d in range(2):
        ab_re, ab_im, bb_re, bb_im = _s5_discretize(lam_re[d], lam_im[d], log_step[d], b_re[d], b_im[d])
        bm = [jnp.einsum('gnp,gh->gphn', m, eye).reshape(MIX_W, S5_LANES) for m in (bb_re, bb_im)]
        cm = [jnp.einsum('gpn,gh->gnhp', m.astype(F32), eye).reshape(S5_LANES, MIX_W)
              for m in (c_re[d], c_im[d])]
        b_all.append(jnp.concatenate(bm, axis=1))
        c_all.append(jnp.concatenate([cm[0], -cm[1]], axis=0))
        a_all.append(jnp.concatenate([ab_re.reshape(1, S5_LANES), ab_im.reshape(1, S5_LANES)], axis=1))
    a = jnp.broadcast_to(jnp.stack(a_all), (2, 8, 2 * S5_LANES))
    return a, jnp.stack(b_all).astype(BF16), jnp.stack(c_all).astype(BF16)


def _s5_kernel(uf_ref, ub_ref, a_ref, b_ref, c_ref, yf_ref, yb_ref, buf_ref, s_ref):
    n = S5_LANES

    @pl.when(pl.program_id(0) == 0)
    def _():
        s_ref[...] = jnp.zeros_like(s_ref)

    for d, u_ref in enumerate((uf_ref, ub_ref)):
        buf_ref[d] = _dot(u_ref[...], b_ref[d])
    state = [(s_ref[d, :, 0:n], s_ref[d, :, n:2 * n]) for d in range(2)]
    for j in range(S5_STEPS):
        for d in range(2):
            jj = j if d == 0 else S5_STEPS - 1 - j
            rows = slice(jj * 8, jj * 8 + 8)
            a_re, a_im = a_ref[d, :, 0:n], a_ref[d, :, n:2 * n]
            s_re, s_im = state[d]
            n_re = a_re * s_re - a_im * s_im + buf_ref[d, rows, 0:n]
            n_im = a_re * s_im + a_im * s_re + buf_ref[d, rows, n:2 * n]
            buf_ref[d, rows, 0:n] = n_re
            buf_ref[d, rows, n:2 * n] = n_im
            state[d] = (n_re, n_im)
    for d, y_ref in enumerate((yf_ref, yb_ref)):
        s_ref[d, :, 0:n] = state[d][0]
        s_ref[d, :, n:2 * n] = state[d][1]
        y_ref[...] = _dot(buf_ref[d].astype(BF16), c_ref[d])


def _s5_finish_kernel(yf_ref, yb_ref, u_ref, d_ref, w_ref, o_ref):
    y = yf_ref[...] + yb_ref[...] + d_ref[...] * u_ref[...].astype(F32)
    v = jax.nn.gelu(y, approximate=True).astype(BF16)
    r = _dot(v, w_ref[...])
    o_ref[...] = (r[:, 0:MIX_W] * jax.nn.sigmoid(r[:, MIX_W:2 * MIX_W])).astype(BF16)


def _s5(p, lam_re, lam_im, log_step, b_re, b_im, c_re, c_im, d_skip, glu_w, bsz, seq, ctx):
    assert bsz <= 8
    t_lat = bsz * seq
    u = p[:, C_BU:C_BU + MIX_W]
    u_l = jnp.transpose(u[:t_lat].reshape(bsz, seq, MIX_W), (1, 0, 2))
    u_c = jnp.transpose(u[t_lat:].reshape(bsz, ctx, MIX_W), (1, 0, 2))
    u_tm = jnp.concatenate([u_c, u_l], axis=0)
    if bsz < 8:
        u_tm = jnp.pad(u_tm, ((0, 0), (0, 8 - bsz), (0, 0)))
    steps = seq + ctx
    u_tm = u_tm.reshape(steps * 8, MIX_W)
    a, bmat, cmat = _s5_mats(lam_re, lam_im, log_step, b_re, b_im, c_re, c_im)
    rows = S5_STEPS * 8
    nc, ncc = steps // S5_STEPS, ctx // S5_STEPS

    def bwd(i):
        return jnp.where(i < ncc, ncc - 1 - i, nc + ncc - 1 - i)

    def whole(shape):
        return pl.BlockSpec(shape, lambda i: (0,) * len(shape))

    yshape = jax.ShapeDtypeStruct((steps * 8, MIX_W), F32)
    yf, yb = pl.pallas_call(
        _s5_kernel,
        out_shape=(yshape, yshape),
        grid=(nc,),
        in_specs=[pl.BlockSpec((rows, MIX_W), lambda i: (i, 0)),
                  pl.BlockSpec((rows, MIX_W), lambda i: (bwd(i), 0)),
                  whole((2, 8, 2 * S5_LANES)), whole((2, MIX_W, 2 * S5_LANES)),
                  whole((2, 2 * S5_LANES, MIX_W))],
        out_specs=(pl.BlockSpec((rows, MIX_W), lambda i: (i, 0)),
                   pl.BlockSpec((rows, MIX_W), lambda i: (bwd(i), 0))),
        scratch_shapes=[pltpu.VMEM((2, rows, 2 * S5_LANES), F32), pltpu.VMEM((2, 8, 2 * S5_LANES), F32)],
        compiler_params=_cp("arbitrary"),
        name="s5_scan",
    )(u_tm, u_tm, a, bmat, cmat)

    tmf = _pow2_tile(2048, steps * 8)
    o = pl.pallas_call(
        _s5_finish_kernel,
        out_shape=jax.ShapeDtypeStruct((steps * 8, MIX_W), BF16),
        grid=(steps * 8 // tmf,),
        in_specs=[pl.BlockSpec((tmf, MIX_W), lambda i: (i, 0)),
                  pl.BlockSpec((tmf, MIX_W), lambda i: (i, 0)),
                  pl.BlockSpec((tmf, MIX_W), lambda i: (i, 0)),
                  pl.BlockSpec((1, MIX_W), lambda i: (0, 0)),
                  pl.BlockSpec((MIX_W, 2 * MIX_W), lambda i: (0, 0))],
        out_specs=pl.BlockSpec((tmf, MIX_W), lambda i: (i, 0)),
        compiler_params=_cp("parallel"),
        name="s5_finish",
    )(yf, yb, u_tm, d_skip.reshape(1, MIX_W).astype(F32), glu_w.astype(BF16))
    o = o.reshape(steps, 8, MIX_W)[:, :bsz]
    o_c = jnp.transpose(o[:ctx], (1, 0, 2)).reshape(bsz * ctx, MIX_W)
    o_l = jnp.transpose(o[ctx:], (1, 0, 2)).reshape(t_lat, MIX_W)
    return jnp.concatenate([o_l, o_c], axis=0)


def _rope_tables(seq, tm):
    rows = seq // GRID_W
    pos_r = jnp.repeat(jnp.arange(rows, dtype=F32), GRID_W)
    pos_c = jnp.tile(jnp.arange(GRID_W, dtype=F32), rows)
    inv = ROPE_BASE ** (-jnp.arange(ROPE_FREQS, dtype=F32) / ROPE_FREQS)
    ar, ac = pos_r[:, None] * inv, pos_c[:, None] * inv
    cos = jnp.concatenate([jnp.cos(ar), jnp.cos(ar), jnp.cos(ac), jnp.cos(ac)], axis=-1)
    sin = jnp.concatenate([-jnp.sin(ar), jnp.sin(ar), -jnp.sin(ac), jnp.sin(ac)], axis=-1)
    cos = jnp.concatenate([jnp.tile(cos, (1, 2)), jnp.ones((tm, 128), F32)], axis=0)
    sin = jnp.concatenate([jnp.tile(sin, (1, 2)), jnp.zeros((tm, 128), F32)], axis=0)
    return cos, sin


def _prep_consts():
    i = np.arange(MIX_W)
    bd = ((i[:, None] // HEAD_DIM) == (i[None, :] // HEAD_DIM)).astype(np.float32) / HEAD_DIM
    pm = (i[:, None] == (i[None, :] ^ ROPE_FREQS)).astype(np.float32)
    return jnp.asarray(bd, BF16), jnp.asarray(pm, BF16)


def _prep_kernel(cq_ref, dq_ref, ck_ref, dk_ref, cos_ref, sin_ref, qg_ref, kg_ref, bd_ref, pm_ref,
                 q1_ref, q2_ref, k1_ref, k2_ref):
    cos, sin = cos_ref[...], sin_ref[...]
    cos2 = jnp.concatenate([cos, cos], axis=-1)
    sin2 = jnp.concatenate([sin, sin], axis=-1)
    bd, pm = bd_ref[...], pm_ref[...]
    tm = cos.shape[0]
    lane = lax.broadcasted_iota(jnp.int32, (tm, 128), 1)

    def rms(x, g, n):
        ms = _dot((x * x).astype(BF16), bd[:n, :n])
        return x * lax.rsqrt(ms + EPS) * g

    def rope(y, c, s, n):
        return y * c + _dot(y.astype(BF16), pm[:n, :n]) * s

    def store_q(q, ref):
        q = q * ATTN_SCALE
        for kv in range(2):
            for g in range(2):
                half = q[:, 128 * kv:128 * (kv + 1)]
                if g != kv:
                    half = pltpu.roll(half, HEAD_DIM, 1)
                keep = (lane >= HEAD_DIM * kv) & (lane < HEAD_DIM * (kv + 1))
                ref[2 * kv + g] = jnp.where(keep, half, 0.0).astype(BF16)

    store_q(rope(rms(cq_ref[...].astype(F32), qg_ref[...], MIX_W), cos2, sin2, MIX_W), q1_ref)
    store_q(rope(dq_ref[...].astype(F32), cos2, sin2, MIX_W), q2_ref)
    k1 = rope(rms(ck_ref[...].astype(F32), kg_ref[...], 128), cos, sin, 128)
    for u in range(tm // KT_UNIT):
        k1_ref[u] = k1[u * KT_UNIT:(u + 1) * KT_UNIT].T.astype(BF16)
    k2 = rope(dk_ref[...].astype(F32), cos, sin, 128)
    for u in range(tm // WINDOW):
        k2_ref[u] = k2[u * WINDOW:(u + 1) * WINDOW].T.astype(BF16)


def _prep(p, qk_gain, t_lat, seq, tm):
    t = p.shape[0]
    cos, sin = _rope_tables(seq, tm)
    bd, pm = _prep_consts()
    qg = jnp.tile(qk_gain[0].astype(F32), N_HEADS).reshape(1, MIX_W)
    kg = jnp.tile(qk_gain[1].astype(F32), 2).reshape(1, 128)
    nt = seq // tm

    def tab(i):
        return (jnp.where(i * tm >= t_lat, nt, i % nt), 0)

    def const(shape):
        return pl.BlockSpec(shape, lambda i: (0,) * len(shape))

    qshape = jax.ShapeDtypeStruct((N_HEADS, t, 128), BF16)
    ktshape = jax.ShapeDtypeStruct((t // KT_UNIT, 128, KT_UNIT), BF16)
    kt2shape = jax.ShapeDtypeStruct((t // WINDOW, 128, WINDOW), BF16)
    return pl.pallas_call(
        _prep_kernel,
        out_shape=(qshape, qshape, ktshape, kt2shape),
        grid=(t // tm,),
        in_specs=[pl.BlockSpec((tm, MIX_W), lambda i: (i, C_CQ // MIX_W)),
                  pl.BlockSpec((tm, MIX_W), lambda i: (i, C_DQ // MIX_W)),
                  pl.BlockSpec((tm, 128), lambda i: (i, C_CK // 128)),
                  pl.BlockSpec((tm, 128), lambda i: (i, C_DK // 128)),
                  pl.BlockSpec((tm, 128), tab), pl.BlockSpec((tm, 128), tab),
                  const((1, MIX_W)), const((1, 128)), const((MIX_W, MIX_W)), const((MIX_W, MIX_W))],
        out_specs=(pl.BlockSpec((N_HEADS, tm, 128), lambda i: (0, i, 0)),
                   pl.BlockSpec((N_HEADS, tm, 128), lambda i: (0, i, 0)),
                   pl.BlockSpec((tm // KT_UNIT, 128, KT_UNIT), lambda i: (i, 0, 0)),
                   pl.BlockSpec((tm // WINDOW, 128, WINDOW), lambda i: (i, 0, 0))),
        compiler_params=_cp("parallel"),
        name="qk_prep",
    )(p, p, p, p, cos, sin, qg, kg, bd, pm)


def _pack_heads(o, tq):
    lane = lax.broadcasted_iota(jnp.int32, (tq, 128), 1)
    left = lane < HEAD_DIM
    o00, o01, o10, o11 = [o[h * tq:(h + 1) * tq] for h in range(N_HEADS)]
    out0 = jnp.where(left, o00, pltpu.roll(o01, HEAD_DIM, 1))
    out1 = jnp.where(left, pltpu.roll(o10, HEAD_DIM, 1), o11)
    return jnp.concatenate([out0, out1], axis=-1)


def _gattn_kernel(q_ref, kl_ref, vl_ref, kc_ref, vc_ref, o_ref, m_ref, l_ref, acc_ref,
                  *, tq, units, n_qb_lat, n_kvb):
    qb = pl.program_id(1)

    def tree(op, xs):
        while len(xs) > 1:
            xs = [op(xs[i], xs[i + 1]) for i in range(0, len(xs) - 1, 2)] + ([xs[-1]] if len(xs) % 2 else [])
        return xs[0]

    def scores(h, kts):
        q = q_ref[h]
        cols = []
        for kt in kts:
            s = _dot(q, kt)
            cols += [s[:, 128 * c:128 * (c + 1)] for c in range(s.shape[1] // 128)]
        return cols

    def softmax(h, cols, first):
        m_blk = jnp.max(tree(jnp.maximum, cols), axis=-1, keepdims=True)
        if first:
            m_new = jnp.broadcast_to(m_blk, (tq, 128))
            alpha = None
            ps = [jnp.exp(c - m_new) for c in cols]
            l_ref[h] = tree(jnp.add, ps)
        else:
            m_old = m_ref[h]
            m_new = jnp.maximum(m_old, m_blk)
            alpha = jnp.exp(m_old - m_new)
            ps = [jnp.exp(c - m_new) for c in cols]
            l_ref[h] = alpha * l_ref[h] + tree(jnp.add, ps)
        m_ref[h] = m_new
        return alpha, jnp.concatenate(ps, axis=-1).astype(BF16)

    def weighted(h, alpha, p, v):
        pv = _dot(p, v)
        acc_ref[h] = pv if alpha is None else alpha * acc_ref[h] + pv

    def block(kts, v, first):
        cols = scores(0, kts)
        for h in range(N_HEADS):
            nxt = scores(h + 1, kts) if h + 1 < N_HEADS else None
            alpha, p = softmax(h, cols, first)
            weighted(h, alpha, p, v)
            cols = nxt

    block([kc_ref[u] for u in range(kc_ref.shape[0])], vc_ref[...], True)

    def body(j, carry):
        rows = pl.ds(pl.multiple_of(j * (units * KT_UNIT), units * KT_UNIT), units * KT_UNIT)
        block([kl_ref[j * units + u] for u in range(units)], vl_ref[rows, :], False)
        return carry

    lax.fori_loop(0, jnp.where(qb < n_qb_lat, n_kvb, 0), body, 0)
    o = [acc_ref[h] / jnp.sum(l_ref[h], axis=-1, keepdims=True) for h in range(N_HEADS)]
    o_ref[...] = _pack_heads(jnp.concatenate(o, axis=0), tq).astype(BF16)


def _gattn(qp, kt, p, bsz, seq, ctx, with_ctx):
    t_lat = bsz * seq
    tq = ctx
    units = 4
    n_qb_lat = seq // tq
    n_qb = n_qb_lat + (1 if with_ctx else 0)
    cb0 = t_lat // ctx
    t_out = t_lat + (bsz * ctx if with_ctx else 0)
    assert ctx % KT_UNIT == 0 and seq % (units * KT_UNIT) == 0

    def qrow(b, i):
        return jnp.where(i < n_qb_lat, b * n_qb_lat + i, cb0 + b)

    kern = functools.partial(_gattn_kernel, tq=tq, units=units, n_qb_lat=n_qb_lat,
                             n_kvb=seq // (units * KT_UNIT))
    return pl.pallas_call(
        kern,
        out_shape=jax.ShapeDtypeStruct((t_out, MIX_W), BF16),
        grid=(bsz, n_qb),
        in_specs=[pl.BlockSpec((N_HEADS, tq, 128), lambda b, i: (0, qrow(b, i), 0)),
                  pl.BlockSpec((seq // KT_UNIT, 128, KT_UNIT), lambda b, i: (b, 0, 0)),
                  pl.BlockSpec((seq, 128), lambda b, i: (b, C_CV // 128)),
                  pl.BlockSpec((ctx // KT_UNIT, 128, KT_UNIT), lambda b, i: (cb0 + b, 0, 0)),
                  pl.BlockSpec((ctx, 128), lambda b, i: (cb0 + b, C_CV // 128))],
        out_specs=pl.BlockSpec((tq, MIX_W), lambda b, i: (qrow(b, i), 0)),
        scratch_shapes=[pltpu.VMEM((N_HEADS, tq, 128), F32)] * 3,
        compiler_params=_cp("parallel", "arbitrary"),
        name="global_attn",
    )(qp, kt, p, kt, p)


def _wattn_kernel(q_ref, kl_ref, vl_ref, kc_ref, vc_ref, sink_ref, o_ref, *, nb):
    w = WINDOW
    n = pl.program_id(1)
    is_lat = n < nb
    rows = 2 * w
    qi = lax.broadcasted_iota(jnp.int32, (rows, w), 0) & (w - 1)
    kj = lax.broadcasted_iota(jnp.int32, (rows, w), 1)
    band = (jnp.clip(n - 1, 0, nb - 1), jnp.clip(n, 0, nb - 1), jnp.clip(n + 1, 0, nb - 1))
    off_prev = jnp.where(is_lat & (n >= 1), 0, w)
    off_cur = jnp.where(is_lat, 0, w)
    off_next = jnp.where(n + 1 < nb, 0, w)
    masks = [kj >= qi + off_prev, kj >= off_cur, kj <= qi - off_next]
    kts = [kc_ref[u] for u in range(kc_ref.shape[0])] + [kl_ref[i] for i in band]
    n_ctx_tiles = kc_ref.shape[0]
    v_all = jnp.concatenate([vc_ref[...]] + [vl_ref[pl.ds(pl.multiple_of(i * w, w), w), :] for i in band],
                            axis=0)

    def tree(op, xs):
        while len(xs) > 1:
            xs = [op(xs[i], xs[i + 1]) for i in range(0, len(xs) - 1, 2)] + ([xs[-1]] if len(xs) % 2 else [])
        return xs[0]

    def scores(c):
        q = q_ref[2 * c:2 * c + 2].reshape(rows, 128)
        tiles = [_dot(q, kt) for kt in kts]
        return tiles[:n_ctx_tiles] + [jnp.where(mk, t, NEG_INF) for mk, t in zip(masks, tiles[n_ctx_tiles:])]

    def finish(c, tiles):
        sink = sink_ref[c * rows:(c + 1) * rows, :]
        m = jnp.maximum(jnp.max(tree(jnp.maximum, tiles), axis=-1, keepdims=True), sink)
        ps = [jnp.exp(t - m) for t in tiles]
        den = jnp.sum(tree(jnp.add, ps), axis=-1, keepdims=True) + jnp.exp(sink - m)
        return _dot(jnp.concatenate(ps, axis=-1).astype(BF16), v_all) / den

    tiles = scores(0)
    nxt = scores(1)
    o = [finish(0, tiles), finish(1, nxt)]
    o_ref[...] = _pack_heads(jnp.concatenate(o, axis=0), w).astype(BF16)


def _wattn(qp, k, p, sink, bsz, seq, ctx, with_ctx):
    t_lat = bsz * seq
    w = WINDOW
    nb = seq // w
    ncb = ctx // w
    n_qb = nb + (ncb if with_ctx else 0)
    cq0 = t_lat // w
    cb0 = t_lat // ctx
    t_out = t_lat + (bsz * ctx if with_ctx else 0)
    sink_rows = jnp.broadcast_to(jnp.repeat(sink.astype(F32), w)[:, None], (N_HEADS * w, 128))

    def qrow(b, i):
        return jnp.where(i < nb, b * nb + i, cq0 + b * ncb + (i - nb))

    return pl.pallas_call(
        functools.partial(_wattn_kernel, nb=nb),
        out_shape=jax.ShapeDtypeStruct((t_out, MIX_W), BF16),
        grid=(bsz, n_qb),
        in_specs=[pl.BlockSpec((N_HEADS, w, 128), lambda b, i: (0, qrow(b, i), 0)),
                  pl.BlockSpec((seq // w, 128, w), lambda b, i: (b, 0, 0)),
                  pl.BlockSpec((seq, 128), lambda b, i: (b, C_DV // 128)),
                  pl.BlockSpec((ctx // w, 128, w), lambda b, i: (cb0 + b, 0, 0)),
                  pl.BlockSpec((ctx, 128), lambda b, i: (cb0 + b, C_DV // 128)),
                  pl.BlockSpec((N_HEADS * w, 128), lambda b, i: (0, 0))],
        out_specs=pl.BlockSpec((w, MIX_W), lambda b, i: (qrow(b, i), 0)),
        compiler_params=_cp("parallel", "arbitrary"),
        name="window_attn",
    )(qp, k, p, k, p, sink_rows)


def _merge_kernel(ya_ref, yb_ref, yc_ref, yd_ref, gate_ref, x_ref, mod_ref, wbr_ref, wout_ref,
                  g2_ref, xo_ref, h2_ref):
    d = x_ref.shape[1]
    acc = None
    for n, ref in enumerate((ya_ref, yb_ref, yc_ref, yd_ref)):
        gate = jax.nn.sigmoid(gate_ref[:, n * d:(n + 1) * d].astype(F32))
        term = gate * _dot(ref[...], wbr_ref[n])
        acc = term if acc is None else acc + term
    x = x_ref[...] + mod_ref[0, 2:3, :] * _dot(acc.astype(BF16), wout_ref[...])
    xo_ref[...] = x
    ms = jnp.mean(x * x, axis=-1, keepdims=True)
    y = x * lax.rsqrt(ms + EPS) * g2_ref[...]
    h2_ref[...] = y * (1.0 + mod_ref[0, 4:5, :]) + mod_ref[0, 3:4, :]


def _merge(ys, p, x, mod, wbr, wout, g2, t_out, t_lat, seq, tm):
    d = x.shape[1]

    def row(width):
        return pl.BlockSpec((tm, width), lambda i: (i, 0))

    def const(shape):
        return pl.BlockSpec(shape, lambda i: (0,) * len(shape))

    return pl.pallas_call(
        _merge_kernel,
        out_shape=(jax.ShapeDtypeStruct((t_out, d), F32), jax.ShapeDtypeStruct((t_out, d), F32)),
        grid=(t_out // tm,),
        in_specs=[row(MIX_W)] * 4 + [row(4 * d), row(d),
                  pl.BlockSpec((1, 6, d), lambda i: (_mod_group(i * tm, t_lat, seq), 0, 0)),
                  const((4, MIX_W, d)), const((d, d)), const((1, d))],
        out_specs=(row(d), row(d)),
        compiler_params=_cp("parallel"),
        name="merge",
    )(*ys, p, x, mod, wbr, wout, g2.reshape(1, d))


def _ffn_dense_kernel(h_ref, x_ref, mod_ref, wg_ref, wu_ref, wo_ref, o_ref, hb_ref, acc_ref):
    j = pl.program_id(1)

    @pl.when(j == 0)
    def _():
        hb_ref[...] = h_ref[...].astype(BF16)
        acc_ref[...] = jnp.zeros_like(acc_ref)

    h = hb_ref[...]
    a = _silu(_dot(h, wg_ref[...])) * _dot(h, wu_ref[...])
    acc_ref[...] += _dot(a.astype(BF16), wo_ref[...])

    @pl.when(j == pl.num_programs(1) - 1)
    def _():
        o_ref[...] = x_ref[...] + mod_ref[0, 5:6, :] * acc_ref[...]


def _ffn_dense(h2, x, mod, w_in, w_out, t_lat, seq, tm):
    t, d = x.shape
    f = w_out.shape[0]
    tf = 256
    nf = f // tf
    return pl.pallas_call(
        _ffn_dense_kernel,
        out_shape=jax.ShapeDtypeStruct((t, d), F32),
        grid=(t // tm, nf),
        in_specs=[pl.BlockSpec((tm, d), lambda i, j: (i, 0)),
                  pl.BlockSpec((tm, d), lambda i, j: (i, 0)),
                  pl.BlockSpec((1, 6, d), lambda i, j: (_mod_group(i * tm, t_lat, seq), 0, 0)),
                  pl.BlockSpec((d, tf), lambda i, j: (0, j)),
                  pl.BlockSpec((d, tf), lambda i, j: (0, nf + j)),
                  pl.BlockSpec((tf, d), lambda i, j: (j, 0))],
        out_specs=pl.BlockSpec((tm, d), lambda i, j: (i, 0)),
        scratch_shapes=[pltpu.VMEM((tm, d), BF16), pltpu.VMEM((tm, d), F32)],
        compiler_params=_cp("parallel", "arbitrary"),
        name="ffn_dense",
    )(h2, x, mod, w_in, w_in, w_out)


def _router_kernel(h_ref, w_ref, e_ref, g1_ref, g2_ref):
    h = h_ref[...]
    h_hi = h.astype(BF16)
    h_lo = (h - h_hi.astype(F32)).astype(BF16)
    logits = _dot(h_hi, w_ref[0]) + _dot(h_lo, w_ref[0]) + _dot(h_hi, w_ref[1])
    lane = lax.broadcasted_iota(jnp.int32, logits.shape, 1)
    lane_f = lane.astype(F32)
    logits = jnp.where(lane < N_EXPERTS, logits, -jnp.inf)
    m1 = jnp.max(logits, axis=-1, keepdims=True)
    i1 = jnp.min(jnp.where(logits == m1, lane_f, 128.0), axis=-1, keepdims=True)
    rest = jnp.where(lane_f == i1, -jnp.inf, logits)
    m2 = jnp.max(rest, axis=-1, keepdims=True)
    i2 = jnp.min(jnp.where(rest == m2, lane_f, 128.0), axis=-1, keepdims=True)
    e2 = jnp.exp(m2 - m1)
    g1 = 1.0 / (1.0 + e2)
    e_ref[...] = jnp.where(lane == 0, i1, jnp.where(lane == 1, i2, 0.0)).astype(jnp.int32)
    g1_ref[...] = jnp.broadcast_to(g1, g1_ref.shape)
    g2_ref[...] = jnp.broadcast_to(e2 * g1, g2_ref.shape)


def _router(h2, router, tm):
    t, d = h2.shape
    r = jnp.zeros((d, 128), F32).at[:, :N_EXPERTS].set(router.astype(F32))
    r_hi = r.astype(BF16)
    r_lo = (r - r_hi.astype(F32)).astype(BF16)
    shp = jax.ShapeDtypeStruct((t, 128), F32)
    return pl.pallas_call(
        _router_kernel,
        out_shape=(jax.ShapeDtypeStruct((t, 128), jnp.int32), shp, shp),
        grid=(t // tm,),
        in_specs=[pl.BlockSpec((tm, d), lambda i: (i, 0)),
                  pl.BlockSpec((2, d, 128), lambda i: (0, 0, 0))],
        out_specs=(pl.BlockSpec((tm, 128), lambda i: (i, 0)),) * 3,
        compiler_params=_cp("parallel"),
        name="router",
    )(h2, jnp.stack([r_hi, r_lo]))


def _experts_kernel(te_ref, tok0_ref, tokn_ref, dst_ref, h_ref, wg_ref, wu_ref, wo_ref, y_ref,
                    xbuf, ybuf, xb_ref, acc_ref, sem_in, sem_out, *, n_tiles, tm):
    i, j = pl.program_id(0), pl.program_id(1)
    last_j = pl.num_programs(1) - 1
    slot = i % 2
    other = 1 - slot
    per_step = tm // MOE_STEPS

    def gather(idx_ref, r, s):
        return pltpu.make_async_copy(h_ref.at[pl.ds(idx_ref[0, 0, r], 1)], xbuf.at[s, pl.ds(r, 1)],
                                     sem_in.at[s])

    def scatter(r, s):
        return pltpu.make_async_copy(ybuf.at[s, pl.ds(r, 1)], y_ref.at[pl.ds(dst_ref[0, 0, r], 1)],
                                     sem_out.at[s])

    def wait_tile(copy_of_row0):
        def body(k, carry):
            for _ in range(MOE_WAITS):
                copy_of_row0().wait()
            return carry

        lax.fori_loop(0, tm // MOE_WAITS, body, 0)

    def wait_gather(s):
        wait_tile(lambda: pltpu.make_async_copy(h_ref.at[pl.ds(0, 1)], xbuf.at[s, pl.ds(0, 1)], sem_in.at[s]))

    def wait_scatter(s):
        wait_tile(lambda: pltpu.make_async_copy(ybuf.at[s, pl.ds(0, 1)], y_ref.at[pl.ds(0, 1)], sem_out.at[s]))

    def row_traffic(with_gather, with_scatter):
        def rows(base, count):
            for k in range(count):
                if with_gather:
                    gather(tokn_ref, base + k, other).start()
                if with_scatter:
                    scatter(base + k, other).start()

        rows(j * per_step, per_step)

        @pl.when(j == last_j)
        def _():
            rows(per_step * MOE_STEPS, tm - per_step * MOE_STEPS)

    def compute():
        x = xb_ref[...]
        a = _silu(_dot(x, wg_ref[0])) * _dot(x, wu_ref[0])
        acc_ref[...] += _dot(a.astype(BF16), wo_ref[0])

    @pl.when((i == 0) & (j == 0))
    def _():
        def body(r, carry):
            gather(tok0_ref, r, 0).start()
            return carry

        lax.fori_loop(0, tm, body, 0)

    @pl.when(j == 0)
    def _():
        wait_gather(slot)

        @pl.when(i < n_tiles)
        def _():
            xb_ref[...] = xbuf[slot].astype(BF16)
            acc_ref[...] = jnp.zeros_like(acc_ref)

    @pl.when(i == 0)
    def _():
        compute()
        row_traffic(True, False)

    @pl.when((i > 0) & (i < n_tiles))
    def _():
        compute()
        row_traffic(True, True)

    @pl.when(i == n_tiles)
    def _():
        row_traffic(False, True)

    @pl.when(j == last_j)
    def _():
        @pl.when((i >= 2) & (i < n_tiles))
        def _():
            wait_scatter(slot)

        @pl.when(i < n_tiles)
        def _():
            ybuf[slot] = acc_ref[...]

        @pl.when(i == n_tiles)
        def _():
            wait_scatter(slot)
            wait_scatter(other)


def _experts(h2, src_tok, dst_row, tile_e, w_in, w_out, n_out_rows):
    t, d = h2.shape
    n_tiles = dst_row.shape[0]
    f = w_out.shape[1]
    tf = f // MOE_STEPS
    tm = MOE_TILE
    assert n_tiles >= 2 and tm % MOE_WAITS == 0 and tf % 128 == 0

    def smem(index_map):
        return pl.BlockSpec((1, 1, tm), index_map, memory_space=pltpu.SMEM)

    grid_spec = pltpu.PrefetchScalarGridSpec(
        num_scalar_prefetch=1,
        grid=(n_tiles + 1, MOE_STEPS),
        in_specs=[smem(lambda i, j, te: (0, 0, 0)),
                  smem(lambda i, j, te: (jnp.minimum(i + 1, n_tiles), 0, 0)),
                  smem(lambda i, j, te: (jnp.clip(i - 1, 0, n_tiles - 1), 0, 0)),
                  pl.BlockSpec(memory_space=pl.ANY),
                  pl.BlockSpec((1, d, tf), lambda i, j, te: (te[i], 0, j)),
                  pl.BlockSpec((1, d, tf), lambda i, j, te: (te[i], 0, MOE_STEPS + j)),
                  pl.BlockSpec((1, tf, d), lambda i, j, te: (te[i], j, 0))],
        out_specs=pl.BlockSpec(memory_space=pl.ANY),
        scratch_shapes=[pltpu.VMEM((2, tm, d), F32), pltpu.VMEM((2, tm, d), F32),
                        pltpu.VMEM((tm, d), BF16), pltpu.VMEM((tm, d), F32),
                        pltpu.SemaphoreType.DMA((2,)), pltpu.SemaphoreType.DMA((2,))])
    return pl.pallas_call(
        functools.partial(_experts_kernel, n_tiles=n_tiles, tm=tm),
        out_shape=jax.ShapeDtypeStruct((n_out_rows, d), F32),
        grid_spec=grid_spec,
        compiler_params=_cp("arbitrary", "arbitrary"),
        name="experts",
    )(tile_e, src_tok, src_tok, dst_row, h2, w_in, w_in, w_out)


def _combine_kernel(y_ref, x_ref, mod_ref, g1_ref, g2_ref, o_ref):
    d = x_ref.shape[1]
    g1 = jnp.concatenate([g1_ref[...]] * (d // 128), axis=-1)
    g2 = jnp.concatenate([g2_ref[...]] * (d // 128), axis=-1)
    o_ref[...] = x_ref[...] + mod_ref[0, 5:6, :] * (g1 * y_ref[:, 0:d] + g2 * y_ref[:, d:2 * d])


def _combine(y_pairs, x, mod, g1, g2, t_lat, seq, tm):
    t, d = x.shape
    return pl.pallas_call(
        _combine_kernel,
        out_shape=jax.ShapeDtypeStruct((t, d), F32),
        grid=(t // tm,),
        in_specs=[pl.BlockSpec((tm, 2 * d), lambda i: (i, 0)),
                  pl.BlockSpec((tm, d), lambda i: (i, 0)),
                  pl.BlockSpec((1, 6, d), lambda i: (_mod_group(i * tm, t_lat, seq), 0, 0)),
                  pl.BlockSpec((tm, 128), lambda i: (i, 0)),
                  pl.BlockSpec((tm, 128), lambda i: (i, 0))],
        out_specs=pl.BlockSpec((tm, d), lambda i: (i, 0)),
        compiler_params=_cp("parallel"),
        name="moe_combine",
    )(y_pairs, x, mod, g1, g2)


def _moe(h2, x, mod, router, w_in, w_out, t_lat, seq, tm):
    t, d = h2.shape
    e_idx, g1, g2 = _router(h2, router, tm)
    e_flat = e_idx[:, :2].reshape(-1)
    onehot = (e_flat[:, None] == jnp.arange(N_EXPERTS, dtype=jnp.int32)[None, :]).astype(jnp.int32)
    csum = jnp.cumsum(onehot, axis=0)
    counts = csum[-1]
    padded = (counts + MOE_TILE - 1) // MOE_TILE * MOE_TILE
    ends = jnp.cumsum(padded)
    pstarts = ends - padded
    dest = jnp.sum(onehot * (pstarts[None, :] + csum - 1), axis=1).astype(jnp.int32)
    n_tiles = (2 * t + MOE_TILE - 1) // MOE_TILE + N_EXPERTS
    n_rows = n_tiles * MOE_TILE
    slot_a = jnp.full((n_rows,), -1, jnp.int32).at[dest].set(jnp.arange(2 * t, dtype=jnp.int32))
    is_pad = slot_a < 0
    slot_row = jnp.where(is_pad, 2 * t - 1 + jnp.cumsum(is_pad.astype(jnp.int32)), slot_a)
    src_tok = jnp.where(is_pad, 0, slot_a // 2)
    src_tok = jnp.concatenate([src_tok, jnp.zeros((MOE_TILE,), jnp.int32)]).reshape(n_tiles + 1, 1, MOE_TILE)
    tile_start = jnp.arange(n_tiles + 1, dtype=jnp.int32) * MOE_TILE
    tile_e = jnp.minimum(jnp.searchsorted(ends, tile_start, side='right'), N_EXPERTS - 1).astype(jnp.int32)

    y = _experts(h2, src_tok, slot_row.reshape(n_tiles, 1, MOE_TILE), tile_e, w_in, w_out, n_rows)
    return _combine(y.reshape(-1, 2 * d), x, mod, g1, g2, t_lat, seq, tm)


def _final_norm_kernel(x_ref, g_ref, o_ref):
    x = x_ref[...]
    ms = jnp.mean(x * x, axis=-1, keepdims=True)
    o_ref[...] = x * lax.rsqrt(ms + EPS) * g_ref[...]


def _final_norm(x, g, tm):
    t, d = x.shape
    return pl.pallas_call(
        _final_norm_kernel,
        out_shape=jax.ShapeDtypeStruct((t, d), F32),
        grid=(t // tm,),
        in_specs=[pl.BlockSpec((tm, d), lambda i: (i, 0)), pl.BlockSpec((1, d), lambda i: (0, 0))],
        out_specs=pl.BlockSpec((tm, d), lambda i: (i, 0)),
        compiler_params=_cp("parallel"),
        name="final_norm",
    )(x, g.reshape(1, d))


def _proj_weights(w_in):
    o = {}
    acc = 0
    for name, size in (('a_z', 256), ('a_x', 256), ('a_b', 256), ('a_c', 256), ('a_dt', 8), ('b_u', 256),
                       ('c_q', 256), ('c_k', 128), ('c_v', 128), ('d_q', 256), ('d_k', 128), ('d_v', 128),
                       ('gates', 4096)):
        o[name] = (acc, size)
        acc += size
    order = ('gates', 'a_z', 'b_u', 'c_q', 'd_q', 'c_k', 'c_v', 'd_k', 'd_v', 'a_x', 'a_b', 'a_c')
    w = jnp.concatenate([w_in[:, :, o[n][0]:o[n][0] + o[n][1]] for n in order], axis=-1).astype(BF16)
    dt0 = o['a_dt'][0]
    wdt = jnp.pad(w_in[:, :, dt0:dt0 + 8], ((0, 0), (0, 0), (0, 120))).astype(BF16)
    return w, wdt


def kernel(x, c, ctx, c_ctx, norm1_g, norm2_g, ada_w, ada_b, w_in, ssd_conv_w, ssd_conv_b, ssd_a_log,
           ssd_dt_bias, ssd_d, ssd_norm_g, s5_lam_re, s5_lam_im, s5_log_step, s5_b_re, s5_b_im, s5_c_re,
           s5_c_im, s5_d, s5_glu_w, qk_norm_g, swa_sink, w_branch, w_out, ffn_w_in, ffn_w_out, moe_router,
           moe_w_in, moe_w_out, final_norm_g):
    bsz, seq, d = x.shape
    n_ctx = ctx.shape[1]
    depth = w_in.shape[0]
    t_lat, t_ctx = bsz * seq, bsz * n_ctx
    tm = _pow2_tile(1024, seq, t_ctx)
    tm_small = _pow2_tile(512, seq, t_ctx)

    cvec = jnp.zeros((16, d), F32).at[0].set(c_ctx).at[1:1 + bsz].set(c)
    mod = _adaln(cvec, ada_w, ada_b).reshape(depth, 16, 6, d)
    wp, wdt = _proj_weights(w_in)
    wbr = w_branch.astype(BF16)
    wo = w_out.astype(BF16)
    ffn_in, ffn_out = ffn_w_in.astype(BF16), ffn_w_out.astype(BF16)
    moe_in, moe_out = moe_w_in.astype(BF16), moe_w_out.astype(BF16)

    xx = jnp.concatenate([x.reshape(t_lat, d), ctx.reshape(t_ctx, d)], axis=0)
    for l in range(depth):
        with_ctx = l < depth - 1
        t_out = t_lat + (t_ctx if with_ctx else 0)
        p, pdt = _inproj(xx, norm1_g[l], mod[l], wp[l], wdt[l], t_lat, seq, tm)
        ya = _ssd(p, pdt, ssd_conv_w[l], ssd_conv_b[l], ssd_a_log[l], ssd_dt_bias[l], ssd_d[l],
                  ssd_norm_g[l], bsz, seq, n_ctx)
        yb = _s5(p, s5_lam_re[l], s5_lam_im[l], s5_log_step[l], s5_b_re[l], s5_b_im[l], s5_c_re[l],
                 s5_c_im[l], s5_d[l], s5_glu_w[l], bsz, seq, n_ctx)
        q1, q2, k1, k2 = _prep(p, qk_norm_g[l], t_lat, seq, tm)
        yc = _gattn(q1, k1, p, bsz, seq, n_ctx, with_ctx)
        yd = _wattn(q2, k2, p, swa_sink[l], bsz, seq, n_ctx, with_ctx)
        xx, h2 = _merge((ya, yb, yc, yd), p, xx, mod[l], wbr[l], wo[l], norm2_g[l], t_out, t_lat, seq,
                        tm_small)
        if l % 2 == 0:
            xx = _ffn_dense(h2, xx, mod[l], ffn_in[l // 2], ffn_out[l // 2], t_lat, seq, tm)
        else:
            xx = _moe(h2, xx, mod[l], moe_router[l // 2], moe_in[l // 2], moe_out[l // 2], t_lat, seq,
                      tm_small)
    out = _final_norm(xx[:t_lat], final_norm_g, tm)
    return out.reshape(bsz, seq, d)
```

```python
import functools

import numpy as np
import jax
import jax.numpy as jnp
from jax import lax
from jax.experimental import pallas as pl
from jax.experimental.pallas import tpu as pltpu

F32 = jnp.float32
BF16 = jnp.bfloat16

EPS = 1e-6
NEG_INF = -1e30
GRID_W = 64
MIX_W = 256
HEAD_DIM = 64
N_HEADS = 4
ATTN_SCALE = HEAD_DIM ** -0.5
SSD_STATE = 128
SSD_CONV = 5
CHUNK = 128
HALO = 16
S5_GROUPS = 16
S5_GROUP = 16
S5_STATE = 64
S5_LANES = S5_GROUPS * S5_STATE
S5_STEPS = 64
ROPE_BASE = 10000.0
ROPE_FREQS = 16
WINDOW = 128
KT_UNIT = 256
N_EXPERTS = 8
MOE_TILE = 512
MOE_STEPS = 7
MOE_WAITS = 64
TOK_ROWS = 8
VMEM_LIMIT = 56 * 1024 * 1024

C_GATES, C_Z, C_BU, C_CQ, C_DQ = 0, 4096, 4352, 4608, 4864
C_CK, C_CV, C_DK, C_DV = 5120, 5248, 5376, 5504
C_AX, C_AB, C_AC = 5632, 5888, 6144
P_COLS = 6400
PROJ_TN = 1280


def _cp(*sem):
    return pltpu.CompilerParams(dimension_semantics=sem, vmem_limit_bytes=VMEM_LIMIT)


def _pow2_tile(cap, *dims):
    t = 1
    while t * 2 <= cap and all(d % (t * 2) == 0 for d in dims):
        t *= 2
    return t


def _dot(a, b):
    return jnp.dot(a, b, preferred_element_type=F32)


def _dot_nt(a, b):
    return lax.dot_general(a, b, (((1,), (1,)), ((), ())), preferred_element_type=F32)


def _dot_tn(a, b):
    return lax.dot_general(a, b, (((0,), (0,)), ((), ())), preferred_element_type=F32)


def _split3(x):
    hi = x.astype(BF16)
    r1 = x - hi.astype(F32)
    mid = r1.astype(BF16)
    lo = (r1 - mid.astype(F32)).astype(BF16)
    return hi, mid, lo


def _silu(x):
    return x * jax.nn.sigmoid(x)


def _adaln_kernel(c_ref, w_ref, b_ref, o_ref):
    c = c_ref[...]
    o_ref[0] = jnp.dot(_silu(c), w_ref[0], preferred_element_type=F32,
                       precision=lax.Precision.HIGHEST) + b_ref[0]


def _adaln(cvec, ada_w, ada_b):
    depth, d, n = ada_w.shape
    tn = 1024
    return pl.pallas_call(
        _adaln_kernel,
        out_shape=jax.ShapeDtypeStruct((depth, 16, n), F32),
        grid=(depth, n // tn),
        in_specs=[pl.BlockSpec((16, d), lambda l, j: (0, 0)),
                  pl.BlockSpec((1, d, tn), lambda l, j: (l, 0, j)),
                  pl.BlockSpec((1, 1, tn), lambda l, j: (l, 0, j))],
        out_specs=pl.BlockSpec((1, 16, tn), lambda l, j: (l, 0, j)),
        compiler_params=_cp("parallel", "parallel"),
        name="adaln",
    )(cvec, ada_w, ada_b.reshape(depth, 1, n))


def _mod_group(row0, t_lat, seq):
    return jnp.where(row0 >= t_lat, 0, 1 + row0 // seq)


def _inproj_kernel(x_ref, g_ref, mod_ref, w_ref, wdt_ref, o_ref, odt_ref, h_ref):
    @pl.when(pl.program_id(1) == 0)
    def _():
        x = x_ref[...]
        ms = jnp.mean(x * x, axis=-1, keepdims=True)
        y = x * lax.rsqrt(ms + EPS) * g_ref[...]
        h = (y * (1.0 + mod_ref[0, 1:2, :]) + mod_ref[0, 0:1, :]).astype(BF16)
        h_ref[...] = h
        odt_ref[...] = _dot(h, wdt_ref[...])

    o_ref[...] = _dot(h_ref[...], w_ref[...]).astype(BF16)


def _inproj(x, g, mod, w, wdt, t_lat, seq, tm):
    t, d = x.shape
    n = w.shape[1]
    tn = PROJ_TN
    return pl.pallas_call(
        _inproj_kernel,
        out_shape=(jax.ShapeDtypeStruct((t, n), BF16), jax.ShapeDtypeStruct((t, 128), F32)),
        grid=(t // tm, n // tn),
        in_specs=[pl.BlockSpec((tm, d), lambda i, j: (i, 0)),
                  pl.BlockSpec((1, d), lambda i, j: (0, 0)),
                  pl.BlockSpec((1, 6, d), lambda i, j: (_mod_group(i * tm, t_lat, seq), 0, 0)),
                  pl.BlockSpec((d, tn), lambda i, j: (0, j)),
                  pl.BlockSpec((d, 128), lambda i, j: (0, 0))],
        out_specs=(pl.BlockSpec((tm, tn), lambda i, j: (i, j)),
                   pl.BlockSpec((tm, 128), lambda i, j: (i, 0))),
        scratch_shapes=[pltpu.VMEM((tm, d), BF16)],
        compiler_params=_cp("parallel", "arbitrary"),
        name="inproj",
    )(x, g.reshape(1, d), mod, w, wdt)


def _ssd_consts():
    r = np.arange(CHUNK)
    tri_l = (r[None, :] <= r[:, None]).astype(np.float32)
    tri_u = tri_l.T.copy()
    shifts = np.zeros((4, CHUNK, CHUNK + 2 * HALO), np.float32)
    for n, k in enumerate((0, 1, 3, 4)):
        shifts[n, r, r + HALO + k - 2] = 1.0
    return (jnp.asarray(np.stack([tri_l, tri_u]), BF16), jnp.asarray(shifts, BF16))


def _ssd_kernel(zl_ref, xl_ref, bl_ref, cl_ref, dtl_ref, zc_ref, xc_ref, bc_ref, cc_ref, dtc_ref,
                cw_ref, cb_ref, an_ref, bias_ref, dsk_ref, ng_ref, tri_ref, sh_ref,
                yl_ref, yc_ref, act_l, act_c, yf_l, yf_c, stf_ref, stb_ref, *, n_lat, n_ctx):
    q = CHUNK
    lane128 = lax.broadcasted_iota(jnp.int32, (q, 128), 1)
    lane256 = lax.broadcasted_iota(jnp.int32, (q, MIX_W), 1)
    lane256r = lax.broadcasted_iota(jnp.int32, (1, MIX_W), 1)
    row_i = lax.broadcasted_iota(jnp.int32, (q, q), 0)
    col_i = lax.broadcasted_iota(jnp.int32, (q, q), 1)
    head_masks = [(lane256 >= HEAD_DIM * h) & (lane256 < HEAD_DIM * (h + 1)) for h in range(N_HEADS)]
    head_masks_r = [(lane256r >= HEAD_DIM * h) & (lane256r < HEAD_DIM * (h + 1)) for h in range(N_HEADS)]
    group_masks = [lane256 < 128, lane256 >= 128]

    def conv_act(x_ref, b_ref, c_ref, n_chunks, c):
        def rows(ref, start, size):
            return ref[pl.ds(start, size), :]

        start = c * q
        if isinstance(c, int):
            p0, n0 = max(start - HALO, 0), min(start + q, n_chunks * q - HALO)
            pf, nf = float(c > 0), float(c < n_chunks - 1)
        else:
            start = pl.multiple_of(start, q)
            p0 = pl.multiple_of(jnp.maximum(start - HALO, 0), HALO)
            n0 = pl.multiple_of(jnp.minimum(start + q, n_chunks * q - HALO), HALO)
            pf, nf = (c > 0).astype(F32), (c < n_chunks - 1).astype(F32)
        parts = []
        for ref in (x_ref, b_ref, c_ref):
            prev = (rows(ref, p0, HALO).astype(F32) * pf).astype(BF16)
            nxt = (rows(ref, n0, HALO).astype(F32) * nf).astype(BF16)
            parts.append(jnp.concatenate([prev, rows(ref, start, q), nxt], axis=0))
        ext = jnp.concatenate(parts, axis=1)
        cur = ext[HALO:HALO + q].astype(F32)
        acc = cur * cw_ref[2:3, :] + cb_ref[...]
        for n, k in enumerate((0, 1, 3, 4)):
            acc = acc + _dot(sh_ref[n], ext) * cw_ref[k:k + 1, :]
        return _silu(acc)

    def conv_chunk(seg, c):
        _, x_ref, b_ref, c_ref, _, act_ref, _, _, n_chunks = seg
        start = c * q if isinstance(c, int) else pl.multiple_of(c * q, q)
        act_ref[pl.ds(start, q), :] = conv_act(x_ref, b_ref, c_ref, n_chunks, c).astype(BF16)

    def chunk(seg, c, d, second):
        z_ref, x_ref, b_ref, c_ref, dt_ref, act_ref, yf_ref, y_ref, n_chunks = seg
        st_ref = st_refs[d]
        start = c * q if isinstance(c, int) else pl.multiple_of(c * q, q)
        act = act_ref[pl.ds(start, q), :]
        xs = act[:, 0:MIX_W].astype(F32)
        bm = act[:, MIX_W:2 * MIX_W]
        cm = act[:, 2 * MIX_W:3 * MIX_W]

        dt_n = jax.nn.softplus(dt_ref[pl.ds(start, q), :] + bias_ref[...])
        la_n = dt_n * an_ref[...]
        hi, mid, lo = _split3(la_n)
        tri = tri_ref[d]
        cs_n = _dot(tri, hi) + _dot(tri, mid) + _dot(tri, lo)
        cs_t = cs_n.T
        edge = q - 1 if d == 0 else 0
        tri_mask = (col_i <= row_i) if d == 0 else (col_i >= row_i)

        dt_full = jnp.zeros((q, MIX_W), F32)
        dte_full = jnp.zeros((q, MIX_W), F32)
        ecs_full = jnp.zeros((q, MIX_W), F32)
        tot_full = jnp.zeros((1, MIX_W), F32)
        decay = []
        for h in range(N_HEADS):
            sel = lane128 == (N_HEADS * d + h)
            cs_col = jnp.sum(jnp.where(sel, cs_n, 0.0), axis=-1, keepdims=True)
            dt_col = jnp.sum(jnp.where(sel, dt_n, 0.0), axis=-1, keepdims=True)
            tot = cs_col[edge:edge + 1, :]
            cs_row = cs_t[N_HEADS * d + h:N_HEADS * d + h + 1, :]
            decay.append(jnp.where(tri_mask, jnp.exp(cs_col - cs_row), 0.0))
            dt_full = jnp.where(head_masks[h], dt_col, dt_full)
            dte_full = jnp.where(head_masks[h], jnp.exp(tot - cs_col), dte_full)
            ecs_full = jnp.where(head_masks[h], jnp.exp(cs_col), ecs_full)
            tot_full = jnp.where(head_masks_r[h], jnp.exp(tot), tot_full)

        xdt = xs * dt_full
        state = st_ref[...]
        y = jnp.zeros((q, MIX_W), F32)
        y_off = jnp.zeros((q, MIX_W), F32)
        upd = jnp.zeros((SSD_STATE, MIX_W), F32)
        xdte = xdt * dte_full
        for g in range(2):
            bg = bm[:, 128 * g:128 * (g + 1)]
            cg = cm[:, 128 * g:128 * (g + 1)]
            cb = _dot_nt(cg, bg)
            for h in (2 * g, 2 * g + 1):
                m = (cb * decay[h]).astype(BF16)
                y = y + _dot(m, jnp.where(head_masks[h], xdt, 0.0).astype(BF16))
            y_off = y_off + _dot(cg, jnp.where(group_masks[g], state, 0.0).astype(BF16))
            upd = upd + _dot_tn(bg, jnp.where(group_masks[g], xdte, 0.0).astype(BF16))
        y = y + y_off * ecs_full
        st_ref[...] = state * tot_full + upd

        if not second:
            yf_ref[pl.ds(start, q), :] = y
        else:
            y = y + yf_ref[pl.ds(start, q), :] + dsk_ref[...] * xs
            y = y * _silu(z_ref[pl.ds(start, q), :].astype(F32))
            ms = jnp.mean(y * y, axis=-1, keepdims=True)
            y_ref[pl.ds(start, q), :] = (y * lax.rsqrt(ms + EPS) * ng_ref[...]).astype(BF16)

    seg_l = (zl_ref, xl_ref, bl_ref, cl_ref, dtl_ref, act_l, yf_l, yl_ref, n_lat)
    seg_c = (zc_ref, xc_ref, bc_ref, cc_ref, dtc_ref, act_c, yf_c, yc_ref, n_ctx)
    st_refs = (stf_ref, stb_ref)
    stf_ref[...] = jnp.zeros_like(stf_ref)
    stb_ref[...] = jnp.zeros_like(stb_ref)

    for c in range(n_ctx):
        conv_chunk(seg_c, c)

    def conv_body(i, carry):
        conv_chunk(seg_l, i)
        return carry

    lax.fori_loop(0, n_lat, conv_body, 0)

    for i in range(n_ctx):
        chunk(seg_c, i, 0, 2 * i >= n_ctx)
        chunk(seg_c, n_ctx - 1 - i, 1, 2 * i >= n_ctx)

    def pair(second):
        def body(i, carry):
            chunk(seg_l, i, 0, second)
            chunk(seg_l, n_lat - 1 - i, 1, second)
            return carry
        return body

    lax.fori_loop(0, n_lat // 2, pair(False), 0)
    lax.fori_loop(n_lat // 2, n_lat, pair(True), 0)


def _ssd(p, pdt, conv_w, conv_b, a_log, dt_bias, d_skip, norm_g, bsz, seq, ctx):
    t = p.shape[0]
    n_lat, n_ctx = seq // CHUNK, ctx // CHUNK
    assert n_lat % 2 == 0 and n_ctx % 2 == 0
    cb0 = (bsz * seq) // ctx
    tri, shifts = _ssd_consts()
    cw = jnp.zeros((8, 3 * MIX_W), F32).at[:SSD_CONV].set(conv_w)
    a_n = jnp.zeros((1, 128), F32).at[0, :8].set(-jnp.exp(a_log.astype(F32)).reshape(8))
    bias_n = jnp.zeros((1, 128), F32).at[0, :8].set(dt_bias.astype(F32).reshape(8))
    dsk = jnp.repeat(d_skip.astype(F32), HEAD_DIM).reshape(1, MIX_W)

    def lat(col):
        return pl.BlockSpec((seq, MIX_W), lambda b, col=col: (b, col // MIX_W))

    def cx(col):
        return pl.BlockSpec((ctx, MIX_W), lambda b, col=col: (cb0 + b, col // MIX_W))

    def full(shape):
        return pl.BlockSpec(shape, lambda b: (0,) * len(shape))

    kern = functools.partial(_ssd_kernel, n_lat=n_lat, n_ctx=n_ctx)
    yl, yc = pl.pallas_call(
        kern,
        out_shape=(jax.ShapeDtypeStruct((bsz * seq, MIX_W), BF16),
                   jax.ShapeDtypeStruct((bsz * ctx, MIX_W), BF16)),
        grid=(bsz,),
        in_specs=[lat(C_Z), lat(C_AX), lat(C_AB), lat(C_AC),
                  pl.BlockSpec((seq, 128), lambda b: (b, 0)),
                  cx(C_Z), cx(C_AX), cx(C_AB), cx(C_AC),
                  pl.BlockSpec((ctx, 128), lambda b: (cb0 + b, 0)),
                  full((8, 3 * MIX_W)), full((1, 3 * MIX_W)), full((1, 128)), full((1, 128)),
                  full((1, MIX_W)), full((1, MIX_W)), full((2, CHUNK, CHUNK)),
                  full((4, CHUNK, CHUNK + 2 * HALO))],
        out_specs=(pl.BlockSpec((seq, MIX_W), lambda b: (b, 0)),
                   pl.BlockSpec((ctx, MIX_W), lambda b: (b, 0))),
        scratch_shapes=[pltpu.VMEM((seq, 3 * MIX_W), BF16), pltpu.VMEM((ctx, 3 * MIX_W), BF16),
                        pltpu.VMEM((seq, MIX_W), F32), pltpu.VMEM((ctx, MIX_W), F32),
                        pltpu.VMEM((SSD_STATE, MIX_W), F32), pltpu.VMEM((SSD_STATE, MIX_W), F32)],
        compiler_params=_cp("parallel"),
        name="ssd",
    )(p, p, p, p, pdt, p, p, p, p, pdt, cw, conv_b.reshape(1, -1).astype(F32), a_n, bias_n,
      dsk, norm_g.reshape(1, MIX_W).astype(F32), tri, shifts)
    return yl, yc


def _s5_discretize(lam_re, lam_im, log_step, b_re, b_im):
    step = jnp.exp(log_step.astype(F32))[:, None]
    lr = jnp.minimum(lam_re.astype(F32), -1e-4)
    li = lam_im.astype(F32)
    mag = jnp.exp(lr * step)
    ang = li * step
    ab_re, ab_im = mag * jnp.cos(ang), mag * jnp.sin(ang)
    den = lr * lr + li * li
    f_re = ((ab_re - 1.0) * lr + ab_im * li) / den
    f_im = (ab_im * lr - (ab_re - 1.0) * li) / den
    br, bi = b_re.astype(F32), b_im.astype(F32)
    bb_re = f_re[..., None] * br - f_im[..., None] * bi
    bb_im = f_re[..., None] * bi + f_im[..., None] * br
    return ab_re, ab_im, bb_re, bb_im


def _s5_mats(lam_re, lam_im, log_step, b_re, b_im, c_re, c_im):
    eye = jnp.eye(S5_GROUPS, dtype=F32)
    a_all, b_all, c_all = [], [], []
    for d in range(2):
        ab_re, ab_im, bb_re, bb_im = _s5_discretize(lam_re[d], lam_im[d], log_step[d], b_re[d], b_im[d])
        bm = [jnp.einsum('gnp,gh->gphn', m, eye).reshape(MIX_W, S5_LANES) for m in (bb_re, bb_im)]
        cm = [jnp.einsum('gpn,gh->gnhp', m.astype(F32), eye).reshape(S5_LANES, MIX_W)
              for m in (c_re[d], c_im[d])]
        b_all.append(jnp.concatenate(bm, axis=1))
        c_all.append(jnp.concatenate([cm[0], -cm[1]], axis=0))
        a_all.append(jnp.concatenate([ab_re.reshape(1, S5_LANES), ab_im.reshape(1, S5_LANES)], axis=1))
    a = jnp.broadcast_to(jnp.stack(a_all), (2, 8, 2 * S5_LANES))
    return a, jnp.stack(b_all).astype(BF16), jnp.stack(c_all).astype(BF16)


def _s5_kernel(uf_ref, ub_ref, a_ref, b_ref, c_ref, yf_ref, yb_ref, buf_ref, s_ref):
    n = S5_LANES

    @pl.when(pl.program_id(0) == 0)
    def _():
        s_ref[...] = jnp.zeros_like(s_ref)

    for d, u_ref in enumerate((uf_ref, ub_ref)):
        buf_ref[d] = _dot(u_ref[...], b_ref[d])
    state = [(s_ref[d, :, 0:n], s_ref[d, :, n:2 * n]) for d in range(2)]
    for j in range(S5_STEPS):
        for d in range(2):
            jj = j if d == 0 else S5_STEPS - 1 - j
            rows = slice(jj * 8, jj * 8 + 8)
            a_re, a_im = a_ref[d, :, 0:n], a_ref[d, :, n:2 * n]
            s_re, s_im = state[d]
            n_re = a_re * s_re - a_im * s_im + buf_ref[d, rows, 0:n]
            n_im = a_re * s_im + a_im * s_re + buf_ref[d, rows, n:2 * n]
            buf_ref[d, rows, 0:n] = n_re
            buf_ref[d, rows, n:2 * n] = n_im
            state[d] = (n_re, n_im)
    for d, y_ref in enumerate((yf_ref, yb_ref)):
        s_ref[d, :, 0:n] = state[d][0]
        s_ref[d, :, n:2 * n] = state[d][1]
        y_ref[...] = _dot(buf_ref[d].astype(BF16), c_ref[d])


def _s5_finish_kernel(yf_ref, yb_ref, u_ref, d_ref, w_ref, o_ref):
    y = yf_ref[...] + yb_ref[...] + d_ref[...] * u_ref[...].astype(F32)
    v = jax.nn.gelu(y, approximate=True).astype(BF16)
    r = _dot(v, w_ref[...])
    o_ref[...] = (r[:, 0:MIX_W] * jax.nn.sigmoid(r[:, MIX_W:2 * MIX_W])).astype(BF16)


def _s5(p, lam_re, lam_im, log_step, b_re, b_im, c_re, c_im, d_skip, glu_w, bsz, seq, ctx):
    assert bsz <= 8
    t_lat = bsz * seq
    u = p[:, C_BU:C_BU + MIX_W]
    u_l = jnp.transpose(u[:t_lat].reshape(bsz, seq, MIX_W), (1, 0, 2))
    u_c = jnp.transpose(u[t_lat:].reshape(bsz, ctx, MIX_W), (1, 0, 2))
    u_tm = jnp.concatenate([u_c, u_l], axis=0)
    if bsz < 8:
        u_tm = jnp.pad(u_tm, ((0, 0), (0, 8 - bsz), (0, 0)))
    steps = seq + ctx
    u_tm = u_tm.reshape(steps * 8, MIX_W)
    a, bmat, cmat = _s5_mats(lam_re, lam_im, log_step, b_re, b_im, c_re, c_im)
    rows = S5_STEPS * 8
    nc, ncc = steps // S5_STEPS, ctx // S5_STEPS

    def bwd(i):
        return jnp.where(i < ncc, ncc - 1 - i, nc + ncc - 1 - i)

    def whole(shape):
        return pl.BlockSpec(shape, lambda i: (0,) * len(shape))

    yshape = jax.ShapeDtypeStruct((steps * 8, MIX_W), F32)
    yf, yb = pl.pallas_call(
        _s5_kernel,
        out_shape=(yshape, yshape),
        grid=(nc,),
        in_specs=[pl.BlockSpec((rows, MIX_W), lambda i: (i, 0)),
                  pl.BlockSpec((rows, MIX_W), lambda i: (bwd(i), 0)),
                  whole((2, 8, 2 * S5_LANES)), whole((2, MIX_W, 2 * S5_LANES)),
                  whole((2, 2 * S5_LANES, MIX_W))],
        out_specs=(pl.BlockSpec((rows, MIX_W), lambda i: (i, 0)),
                   pl.BlockSpec((rows, MIX_W), lambda i: (bwd(i), 0))),
        scratch_shapes=[pltpu.VMEM((2, rows, 2 * S5_LANES), F32), pltpu.VMEM((2, 8, 2 * S5_LANES), F32)],
        compiler_params=_cp("arbitrary"),
        name="s5_scan",
    )(u_tm, u_tm, a, bmat, cmat)

    tmf = _pow2_tile(2048, steps * 8)
    o = pl.pallas_call(
        _s5_finish_kernel,
        out_shape=jax.ShapeDtypeStruct((steps * 8, MIX_W), BF16),
        grid=(steps * 8 // tmf,),
        in_specs=[pl.BlockSpec((tmf, MIX_W), lambda i: (i, 0)),
                  pl.BlockSpec((tmf, MIX_W), lambda i: (i, 0)),
                  pl.BlockSpec((tmf, MIX_W), lambda i: (i, 0)),
                  pl.BlockSpec((1, MIX_W), lambda i: (0, 0)),
                  pl.BlockSpec((MIX_W, 2 * MIX_W), lambda i: (0, 0))],
        out_specs=pl.BlockSpec((tmf, MIX_W), lambda i: (i, 0)),
        compiler_params=_cp("parallel"),
        name="s5_finish",
    )(yf, yb, u_tm, d_skip.reshape(1, MIX_W).astype(F32), glu_w.astype(BF16))
    o = o.reshape(steps, 8, MIX_W)[:, :bsz]
    o_c = jnp.transpose(o[:ctx], (1, 0, 2)).reshape(bsz * ctx, MIX_W)
    o_l = jnp.transpose(o[ctx:], (1, 0, 2)).reshape(t_lat, MIX_W)
    return o_l, o_c


def _rope_tables(seq, tm):
    rows = seq // GRID_W
    pos_r = jnp.repeat(jnp.arange(rows, dtype=F32), GRID_W)
    pos_c = jnp.tile(jnp.arange(GRID_W, dtype=F32), rows)
    inv = ROPE_BASE ** (-jnp.arange(ROPE_FREQS, dtype=F32) / ROPE_FREQS)
    ar, ac = pos_r[:, None] * inv, pos_c[:, None] * inv
    cos = jnp.concatenate([jnp.cos(ar), jnp.cos(ar), jnp.cos(ac), jnp.cos(ac)], axis=-1)
    sin = jnp.concatenate([-jnp.sin(ar), jnp.sin(ar), -jnp.sin(ac), jnp.sin(ac)], axis=-1)
    cos = jnp.concatenate([jnp.tile(cos, (1, 2)), jnp.ones((tm, 128), F32)], axis=0)
    sin = jnp.concatenate([jnp.tile(sin, (1, 2)), jnp.zeros((tm, 128), F32)], axis=0)
    return cos, sin


def _prep_consts():
    i = np.arange(MIX_W)
    bd = ((i[:, None] // HEAD_DIM) == (i[None, :] // HEAD_DIM)).astype(np.float32) / HEAD_DIM
    pm = (i[:, None] == (i[None, :] ^ ROPE_FREQS)).astype(np.float32)
    return jnp.asarray(bd, BF16), jnp.asarray(pm, BF16)


def _prep_kernel(cq_ref, dq_ref, ck_ref, dk_ref, cos_ref, sin_ref, qg_ref, kg_ref, bd_ref, pm_ref,
                 q1_ref, q2_ref, k1_ref, k2_ref):
    cos, sin = cos_ref[...], sin_ref[...]
    cos2 = jnp.concatenate([cos, cos], axis=-1)
    sin2 = jnp.concatenate([sin, sin], axis=-1)
    bd, pm = bd_ref[...], pm_ref[...]
    tm = cos.shape[0]
    lane = lax.broadcasted_iota(jnp.int32, (tm, 128), 1)

    def rms(x, g, n):
        ms = _dot((x * x).astype(BF16), bd[:n, :n])
        return x * lax.rsqrt(ms + EPS) * g

    def rope(y, c, s, n):
        return y * c + _dot(y.astype(BF16), pm[:n, :n]) * s

    def store_q(q, ref):
        q = q * ATTN_SCALE
        for kv in range(2):
            for g in range(2):
                half = q[:, 128 * kv:128 * (kv + 1)]
                if g != kv:
                    half = pltpu.roll(half, HEAD_DIM, 1)
                keep = (lane >= HEAD_DIM * kv) & (lane < HEAD_DIM * (kv + 1))
                ref[2 * kv + g] = jnp.where(keep, half, 0.0).astype(BF16)

    store_q(rope(rms(cq_ref[...].astype(F32), qg_ref[...], MIX_W), cos2, sin2, MIX_W), q1_ref)
    store_q(rope(dq_ref[...].astype(F32), cos2, sin2, MIX_W), q2_ref)
    k1 = rope(rms(ck_ref[...].astype(F32), kg_ref[...], 128), cos, sin, 128)
    for u in range(tm // KT_UNIT):
        k1_ref[u] = k1[u * KT_UNIT:(u + 1) * KT_UNIT].T.astype(BF16)
    k2 = rope(dk_ref[...].astype(F32), cos, sin, 128)
    for u in range(tm // WINDOW):
        k2_ref[u] = k2[u * WINDOW:(u + 1) * WINDOW].T.astype(BF16)


def _prep(p, qk_gain, t_lat, seq, tm):
    t = p.shape[0]
    cos, sin = _rope_tables(seq, tm)
    bd, pm = _prep_consts()
    qg = jnp.tile(qk_gain[0].astype(F32), N_HEADS).reshape(1, MIX_W)
    kg = jnp.tile(qk_gain[1].astype(F32), 2).reshape(1, 128)
    nt = seq // tm

    def tab(i):
        return (jnp.where(i * tm >= t_lat, nt, i % nt), 0)

    def const(shape):
        return pl.BlockSpec(shape, lambda i: (0,) * len(shape))

    qshape = jax.ShapeDtypeStruct((N_HEADS, t, 128), BF16)
    ktshape = jax.ShapeDtypeStruct((t // KT_UNIT, 128, KT_UNIT), BF16)
    kt2shape = jax.ShapeDtypeStruct((t // WINDOW, 128, WINDOW), BF16)
    return pl.pallas_call(
        _prep_kernel,
        out_shape=(qshape, qshape, ktshape, kt2shape),
        grid=(t // tm,),
        in_specs=[pl.BlockSpec((tm, MIX_W), lambda i: (i, C_CQ // MIX_W)),
                  pl.BlockSpec((tm, MIX_W), lambda i: (i, C_DQ // MIX_W)),
                  pl.BlockSpec((tm, 128), lambda i: (i, C_CK // 128)),
                  pl.BlockSpec((tm, 128), lambda i: (i, C_DK // 128)),
                  pl.BlockSpec((tm, 128), tab), pl.BlockSpec((tm, 128), tab),
                  const((1, MIX_W)), const((1, 128)), const((MIX_W, MIX_W)), const((MIX_W, MIX_W))],
        out_specs=(pl.BlockSpec((N_HEADS, tm, 128), lambda i: (0, i, 0)),
                   pl.BlockSpec((N_HEADS, tm, 128), lambda i: (0, i, 0)),
                   pl.BlockSpec((tm // KT_UNIT, 128, KT_UNIT), lambda i: (i, 0, 0)),
                   pl.BlockSpec((tm // WINDOW, 128, WINDOW), lambda i: (i, 0, 0))),
        compiler_params=_cp("parallel"),
        name="qk_prep",
    )(p, p, p, p, cos, sin, qg, kg, bd, pm)


def _pack_heads(o, tq):
    lane = lax.broadcasted_iota(jnp.int32, (tq, 128), 1)
    left = lane < HEAD_DIM
    o00, o01, o10, o11 = [o[h * tq:(h + 1) * tq] for h in range(N_HEADS)]
    out0 = jnp.where(left, o00, pltpu.roll(o01, HEAD_DIM, 1))
    out1 = jnp.where(left, pltpu.roll(o10, HEAD_DIM, 1), o11)
    return jnp.concatenate([out0, out1], axis=-1)


def _gattn_kernel(q_ref, kl_ref, vl_ref, kc_ref, vc_ref, o_ref, m_ref, l_ref, acc_ref,
                  *, tq, units, n_qb_lat, n_kvb):
    qb = pl.program_id(1)

    def tree(op, xs):
        while len(xs) > 1:
            xs = [op(xs[i], xs[i + 1]) for i in range(0, len(xs) - 1, 2)] + ([xs[-1]] if len(xs) % 2 else [])
        return xs[0]

    def scores(h, kts):
        q = q_ref[h]
        cols = []
        for kt in kts:
            s = _dot(q, kt)
            cols += [s[:, 128 * c:128 * (c + 1)] for c in range(s.shape[1] // 128)]
        return cols

    def softmax(h, cols, first):
        m_blk = jnp.max(tree(jnp.maximum, cols), axis=-1, keepdims=True)
        if first:
            m_new = jnp.broadcast_to(m_blk, (tq, 128))
            alpha = None
            ps = [jnp.exp(c - m_new) for c in cols]
            l_ref[h] = tree(jnp.add, ps)
        else:
            m_old = m_ref[h]
            m_new = jnp.maximum(m_old, m_blk)
            alpha = jnp.exp(m_old - m_new)
            ps = [jnp.exp(c - m_new) for c in cols]
            l_ref[h] = alpha * l_ref[h] + tree(jnp.add, ps)
        m_ref[h] = m_new
        return alpha, jnp.concatenate(ps, axis=-1).astype(BF16)

    def weighted(h, alpha, p, v):
        pv = _dot(p, v)
        acc_ref[h] = pv if alpha is None else alpha * acc_ref[h] + pv

    def block(kts, v, first):
        cols = scores(0, kts)
        for h in range(N_HEADS):
            nxt = scores(h + 1, kts) if h + 1 < N_HEADS else None
            alpha, p = softmax(h, cols, first)
            weighted(h, alpha, p, v)
            cols = nxt

    block([kc_ref[u] for u in range(kc_ref.shape[0])], vc_ref[...], True)

    def body(j, carry):
        rows = pl.ds(pl.multiple_of(j * (units * KT_UNIT), units * KT_UNIT), units * KT_UNIT)
        block([kl_ref[j * units + u] for u in range(units)], vl_ref[rows, :], False)
        return carry

    lax.fori_loop(0, jnp.where(qb < n_qb_lat, n_kvb, 0), body, 0)
    o = [acc_ref[h] / jnp.sum(l_ref[h], axis=-1, keepdims=True) for h in range(N_HEADS)]
    o_ref[...] = _pack_heads(jnp.concatenate(o, axis=0), tq).astype(BF16)


def _gattn(qp, kt, p, bsz, seq, ctx, with_ctx):
    t_lat = bsz * seq
    tq = ctx
    units = 4
    n_qb_lat = seq // tq
    n_qb = n_qb_lat + (1 if with_ctx else 0)
    cb0 = t_lat // ctx
    t_out = t_lat + (bsz * ctx if with_ctx else 0)
    assert ctx % KT_UNIT == 0 and seq % (units * KT_UNIT) == 0

    def qrow(b, i):
        return jnp.where(i < n_qb_lat, b * n_qb_lat + i, cb0 + b)

    kern = functools.partial(_gattn_kernel, tq=tq, units=units, n_qb_lat=n_qb_lat,
                             n_kvb=seq // (units * KT_UNIT))
    return pl.pallas_call(
        kern,
        out_shape=jax.ShapeDtypeStruct((t_out, MIX_W), BF16),
        grid=(bsz, n_qb),
        in_specs=[pl.BlockSpec((N_HEADS, tq, 128), lambda b, i: (0, qrow(b, i), 0)),
                  pl.BlockSpec((seq // KT_UNIT, 128, KT_UNIT), lambda b, i: (b, 0, 0)),
                  pl.BlockSpec((seq, 128), lambda b, i: (b, C_CV // 128)),
                  pl.BlockSpec((ctx // KT_UNIT, 128, KT_UNIT), lambda b, i: (cb0 + b, 0, 0)),
                  pl.BlockSpec((ctx, 128), lambda b, i: (cb0 + b, C_CV // 128))],
        out_specs=pl.BlockSpec((tq, MIX_W), lambda b, i: (qrow(b, i), 0)),
        scratch_shapes=[pltpu.VMEM((N_HEADS, tq, 128), F32)] * 3,
        compiler_params=_cp("parallel", "arbitrary"),
        name="global_attn",
    )(qp, kt, p, kt, p)


def _wattn_kernel(q_ref, kl_ref, vl_ref, kc_ref, vc_ref, sink_ref, o_ref, *, nb):
    w = WINDOW
    n = pl.program_id(1)
    is_lat = n < nb
    rows = 2 * w
    qi = lax.broadcasted_iota(jnp.int32, (rows, w), 0) & (w - 1)
    kj = lax.broadcasted_iota(jnp.int32, (rows, w), 1)
    band = (jnp.clip(n - 1, 0, nb - 1), jnp.clip(n, 0, nb - 1), jnp.clip(n + 1, 0, nb - 1))
    off_prev = jnp.where(is_lat & (n >= 1), 0, w)
    off_cur = jnp.where(is_lat, 0, w)
    off_next = jnp.where(n + 1 < nb, 0, w)
    masks = [kj >= qi + off_prev, kj >= off_cur, kj <= qi - off_next]
    kts = [kc_ref[u] for u in range(kc_ref.shape[0])] + [kl_ref[i] for i in band]
    n_ctx_tiles = kc_ref.shape[0]
    v_all = jnp.concatenate([vc_ref[...]] + [vl_ref[pl.ds(pl.multiple_of(i * w, w), w), :] for i in band],
                            axis=0)

    def tree(op, xs):
        while len(xs) > 1:
            xs = [op(xs[i], xs[i + 1]) for i in range(0, len(xs) - 1, 2)] + ([xs[-1]] if len(xs) % 2 else [])
        return xs[0]

    def scores(c):
        q = q_ref[2 * c:2 * c + 2].reshape(rows, 128)
        tiles = [_dot(q, kt) for kt in kts]
        return tiles[:n_ctx_tiles] + [jnp.where(mk, t, NEG_INF) for mk, t in zip(masks, tiles[n_ctx_tiles:])]

    def finish(c, tiles):
        sink = sink_ref[c * rows:(c + 1) * rows, :]
        m = jnp.maximum(jnp.max(tree(jnp.maximum, tiles), axis=-1, keepdims=True), sink)
        ps = [jnp.exp(t - m) for t in tiles]
        den = jnp.sum(tree(jnp.add, ps), axis=-1, keepdims=True) + jnp.exp(sink - m)
        return _dot(jnp.concatenate(ps, axis=-1).astype(BF16), v_all) / den

    tiles = scores(0)
    nxt = scores(1)
    o = [finish(0, tiles), finish(1, nxt)]
    o_ref[...] = _pack_heads(jnp.concatenate(o, axis=0), w).astype(BF16)


def _wattn(qp, k, p, sink, bsz, seq, ctx, with_ctx):
    t_lat = bsz * seq
    w = WINDOW
    nb = seq // w
    ncb = ctx // w
    n_qb = nb + (ncb if with_ctx else 0)
    cq0 = t_lat // w
    cb0 = t_lat // ctx
    t_out = t_lat + (bsz * ctx if with_ctx else 0)
    sink_rows = jnp.broadcast_to(jnp.repeat(sink.astype(F32), w)[:, None], (N_HEADS * w, 128))

    def qrow(b, i):
        return jnp.where(i < nb, b * nb + i, cq0 + b * ncb + (i - nb))

    return pl.pallas_call(
        functools.partial(_wattn_kernel, nb=nb),
        out_shape=jax.ShapeDtypeStruct((t_out, MIX_W), BF16),
        grid=(bsz, n_qb),
        in_specs=[pl.BlockSpec((N_HEADS, w, 128), lambda b, i: (0, qrow(b, i), 0)),
                  pl.BlockSpec((seq // w, 128, w), lambda b, i: (b, 0, 0)),
                  pl.BlockSpec((seq, 128), lambda b, i: (b, C_DV // 128)),
                  pl.BlockSpec((ctx // w, 128, w), lambda b, i: (cb0 + b, 0, 0)),
                  pl.BlockSpec((ctx, 128), lambda b, i: (cb0 + b, C_DV // 128)),
                  pl.BlockSpec((N_HEADS * w, 128), lambda b, i: (0, 0))],
        out_specs=pl.BlockSpec((w, MIX_W), lambda b, i: (qrow(b, i), 0)),
        compiler_params=_cp("parallel", "arbitrary"),
        name="window_attn",
    )(qp, k, p, k, p, sink_rows)


def _merge_kernel(yal_ref, yac_ref, ybl_ref, ybc_ref, yc_ref, yd_ref, gate_ref, x_ref, mod_ref, wbr_ref,
                  wout_ref, g2_ref, xo_ref, h2_ref, *maybe_tok_ref, n_lat_tiles):
    tm, d = x_ref.shape
    ctx_rows = jnp.full((tm, MIX_W), pl.program_id(0), jnp.int32) >= n_lat_tiles
    ya = jnp.where(ctx_rows, yac_ref[...], yal_ref[...])
    yb = jnp.where(ctx_rows, ybc_ref[...], ybl_ref[...])
    acc = None
    for n, y in enumerate((ya, yb, yc_ref[...], yd_ref[...])):
        gate = jax.nn.sigmoid(gate_ref[:, n * d:(n + 1) * d].astype(F32))
        term = gate * _dot(y, wbr_ref[n])
        acc = term if acc is None else acc + term
    x = x_ref[...] + mod_ref[0, 2:3, :] * _dot(acc.astype(BF16), wout_ref[...])
    xo_ref[...] = x
    ms = jnp.mean(x * x, axis=-1, keepdims=True)
    y = x * lax.rsqrt(ms + EPS) * g2_ref[...]
    h2 = y * (1.0 + mod_ref[0, 4:5, :]) + mod_ref[0, 3:4, :]
    h2_ref[...] = h2
    if maybe_tok_ref:
        _to_token_tiles(maybe_tok_ref[0], h2)


def _to_token_tiles(ref, x):
    rows, d = x.shape
    k = d // 128
    for c in range(k):
        ref[pl.ds(c, rows, stride=k), :] = x[:, 128 * c:128 * (c + 1)]


def _from_token_tiles(ref, rows, d):
    k = d // 128
    return jnp.concatenate([ref[pl.ds(c, rows, stride=k), :] for c in range(k)], axis=-1)


def _merge(ys, p, x, mod, wbr, wout, g2, t_out, t_lat, seq, tm, token_tiles):
    d = x.shape[1]
    k = d // 128

    def row(width):
        return pl.BlockSpec((tm, width), lambda i: (i, 0))

    def const(shape):
        return pl.BlockSpec(shape, lambda i: (0,) * len(shape))

    out_shape = [jax.ShapeDtypeStruct((t_out, d), F32), jax.ShapeDtypeStruct((t_out, d), F32)]
    out_specs = [row(d), row(d)]
    if token_tiles:
        out_shape.append(jax.ShapeDtypeStruct((t_out * k, 128), F32))
        out_specs.append(pl.BlockSpec((tm * k, 128), lambda i: (i, 0)))
    (ya_l, ya_c), (yb_l, yb_c), yc, yd = ys
    n_lat_tiles = t_lat // tm
    n_ctx_tiles = ya_c.shape[0] // tm
    lat = pl.BlockSpec((tm, MIX_W), lambda i: (jnp.minimum(i, n_lat_tiles - 1), 0))
    cx = pl.BlockSpec((tm, MIX_W), lambda i: (jnp.clip(i - n_lat_tiles, 0, n_ctx_tiles - 1), 0))
    return pl.pallas_call(
        functools.partial(_merge_kernel, n_lat_tiles=n_lat_tiles),
        out_shape=tuple(out_shape),
        grid=(t_out // tm,),
        in_specs=[lat, cx, lat, cx, row(MIX_W), row(MIX_W), row(4 * d), row(d),
                  pl.BlockSpec((1, 6, d), lambda i: (_mod_group(i * tm, t_lat, seq), 0, 0)),
                  const((4, MIX_W, d)), const((d, d)), const((1, d))],
        out_specs=tuple(out_specs),
        compiler_params=_cp("parallel"),
        name="merge",
    )(ya_l, ya_c, yb_l, yb_c, yc, yd, p, x, mod, wbr, wout, g2.reshape(1, d))


def _ffn_dense_kernel(h_ref, x_ref, mod_ref, wg_ref, wu_ref, wo_ref, o_ref, hb_ref, acc_ref):
    j = pl.program_id(1)

    @pl.when(j == 0)
    def _():
        hb_ref[...] = h_ref[...].astype(BF16)
        acc_ref[...] = jnp.zeros_like(acc_ref)

    h = hb_ref[...]
    a = _silu(_dot(h, wg_ref[...])) * _dot(h, wu_ref[...])
    acc_ref[...] += _dot(a.astype(BF16), wo_ref[...])

    @pl.when(j == pl.num_programs(1) - 1)
    def _():
        o_ref[...] = x_ref[...] + mod_ref[0, 5:6, :] * acc_ref[...]


def _ffn_dense(h2, x, mod, w_in, w_out, t_lat, seq, tm):
    t, d = x.shape
    f = w_out.shape[0]
    tf = 256
    nf = f // tf
    return pl.pallas_call(
        _ffn_dense_kernel,
        out_shape=jax.ShapeDtypeStruct((t, d), F32),
        grid=(t // tm, nf),
        in_specs=[pl.BlockSpec((tm, d), lambda i, j: (i, 0)),
                  pl.BlockSpec((tm, d), lambda i, j: (i, 0)),
                  pl.BlockSpec((1, 6, d), lambda i, j: (_mod_group(i * tm, t_lat, seq), 0, 0)),
                  pl.BlockSpec((d, tf), lambda i, j: (0, j)),
                  pl.BlockSpec((d, tf), lambda i, j: (0, nf + j)),
                  pl.BlockSpec((tf, d), lambda i, j: (j, 0))],
        out_specs=pl.BlockSpec((tm, d), lambda i, j: (i, 0)),
        scratch_shapes=[pltpu.VMEM((tm, d), BF16), pltpu.VMEM((tm, d), F32)],
        compiler_params=_cp("parallel", "arbitrary"),
        name="ffn_dense",
    )(h2, x, mod, w_in, w_in, w_out)


def _router_kernel(h_ref, w_ref, e_ref, g1_ref, g2_ref):
    h = h_ref[...]
    h_hi = h.astype(BF16)
    h_lo = (h - h_hi.astype(F32)).astype(BF16)
    logits = _dot(h_hi, w_ref[0]) + _dot(h_lo, w_ref[0]) + _dot(h_hi, w_ref[1])
    lane = lax.broadcasted_iota(jnp.int32, logits.shape, 1)
    lane_f = lane.astype(F32)
    logits = jnp.where(lane < N_EXPERTS, logits, -jnp.inf)
    m1 = jnp.max(logits, axis=-1, keepdims=True)
    i1 = jnp.min(jnp.where(logits == m1, lane_f, 128.0), axis=-1, keepdims=True)
    rest = jnp.where(lane_f == i1, -jnp.inf, logits)
    m2 = jnp.max(rest, axis=-1, keepdims=True)
    i2 = jnp.min(jnp.where(rest == m2, lane_f, 128.0), axis=-1, keepdims=True)
    e2 = jnp.exp(m2 - m1)
    g1 = 1.0 / (1.0 + e2)
    e_ref[...] = jnp.where(lane == 0, i1, jnp.where(lane == 1, i2, 0.0)).astype(jnp.int32)
    g1_ref[...] = jnp.broadcast_to(g1, g1_ref.shape)
    g2_ref[...] = jnp.broadcast_to(e2 * g1, g2_ref.shape)


def _router(h2, router, tm):
    t, d = h2.shape
    r = jnp.zeros((d, 128), F32).at[:, :N_EXPERTS].set(router.astype(F32))
    r_hi = r.astype(BF16)
    r_lo = (r - r_hi.astype(F32)).astype(BF16)
    shp = jax.ShapeDtypeStruct((t, 128), F32)
    return pl.pallas_call(
        _router_kernel,
        out_shape=(jax.ShapeDtypeStruct((t, 128), jnp.int32), shp, shp),
        grid=(t // tm,),
        in_specs=[pl.BlockSpec((tm, d), lambda i: (i, 0)),
                  pl.BlockSpec((2, d, 128), lambda i: (0, 0, 0))],
        out_specs=(pl.BlockSpec((tm, 128), lambda i: (i, 0)),) * 3,
        compiler_params=_cp("parallel"),
        name="router",
    )(h2, jnp.stack([r_hi, r_lo]))


def _experts_kernel(te_ref, tok0_ref, tokn_ref, dst_ref, h_ref, wg_ref, wu_ref, wo_ref, y_ref,
                    xbuf, ybuf, xb_ref, acc_ref, sem_in, sem_out, *, n_tiles, tm):
    i, j = pl.program_id(0), pl.program_id(1)
    last_j = pl.num_programs(1) - 1
    slot = i % 2
    other = 1 - slot
    per_step = tm // MOE_STEPS
    tr = TOK_ROWS

    def tok(ref, t):
        return ref.at[pl.ds(pl.multiple_of(t * tr, tr), tr)]

    def gather(idx_ref, r, s):
        return pltpu.make_async_copy(tok(h_ref, idx_ref[0, 0, r]), tok(xbuf.at[s], r), sem_in.at[s])

    def scatter(r, s):
        return pltpu.make_async_copy(tok(ybuf.at[s], r), tok(y_ref, dst_ref[0, 0, r]), sem_out.at[s])

    def wait_tile(copy_of_row0):
        def body(k, carry):
            for _ in range(MOE_WAITS):
                copy_of_row0().wait()
            return carry

        lax.fori_loop(0, tm // MOE_WAITS, body, 0)

    def wait_gather(s):
        wait_tile(lambda: pltpu.make_async_copy(tok(h_ref, 0), tok(xbuf.at[s], 0), sem_in.at[s]))

    def wait_scatter(s):
        wait_tile(lambda: pltpu.make_async_copy(tok(ybuf.at[s], 0), tok(y_ref, 0), sem_out.at[s]))

    def row_traffic(with_gather, with_scatter):
        def rows(base, count):
            for k in range(count):
                if with_gather:
                    gather(tokn_ref, base + k, other).start()
                if with_scatter:
                    scatter(base + k, other).start()

        rows(j * per_step, per_step)

        @pl.when(j == last_j)
        def _():
            rows(per_step * MOE_STEPS, tm - per_step * MOE_STEPS)

    def compute():
        x = xb_ref[...]
        a = _silu(_dot(x, wg_ref[0])) * _dot(x, wu_ref[0])
        acc_ref[...] += _dot(a.astype(BF16), wo_ref[0])

    @pl.when((i == 0) & (j == 0))
    def _():
        def body(r, carry):
            gather(tok0_ref, r, 0).start()
            return carry

        lax.fori_loop(0, tm, body, 0)

    @pl.when(j == 0)
    def _():
        wait_gather(slot)

        @pl.when(i < n_tiles)
        def _():
            xb_ref[...] = _from_token_tiles(xbuf.at[slot], tm, xb_ref.shape[1]).astype(BF16)
            acc_ref[...] = jnp.zeros_like(acc_ref)

    @pl.when(i == 0)
    def _():
        compute()
        row_traffic(True, False)

    @pl.when((i > 0) & (i < n_tiles))
    def _():
        compute()
        row_traffic(True, True)

    @pl.when(i == n_tiles)
    def _():
        row_traffic(False, True)

    @pl.when(j == last_j)
    def _():
        @pl.when((i >= 2) & (i < n_tiles))
        def _():
            wait_scatter(slot)

        @pl.when(i < n_tiles)
        def _():
            _to_token_tiles(ybuf.at[slot], acc_ref[...])

        @pl.when(i == n_tiles)
        def _():
            wait_scatter(slot)
            wait_scatter(other)


def _experts(h2_tiles, src_tok, dst_row, tile_e, w_in, w_out, n_out_rows):
    d = w_in.shape[1]
    assert d == TOK_ROWS * 128
    n_tiles = dst_row.shape[0]
    f = w_out.shape[1]
    tf = f // MOE_STEPS
    tm = MOE_TILE
    assert n_tiles >= 2 and tm % MOE_WAITS == 0 and tf % 128 == 0

    def smem(index_map):
        return pl.BlockSpec((1, 1, tm), index_map, memory_space=pltpu.SMEM)

    grid_spec = pltpu.PrefetchScalarGridSpec(
        num_scalar_prefetch=1,
        grid=(n_tiles + 1, MOE_STEPS),
        in_specs=[smem(lambda i, j, te: (0, 0, 0)),
                  smem(lambda i, j, te: (jnp.minimum(i + 1, n_tiles), 0, 0)),
                  smem(lambda i, j, te: (jnp.clip(i - 1, 0, n_tiles - 1), 0, 0)),
                  pl.BlockSpec(memory_space=pl.ANY),
                  pl.BlockSpec((1, d, tf), lambda i, j, te: (te[i], 0, j)),
                  pl.BlockSpec((1, d, tf), lambda i, j, te: (te[i], 0, MOE_STEPS + j)),
                  pl.BlockSpec((1, tf, d), lambda i, j, te: (te[i], j, 0))],
        out_specs=pl.BlockSpec(memory_space=pl.ANY),
        scratch_shapes=[pltpu.VMEM((2, tm * TOK_ROWS, 128), F32), pltpu.VMEM((2, tm * TOK_ROWS, 128), F32),
                        pltpu.VMEM((tm, d), BF16), pltpu.VMEM((tm, d), F32),
                        pltpu.SemaphoreType.DMA((2,)), pltpu.SemaphoreType.DMA((2,))])
    return pl.pallas_call(
        functools.partial(_experts_kernel, n_tiles=n_tiles, tm=tm),
        out_shape=jax.ShapeDtypeStruct((n_out_rows * TOK_ROWS, 128), F32),
        grid_spec=grid_spec,
        compiler_params=_cp("arbitrary", "arbitrary"),
        name="experts",
    )(tile_e, src_tok, src_tok, dst_row, h2_tiles, w_in, w_in, w_out)


def _combine_kernel(y1_ref, y2_ref, x_ref, mod_ref, g1_ref, g2_ref, *rest):
    o_ref = rest[-1]
    tm, d = x_ref.shape
    g1 = jnp.concatenate([g1_ref[...]] * (d // 128), axis=-1)
    g2 = jnp.concatenate([g2_ref[...]] * (d // 128), axis=-1)
    y = g1 * _from_token_tiles(y1_ref, tm, d) + g2 * _from_token_tiles(y2_ref, tm, d)
    x = x_ref[...] + mod_ref[0, 5:6, :] * y
    if len(rest) == 2:
        ms = jnp.mean(x * x, axis=-1, keepdims=True)
        x = x * lax.rsqrt(ms + EPS) * rest[0][...]
    o_ref[...] = x


def _combine(y_tiles, x, mod, g1, g2, t_lat, seq, tm, final_g):
    t, d = x.shape
    k = d // 128
    extra_specs, extra_args = [], []
    if final_g is not None:
        extra_specs, extra_args = [pl.BlockSpec((1, d), lambda i: (0, 0))], [final_g.reshape(1, d)]
    return pl.pallas_call(
        _combine_kernel,
        out_shape=jax.ShapeDtypeStruct((t, d), F32),
        grid=(t // tm,),
        in_specs=[pl.BlockSpec((tm * k, 128), lambda i: (i, 0)),
                  pl.BlockSpec((tm * k, 128), lambda i: (t // tm + i, 0)),
                  pl.BlockSpec((tm, d), lambda i: (i, 0)),
                  pl.BlockSpec((1, 6, d), lambda i: (_mod_group(i * tm, t_lat, seq), 0, 0)),
                  pl.BlockSpec((tm, 128), lambda i: (i, 0)),
                  pl.BlockSpec((tm, 128), lambda i: (i, 0))] + extra_specs,
        out_specs=pl.BlockSpec((tm, d), lambda i: (i, 0)),
        compiler_params=_cp("parallel"),
        name="moe_combine",
    )(y_tiles, y_tiles, x, mod, g1, g2, *extra_args)


def _moe(h2, h2_tiles, x, mod, router, w_in, w_out, t_lat, seq, tm, final_g=None):
    t, d = h2.shape
    e_idx, g1, g2 = _router(h2, router, tm)
    e_flat = e_idx[:, :2].reshape(-1)
    onehot = (e_flat[:, None] == jnp.arange(N_EXPERTS, dtype=jnp.int32)[None, :]).astype(jnp.int32)
    csum = jnp.cumsum(onehot, axis=0)
    counts = csum[-1]
    padded = (counts + MOE_TILE - 1) // MOE_TILE * MOE_TILE
    ends = jnp.cumsum(padded)
    pstarts = ends - padded
    dest = jnp.sum(onehot * (pstarts[None, :] + csum - 1), axis=1).astype(jnp.int32)
    n_tiles = (2 * t + MOE_TILE - 1) // MOE_TILE + N_EXPERTS
    n_rows = n_tiles * MOE_TILE
    slot_a = jnp.full((n_rows,), -1, jnp.int32).at[dest].set(jnp.arange(2 * t, dtype=jnp.int32))
    is_pad = slot_a < 0
    slot_row = jnp.where(is_pad, 2 * t - 1 + jnp.cumsum(is_pad.astype(jnp.int32)),
                         (slot_a % 2) * t + slot_a // 2)
    src_tok = jnp.where(is_pad, 0, slot_a // 2)
    src_tok = jnp.concatenate([src_tok, jnp.zeros((MOE_TILE,), jnp.int32)]).reshape(n_tiles + 1, 1, MOE_TILE)
    tile_start = jnp.arange(n_tiles + 1, dtype=jnp.int32) * MOE_TILE
    tile_e = jnp.minimum(jnp.searchsorted(ends, tile_start, side='right'), N_EXPERTS - 1).astype(jnp.int32)

    y = _experts(h2_tiles, src_tok, slot_row.reshape(n_tiles, 1, MOE_TILE), tile_e, w_in, w_out, n_rows)
    return _combine(y, x, mod, g1, g2, t_lat, seq, tm, final_g)


def _final_norm_kernel(x_ref, g_ref, o_ref):
    x = x_ref[...]
    ms = jnp.mean(x * x, axis=-1, keepdims=True)
    o_ref[...] = x * lax.rsqrt(ms + EPS) * g_ref[...]


def _final_norm(x, g, tm):
    t, d = x.shape
    return pl.pallas_call(
        _final_norm_kernel,
        out_shape=jax.ShapeDtypeStruct((t, d), F32),
        grid=(t // tm,),
        in_specs=[pl.BlockSpec((tm, d), lambda i: (i, 0)), pl.BlockSpec((1, d), lambda i: (0, 0))],
        out_specs=pl.BlockSpec((tm, d), lambda i: (i, 0)),
        compiler_params=_cp("parallel"),
        name="final_norm",
    )(x, g.reshape(1, d))


def _proj_weights(w_in):
    o = {}
    acc = 0
    for name, size in (('a_z', 256), ('a_x', 256), ('a_b', 256), ('a_c', 256), ('a_dt', 8), ('b_u', 256),
                       ('c_q', 256), ('c_k', 128), ('c_v', 128), ('d_q', 256), ('d_k', 128), ('d_v', 128),
                       ('gates', 4096)):
        o[name] = (acc, size)
        acc += size
    order = ('gates', 'a_z', 'b_u', 'c_q', 'd_q', 'c_k', 'c_v', 'd_k', 'd_v', 'a_x', 'a_b', 'a_c')
    w = jnp.concatenate([w_in[:, :, o[n][0]:o[n][0] + o[n][1]] for n in order], axis=-1).astype(BF16)
    dt0 = o['a_dt'][0]
    wdt = jnp.pad(w_in[:, :, dt0:dt0 + 8], ((0, 0), (0, 0), (0, 120))).astype(BF16)
    return w, wdt


def kernel(x, c, ctx, c_ctx, norm1_g, norm2_g, ada_w, ada_b, w_in, ssd_conv_w, ssd_conv_b, ssd_a_log,
           ssd_dt_bias, ssd_d, ssd_norm_g, s5_lam_re, s5_lam_im, s5_log_step, s5_b_re, s5_b_im, s5_c_re,
           s5_c_im, s5_d, s5_glu_w, qk_norm_g, swa_sink, w_branch, w_out, ffn_w_in, ffn_w_out, moe_router,
           moe_w_in, moe_w_out, final_norm_g):
    bsz, seq, d = x.shape
    n_ctx = ctx.shape[1]
    depth = w_in.shape[0]
    t_lat, t_ctx = bsz * seq, bsz * n_ctx
    tm = _pow2_tile(1024, seq, t_ctx)
    tm_small = _pow2_tile(512, seq, t_ctx)

    cvec = jnp.zeros((16, d), F32).at[0].set(c_ctx).at[1:1 + bsz].set(c)
    mod = _adaln(cvec, ada_w, ada_b).reshape(depth, 16, 6, d)
    wp, wdt = _proj_weights(w_in)
    wbr = w_branch.astype(BF16)
    wo = w_out.astype(BF16)
    ffn_in, ffn_out = ffn_w_in.astype(BF16), ffn_w_out.astype(BF16)
    moe_in, moe_out = moe_w_in.astype(BF16), moe_w_out.astype(BF16)

    xx = jnp.concatenate([x.reshape(t_lat, d), ctx.reshape(t_ctx, d)], axis=0)
    for l in range(depth):
        with_ctx = l < depth - 1
        t_out = t_lat + (t_ctx if with_ctx else 0)
        p, pdt = _inproj(xx, norm1_g[l], mod[l], wp[l], wdt[l], t_lat, seq, tm)
        ya = _ssd(p, pdt, ssd_conv_w[l], ssd_conv_b[l], ssd_a_log[l], ssd_dt_bias[l], ssd_d[l],
                  ssd_norm_g[l], bsz, seq, n_ctx)
        yb = _s5(p, s5_lam_re[l], s5_lam_im[l], s5_log_step[l], s5_b_re[l], s5_b_im[l], s5_c_re[l],
                 s5_c_im[l], s5_d[l], s5_glu_w[l], bsz, seq, n_ctx)
        q1, q2, k1, k2 = _prep(p, qk_norm_g[l], t_lat, seq, tm)
        yc = _gattn(q1, k1, p, bsz, seq, n_ctx, with_ctx)
        yd = _wattn(q2, k2, p, swa_sink[l], bsz, seq, n_ctx, with_ctx)
        routed = l % 2 == 1
        xx, h2, *tiles = _merge((ya, yb, yc, yd), p, xx, mod[l], wbr[l], wo[l], norm2_g[l], t_out, t_lat, seq,
                                tm_small, routed)
        if routed:
            xx = _moe(h2, tiles[0], xx, mod[l], moe_router[l // 2], moe_in[l // 2], moe_out[l // 2], t_lat,
                      seq, tm_small, None if with_ctx else final_norm_g)
        else:
            xx = _ffn_dense(h2, xx, mod[l], ffn_in[l // 2], ffn_out[l // 2], t_lat, seq, tm)
    if depth % 2 == 1:
        xx = _final_norm(xx[:t_lat], final_norm_g, tm)
    return xx.reshape(bsz, seq, d)
```

```python
import functools

import numpy as np
import jax
import jax.numpy as jnp
from jax import lax
from jax.experimental import pallas as pl
from jax.experimental.pallas import tpu as pltpu

F32 = jnp.float32
BF16 = jnp.bfloat16

EPS = 1e-6
NEG_INF = -1e30
GRID_W = 64
MIX_W = 256
HEAD_DIM = 64
N_HEADS = 4
ATTN_SCALE = HEAD_DIM ** -0.5
SSD_STATE = 128
SSD_CONV = 5
CHUNK = 128
HALO = 16
S5_GROUPS = 16
S5_GROUP = 16
S5_STATE = 64
S5_LANES = S5_GROUPS * S5_STATE
S5_STEPS = 64
ROPE_BASE = 10000.0
ROPE_FREQS = 16
WINDOW = 128
KT_UNIT = 256
N_EXPERTS = 8
MOE_TILE = 512
MOE_STEPS = 7
MOE_WAITS = 64
TOK_ROWS = 8
VMEM_LIMIT = 56 * 1024 * 1024

C_GATES, C_Z, C_BU, C_CQ, C_DQ = 0, 4096, 4352, 4608, 4864
C_CK, C_CV, C_DK, C_DV = 5120, 5248, 5376, 5504
C_AX, C_AB, C_AC = 5632, 5888, 6144
P_COLS = 6400
PROJ_TN = 1280


def _cp(*sem):
    return pltpu.CompilerParams(dimension_semantics=sem, vmem_limit_bytes=VMEM_LIMIT)


def _pow2_tile(cap, *dims):
    t = 1
    while t * 2 <= cap and all(d % (t * 2) == 0 for d in dims):
        t *= 2
    return t


def _dot(a, b):
    return jnp.dot(a, b, preferred_element_type=F32)


def _dot_nt(a, b):
    return lax.dot_general(a, b, (((1,), (1,)), ((), ())), preferred_element_type=F32)


def _dot_tn(a, b):
    return lax.dot_general(a, b, (((0,), (0,)), ((), ())), preferred_element_type=F32)


def _split3(x):
    hi = x.astype(BF16)
    r1 = x - hi.astype(F32)
    mid = r1.astype(BF16)
    lo = (r1 - mid.astype(F32)).astype(BF16)
    return hi, mid, lo


def _silu(x):
    return x * jax.nn.sigmoid(x)


def _adaln_kernel(c_ref, w_ref, b_ref, o_ref):
    c = c_ref[...]
    o_ref[0] = jnp.dot(_silu(c), w_ref[0], preferred_element_type=F32,
                       precision=lax.Precision.HIGHEST) + b_ref[0]


def _adaln(cvec, ada_w, ada_b):
    depth, d, n = ada_w.shape
    tn = 1024
    return pl.pallas_call(
        _adaln_kernel,
        out_shape=jax.ShapeDtypeStruct((depth, 16, n), F32),
        grid=(depth, n // tn),
        in_specs=[pl.BlockSpec((16, d), lambda l, j: (0, 0)),
                  pl.BlockSpec((1, d, tn), lambda l, j: (l, 0, j)),
                  pl.BlockSpec((1, 1, tn), lambda l, j: (l, 0, j))],
        out_specs=pl.BlockSpec((1, 16, tn), lambda l, j: (l, 0, j)),
        compiler_params=_cp("parallel", "parallel"),
        name="adaln",
    )(cvec, ada_w, ada_b.reshape(depth, 1, n))


def _mod_group(row0, t_lat, seq):
    return jnp.where(row0 >= t_lat, 0, 1 + row0 // seq)


def _inproj_kernel(x_ref, g_ref, mod_ref, w_ref, wdt_ref, o_ref, odt_ref, h_ref):
    @pl.when(pl.program_id(1) == 0)
    def _():
        x = x_ref[...]
        ms = jnp.mean(x * x, axis=-1, keepdims=True)
        y = x * lax.rsqrt(ms + EPS) * g_ref[...]
        h = (y * (1.0 + mod_ref[0, 1:2, :]) + mod_ref[0, 0:1, :]).astype(BF16)
        h_ref[...] = h
        odt_ref[...] = _dot(h, wdt_ref[...])

    o_ref[...] = _dot(h_ref[...], w_ref[...]).astype(BF16)


def _inproj(x, g, mod, w, wdt, t_lat, seq, tm):
    t, d = x.shape
    n = w.shape[1]
    tn = PROJ_TN
    return pl.pallas_call(
        _inproj_kernel,
        out_shape=(jax.ShapeDtypeStruct((t, n), BF16), jax.ShapeDtypeStruct((t, 128), F32)),
        grid=(t // tm, n // tn),
        in_specs=[pl.BlockSpec((tm, d), lambda i, j: (i, 0)),
                  pl.BlockSpec((1, d), lambda i, j: (0, 0)),
                  pl.BlockSpec((1, 6, d), lambda i, j: (_mod_group(i * tm, t_lat, seq), 0, 0)),
                  pl.BlockSpec((d, tn), lambda i, j: (0, j)),
                  pl.BlockSpec((d, 128), lambda i, j: (0, 0))],
        out_specs=(pl.BlockSpec((tm, tn), lambda i, j: (i, j)),
                   pl.BlockSpec((tm, 128), lambda i, j: (i, 0))),
        scratch_shapes=[pltpu.VMEM((tm, d), BF16)],
        compiler_params=_cp("parallel", "arbitrary"),
        name="inproj",
    )(x, g.reshape(1, d), mod, w, wdt)


def _ssd_consts():
    r = np.arange(CHUNK)
    tri_l = (r[None, :] <= r[:, None]).astype(np.float32)
    tri_u = tri_l.T.copy()
    shifts = np.zeros((4, CHUNK, CHUNK + 2 * HALO), np.float32)
    for n, k in enumerate((0, 1, 3, 4)):
        shifts[n, r, r + HALO + k - 2] = 1.0
    spread = np.zeros((2, 128, MIX_W), np.float32)
    for d in range(2):
        for h in range(N_HEADS):
            spread[d, N_HEADS * d + h, HEAD_DIM * h:HEAD_DIM * (h + 1)] = 1.0
    return (jnp.asarray(np.stack([tri_l, tri_u]), BF16), jnp.asarray(shifts, BF16),
            jnp.asarray(spread, BF16))


def _ssd_kernel(zl_ref, xl_ref, bl_ref, cl_ref, dtl_ref, zc_ref, xc_ref, bc_ref, cc_ref, dtc_ref,
                cw_ref, cb_ref, an_ref, bias_ref, dsk_ref, ng_ref, tri_ref, sh_ref, e_ref,
                yl_ref, yc_ref, act_l, act_c, yf_l, yf_c, stf_ref, stb_ref, *, n_lat, n_ctx):
    q = CHUNK
    lane128 = lax.broadcasted_iota(jnp.int32, (q, 128), 1)
    lane256 = lax.broadcasted_iota(jnp.int32, (q, MIX_W), 1)
    lane256r = lax.broadcasted_iota(jnp.int32, (1, MIX_W), 1)
    row_i = lax.broadcasted_iota(jnp.int32, (q, q), 0)
    col_i = lax.broadcasted_iota(jnp.int32, (q, q), 1)
    head_masks = [(lane256 >= HEAD_DIM * h) & (lane256 < HEAD_DIM * (h + 1)) for h in range(N_HEADS)]
    head_masks_r = [(lane256r >= HEAD_DIM * h) & (lane256r < HEAD_DIM * (h + 1)) for h in range(N_HEADS)]
    group_masks = [lane256 < 128, lane256 >= 128]
    lane_state = lax.broadcasted_iota(jnp.int32, (SSD_STATE, MIX_W), 1)
    state_group_masks = [lane_state < 128, lane_state >= 128]

    def conv_act(x_ref, b_ref, c_ref, n_chunks, c):
        def rows(ref, start, size):
            return ref[pl.ds(start, size), :]

        start = c * q
        if isinstance(c, int):
            p0, n0 = max(start - HALO, 0), min(start + q, n_chunks * q - HALO)
            pf, nf = float(c > 0), float(c < n_chunks - 1)
        else:
            start = pl.multiple_of(start, q)
            p0 = pl.multiple_of(jnp.maximum(start - HALO, 0), HALO)
            n0 = pl.multiple_of(jnp.minimum(start + q, n_chunks * q - HALO), HALO)
            pf, nf = (c > 0).astype(F32), (c < n_chunks - 1).astype(F32)
        parts = []
        for ref in (x_ref, b_ref, c_ref):
            prev = (rows(ref, p0, HALO).astype(F32) * pf).astype(BF16)
            nxt = (rows(ref, n0, HALO).astype(F32) * nf).astype(BF16)
            parts.append(jnp.concatenate([prev, rows(ref, start, q), nxt], axis=0))
        ext = jnp.concatenate(parts, axis=1)
        cur = ext[HALO:HALO + q].astype(F32)
        acc = cur * cw_ref[2:3, :] + cb_ref[...]
        for n, k in enumerate((0, 1, 3, 4)):
            acc = acc + _dot(sh_ref[n], ext) * cw_ref[k:k + 1, :]
        return _silu(acc)

    def conv_chunk(seg, c):
        _, x_ref, b_ref, c_ref, _, act_ref, _, _, n_chunks = seg
        start = c * q if isinstance(c, int) else pl.multiple_of(c * q, q)
        act_ref[pl.ds(start, q), :] = conv_act(x_ref, b_ref, c_ref, n_chunks, c).astype(BF16)

    def chunk(seg, c, d, second):
        z_ref, x_ref, b_ref, c_ref, dt_ref, act_ref, yf_ref, y_ref, n_chunks = seg
        st_ref = st_refs[d]
        start = c * q if isinstance(c, int) else pl.multiple_of(c * q, q)
        act = act_ref[pl.ds(start, q), :]
        xs = act[:, 0:MIX_W].astype(F32)
        bm = act[:, MIX_W:2 * MIX_W]
        cm = act[:, 2 * MIX_W:3 * MIX_W]

        dt_n = jax.nn.softplus(dt_ref[pl.ds(start, q), :] + bias_ref[...])
        la_n = dt_n * an_ref[...]
        hi, mid, lo = _split3(la_n)
        tri = tri_ref[d]
        cs_n = _dot(tri, hi) + _dot(tri, mid) + _dot(tri, lo)
        cs_t = cs_n.T
        edge = q - 1 if d == 0 else 0
        tri_mask = (col_i <= row_i) if d == 0 else (col_i >= row_i)

        tot_n = cs_n[edge:edge + 1, :]
        spread = e_ref[d]
        dt_full = _dot(dt_n.astype(BF16), spread)
        dtdte_full = _dot((dt_n * jnp.exp(tot_n - cs_n)).astype(BF16), spread)
        ecs_full = _dot(jnp.exp(cs_n).astype(BF16), spread)
        etot_n = jnp.exp(tot_n)
        tot_full = jnp.zeros((1, MIX_W), F32)
        decay = []
        for h in range(N_HEADS):
            sel = lane128 == (N_HEADS * d + h)
            cs_col = jnp.sum(jnp.where(sel, cs_n, 0.0), axis=-1, keepdims=True)
            cs_row = cs_t[N_HEADS * d + h:N_HEADS * d + h + 1, :]
            decay.append(jnp.where(tri_mask, jnp.exp(cs_col - cs_row), 0.0))
            etot = jnp.sum(jnp.where(sel[0:1, :], etot_n, 0.0), axis=-1, keepdims=True)
            tot_full = jnp.where(head_masks_r[h], etot, tot_full)

        xdt = xs * dt_full
        state = st_ref[...]
        y = jnp.zeros((q, MIX_W), F32)
        y_off = jnp.zeros((q, MIX_W), F32)
        upd = jnp.zeros((SSD_STATE, MIX_W), F32)
        xdte = xs * dtdte_full
        for g in range(2):
            bg = bm[:, 128 * g:128 * (g + 1)]
            cg = cm[:, 128 * g:128 * (g + 1)]
            cb = _dot_nt(cg, bg)
            for h in (2 * g, 2 * g + 1):
                m = (cb * decay[h]).astype(BF16)
                y = y + _dot(m, jnp.where(head_masks[h], xdt, 0.0).astype(BF16))
            y_off = y_off + _dot(cg, jnp.where(state_group_masks[g], state, 0.0).astype(BF16))
            upd = upd + _dot_tn(bg, jnp.where(group_masks[g], xdte, 0.0).astype(BF16))
        y = y + y_off * ecs_full
        st_ref[...] = state * tot_full + upd

        if not second:
            yf_ref[pl.ds(start, q), :] = y
        else:
            y = y + yf_ref[pl.ds(start, q), :] + dsk_ref[...] * xs
            y = y * _silu(z_ref[pl.ds(start, q), :].astype(F32))
            ms = jnp.mean(y * y, axis=-1, keepdims=True)
            y_ref[pl.ds(start, q), :] = (y * lax.rsqrt(ms + EPS) * ng_ref[...]).astype(BF16)

    seg_l = (zl_ref, xl_ref, bl_ref, cl_ref, dtl_ref, act_l, yf_l, yl_ref, n_lat)
    seg_c = (zc_ref, xc_ref, bc_ref, cc_ref, dtc_ref, act_c, yf_c, yc_ref, n_ctx)
    st_refs = (stf_ref, stb_ref)
    stf_ref[...] = jnp.zeros_like(stf_ref)
    stb_ref[...] = jnp.zeros_like(stb_ref)

    for c in range(n_ctx):
        conv_chunk(seg_c, c)

    def conv_body(i, carry):
        conv_chunk(seg_l, i)
        return carry

    lax.fori_loop(0, n_lat, conv_body, 0)

    for i in range(n_ctx):
        chunk(seg_c, i, 0, 2 * i >= n_ctx)
        chunk(seg_c, n_ctx - 1 - i, 1, 2 * i >= n_ctx - 1)

    def pair(second):
        def body(i, carry):
            chunk(seg_l, i, 0, second)
            chunk(seg_l, n_lat - 1 - i, 1, second)
            return carry
        return body

    lax.fori_loop(0, n_lat // 2, pair(False), 0, unroll=2)
    lax.fori_loop(n_lat // 2, n_lat, pair(True), 0, unroll=2)


def _ssd(p, pdt, conv_w, conv_b, a_log, dt_bias, d_skip, norm_g, bsz, seq, ctx):
    t = p.shape[0]
    n_lat, n_ctx = seq // CHUNK, ctx // CHUNK
    assert n_lat % 2 == 0
    cb0 = (bsz * seq) // ctx
    tri, shifts, spread = _ssd_consts()
    cw = jnp.zeros((8, 3 * MIX_W), F32).at[:SSD_CONV].set(conv_w)
    a_n = jnp.zeros((1, 128), F32).at[0, :8].set(-jnp.exp(a_log.astype(F32)).reshape(8))
    bias_n = jnp.zeros((1, 128), F32).at[0, :8].set(dt_bias.astype(F32).reshape(8))
    dsk = jnp.repeat(d_skip.astype(F32), HEAD_DIM).reshape(1, MIX_W)

    def lat(col):
        return pl.BlockSpec((seq, MIX_W), lambda b, col=col: (b, col // MIX_W))

    def cx(col):
        return pl.BlockSpec((ctx, MIX_W), lambda b, col=col: (cb0 + b, col // MIX_W))

    def full(shape):
        return pl.BlockSpec(shape, lambda b: (0,) * len(shape))

    kern = functools.partial(_ssd_kernel, n_lat=n_lat, n_ctx=n_ctx)
    yl, yc = pl.pallas_call(
        kern,
        out_shape=(jax.ShapeDtypeStruct((bsz * seq, MIX_W), BF16),
                   jax.ShapeDtypeStruct((bsz * ctx, MIX_W), BF16)),
        grid=(bsz,),
        in_specs=[lat(C_Z), lat(C_AX), lat(C_AB), lat(C_AC),
                  pl.BlockSpec((seq, 128), lambda b: (b, 0)),
                  cx(C_Z), cx(C_AX), cx(C_AB), cx(C_AC),
                  pl.BlockSpec((ctx, 128), lambda b: (cb0 + b, 0)),
                  full((8, 3 * MIX_W)), full((1, 3 * MIX_W)), full((1, 128)), full((1, 128)),
                  full((1, MIX_W)), full((1, MIX_W)), full((2, CHUNK, CHUNK)),
                  full((4, CHUNK, CHUNK + 2 * HALO)), full((2, 128, MIX_W))],
        out_specs=(pl.BlockSpec((seq, MIX_W), lambda b: (b, 0)),
                   pl.BlockSpec((ctx, MIX_W), lambda b: (b, 0))),
        scratch_shapes=[pltpu.VMEM((seq, 3 * MIX_W), BF16), pltpu.VMEM((ctx, 3 * MIX_W), BF16),
                        pltpu.VMEM((seq, MIX_W), F32), pltpu.VMEM((ctx, MIX_W), F32),
                        pltpu.VMEM((SSD_STATE, MIX_W), F32), pltpu.VMEM((SSD_STATE, MIX_W), F32)],
        compiler_params=_cp("parallel"),
        name="ssd",
    )(p, p, p, p, pdt, p, p, p, p, pdt, cw, conv_b.reshape(1, -1).astype(F32), a_n, bias_n,
      dsk, norm_g.reshape(1, MIX_W).astype(F32), tri, shifts, spread)
    return yl, yc


def _s5_discretize(lam_re, lam_im, log_step, b_re, b_im):
    step = jnp.exp(log_step.astype(F32))[:, None]
    lr = jnp.minimum(lam_re.astype(F32), -1e-4)
    li = lam_im.astype(F32)
    mag = jnp.exp(lr * step)
    ang = li * step
    ab_re, ab_im = mag * jnp.cos(ang), mag * jnp.sin(ang)
    den = lr * lr + li * li
    f_re = ((ab_re - 1.0) * lr + ab_im * li) / den
    f_im = (ab_im * lr - (ab_re - 1.0) * li) / den
    br, bi = b_re.astype(F32), b_im.astype(F32)
    bb_re = f_re[..., None] * br - f_im[..., None] * bi
    bb_im = f_re[..., None] * bi + f_im[..., None] * br
    return ab_re, ab_im, bb_re, bb_im


def _s5_mats(lam_re, lam_im, log_step, b_re, b_im, c_re, c_im):
    eye = jnp.eye(S5_GROUPS, dtype=F32)
    a_all, b_all, c_all = [], [], []
    for d in range(2):
        ab_re, ab_im, bb_re, bb_im = _s5_discretize(lam_re[d], lam_im[d], log_step[d], b_re[d], b_im[d])
        bm = [jnp.einsum('gnp,gh->gphn', m, eye).reshape(MIX_W, S5_LANES) for m in (bb_re, bb_im)]
        cm = [jnp.einsum('gpn,gh->gnhp', m.astype(F32), eye).reshape(S5_LANES, MIX_W)
              for m in (c_re[d], c_im[d])]
        b_all.append(jnp.concatenate(bm, axis=1))
        c_all.append(jnp.concatenate([cm[0], -cm[1]], axis=0))
        a_all.append(jnp.concatenate([ab_re.reshape(1, S5_LANES), ab_im.reshape(1, S5_LANES)], axis=1))
    a = jnp.broadcast_to(jnp.stack(a_all), (2, 8, 2 * S5_LANES))
    return a, jnp.stack(b_all).astype(BF16), jnp.stack(c_all).astype(BF16)


def _s5_kernel(uf_ref, ub_ref, a_ref, b_ref, c_ref, yf_ref, yb_ref, buf_ref, s_ref):
    n = S5_LANES

    @pl.when(pl.program_id(0) == 0)
    def _():
        s_ref[...] = jnp.zeros_like(s_ref)

    for d, u_ref in enumerate((uf_ref, ub_ref)):
        buf_ref[d] = _dot(u_ref[...], b_ref[d])
    state = [(s_ref[d, :, 0:n], s_ref[d, :, n:2 * n]) for d in range(2)]
    for j in range(S5_STEPS):
        for d in range(2):
            jj = j if d == 0 else S5_STEPS - 1 - j
            rows = slice(jj * 8, jj * 8 + 8)
            a_re, a_im = a_ref[d, :, 0:n], a_ref[d, :, n:2 * n]
            s_re, s_im = state[d]
            n_re = a_re * s_re - a_im * s_im + buf_ref[d, rows, 0:n]
            n_im = a_re * s_im + a_im * s_re + buf_ref[d, rows, n:2 * n]
            buf_ref[d, rows, 0:n] = n_re
            buf_ref[d, rows, n:2 * n] = n_im
            state[d] = (n_re, n_im)
    for d, y_ref in enumerate((yf_ref, yb_ref)):
        s_ref[d, :, 0:n] = state[d][0]
        s_ref[d, :, n:2 * n] = state[d][1]
        y_ref[...] = _dot(buf_ref[d].astype(BF16), c_ref[d])


def _s5_finish_kernel(yf_ref, yb_ref, u_ref, d_ref, w_ref, o_ref):
    y = yf_ref[...] + yb_ref[...] + d_ref[...] * u_ref[...].astype(F32)
    v = jax.nn.gelu(y, approximate=True).astype(BF16)
    r = _dot(v, w_ref[...])
    o_ref[...] = (r[:, 0:MIX_W] * jax.nn.sigmoid(r[:, MIX_W:2 * MIX_W])).astype(BF16)


def _s5(p, lam_re, lam_im, log_step, b_re, b_im, c_re, c_im, d_skip, glu_w, bsz, seq, ctx):
    assert bsz <= 8
    t_lat = bsz * seq
    u = p[:, C_BU:C_BU + MIX_W]
    u_l = jnp.transpose(u[:t_lat].reshape(bsz, seq, MIX_W), (1, 0, 2))
    u_c = jnp.transpose(u[t_lat:].reshape(bsz, ctx, MIX_W), (1, 0, 2))
    u_tm = jnp.concatenate([u_c, u_l], axis=0)
    if bsz < 8:
        u_tm = jnp.pad(u_tm, ((0, 0), (0, 8 - bsz), (0, 0)))
    steps = seq + ctx
    u_tm = u_tm.reshape(steps * 8, MIX_W)
    a, bmat, cmat = _s5_mats(lam_re, lam_im, log_step, b_re, b_im, c_re, c_im)
    rows = S5_STEPS * 8
    nc, ncc = steps // S5_STEPS, ctx // S5_STEPS

    def bwd(i):
        return jnp.where(i < ncc, ncc - 1 - i, nc + ncc - 1 - i)

    def whole(shape):
        return pl.BlockSpec(shape, lambda i: (0,) * len(shape))

    yshape = jax.ShapeDtypeStruct((steps * 8, MIX_W), F32)
    yf, yb = pl.pallas_call(
        _s5_kernel,
        out_shape=(yshape, yshape),
        grid=(nc,),
        in_specs=[pl.BlockSpec((rows, MIX_W), lambda i: (i, 0)),
                  pl.BlockSpec((rows, MIX_W), lambda i: (bwd(i), 0)),
                  whole((2, 8, 2 * S5_LANES)), whole((2, MIX_W, 2 * S5_LANES)),
                  whole((2, 2 * S5_LANES, MIX_W))],
        out_specs=(pl.BlockSpec((rows, MIX_W), lambda i: (i, 0)),
                   pl.BlockSpec((rows, MIX_W), lambda i: (bwd(i), 0))),
        scratch_shapes=[pltpu.VMEM((2, rows, 2 * S5_LANES), F32), pltpu.VMEM((2, 8, 2 * S5_LANES), F32)],
        compiler_params=_cp("arbitrary"),
        name="s5_scan",
    )(u_tm, u_tm, a, bmat, cmat)

    tmf = _pow2_tile(2048, steps * 8)
    o = pl.pallas_call(
        _s5_finish_kernel,
        out_shape=jax.ShapeDtypeStruct((steps * 8, MIX_W), BF16),
        grid=(steps * 8 // tmf,),
        in_specs=[pl.BlockSpec((tmf, MIX_W), lambda i: (i, 0)),
                  pl.BlockSpec((tmf, MIX_W), lambda i: (i, 0)),
                  pl.BlockSpec((tmf, MIX_W), lambda i: (i, 0)),
                  pl.BlockSpec((1, MIX_W), lambda i: (0, 0)),
                  pl.BlockSpec((MIX_W, 2 * MIX_W), lambda i: (0, 0))],
        out_specs=pl.BlockSpec((tmf, MIX_W), lambda i: (i, 0)),
        compiler_params=_cp("parallel"),
        name="s5_finish",
    )(yf, yb, u_tm, d_skip.reshape(1, MIX_W).astype(F32), glu_w.astype(BF16))
    o = o.reshape(steps, 8, MIX_W)[:, :bsz]
    o_c = jnp.transpose(o[:ctx], (1, 0, 2)).reshape(bsz * ctx, MIX_W)
    o_l = jnp.transpose(o[ctx:], (1, 0, 2)).reshape(t_lat, MIX_W)
    return o_l, o_c


def _rope_tables(seq, tm):
    rows = seq // GRID_W
    pos_r = jnp.repeat(jnp.arange(rows, dtype=F32), GRID_W)
    pos_c = jnp.tile(jnp.arange(GRID_W, dtype=F32), rows)
    inv = ROPE_BASE ** (-jnp.arange(ROPE_FREQS, dtype=F32) / ROPE_FREQS)
    ar, ac = pos_r[:, None] * inv, pos_c[:, None] * inv
    cos = jnp.concatenate([jnp.cos(ar), jnp.cos(ar), jnp.cos(ac), jnp.cos(ac)], axis=-1)
    sin = jnp.concatenate([-jnp.sin(ar), jnp.sin(ar), -jnp.sin(ac), jnp.sin(ac)], axis=-1)
    cos = jnp.concatenate([jnp.tile(cos, (1, 2)), jnp.ones((tm, 128), F32)], axis=0)
    sin = jnp.concatenate([jnp.tile(sin, (1, 2)), jnp.zeros((tm, 128), F32)], axis=0)
    return cos, sin


def _prep_consts():
    i = np.arange(MIX_W)
    bd = ((i[:, None] // HEAD_DIM) == (i[None, :] // HEAD_DIM)).astype(np.float32) / HEAD_DIM
    pm = (i[:, None] == (i[None, :] ^ ROPE_FREQS)).astype(np.float32)
    return jnp.asarray(bd, BF16), jnp.asarray(pm, BF16)


def _prep_kernel(cq_ref, dq_ref, ck_ref, dk_ref, cos_ref, sin_ref, qg_ref, kg_ref, bd_ref, pm_ref,
                 q1_ref, q2_ref, k1_ref, k2_ref):
    cos, sin = cos_ref[...], sin_ref[...]
    cos2 = jnp.concatenate([cos, cos], axis=-1)
    sin2 = jnp.concatenate([sin, sin], axis=-1)
    bd, pm = bd_ref[...], pm_ref[...]
    tm = cos.shape[0]
    lane = lax.broadcasted_iota(jnp.int32, (tm, 128), 1)

    def rms(x, g, n):
        ms = _dot((x * x).astype(BF16), bd[:n, :n])
        return x * lax.rsqrt(ms + EPS) * g

    def rope(y, c, s, n):
        return y * c + _dot(y.astype(BF16), pm[:n, :n]) * s

    def store_q(q, ref):
        q = q * ATTN_SCALE
        for kv in range(2):
            for g in range(2):
                half = q[:, 128 * kv:128 * (kv + 1)]
                if g != kv:
                    half = pltpu.roll(half, HEAD_DIM, 1)
                keep = (lane >= HEAD_DIM * kv) & (lane < HEAD_DIM * (kv + 1))
                ref[2 * kv + g] = jnp.where(keep, half, 0.0).astype(BF16)

    store_q(rope(rms(cq_ref[...].astype(F32), qg_ref[...], MIX_W), cos2, sin2, MIX_W), q1_ref)
    store_q(rope(dq_ref[...].astype(F32), cos2, sin2, MIX_W), q2_ref)
    k1 = rope(rms(ck_ref[...].astype(F32), kg_ref[...], 128), cos, sin, 128)
    for u in range(tm // KT_UNIT):
        k1_ref[u] = k1[u * KT_UNIT:(u + 1) * KT_UNIT].T.astype(BF16)
    k2 = rope(dk_ref[...].astype(F32), cos, sin, 128)
    for u in range(tm // WINDOW):
        k2_ref[u] = k2[u * WINDOW:(u + 1) * WINDOW].T.astype(BF16)


def _prep(p, qk_gain, t_lat, seq, tm):
    t = p.shape[0]
    cos, sin = _rope_tables(seq, tm)
    bd, pm = _prep_consts()
    qg = jnp.tile(qk_gain[0].astype(F32), N_HEADS).reshape(1, MIX_W)
    kg = jnp.tile(qk_gain[1].astype(F32), 2).reshape(1, 128)
    nt = seq // tm

    def tab(i):
        return (jnp.where(i * tm >= t_lat, nt, i % nt), 0)

    def const(shape):
        return pl.BlockSpec(shape, lambda i: (0,) * len(shape))

    qshape = jax.ShapeDtypeStruct((N_HEADS, t, 128), BF16)
    ktshape = jax.ShapeDtypeStruct((t // KT_UNIT, 128, KT_UNIT), BF16)
    kt2shape = jax.ShapeDtypeStruct((t // WINDOW, 128, WINDOW), BF16)
    return pl.pallas_call(
        _prep_kernel,
        out_shape=(qshape, qshape, ktshape, kt2shape),
        grid=(t // tm,),
        in_specs=[pl.BlockSpec((tm, MIX_W), lambda i: (i, C_CQ // MIX_W)),
                  pl.BlockSpec((tm, MIX_W), lambda i: (i, C_DQ // MIX_W)),
                  pl.BlockSpec((tm, 128), lambda i: (i, C_CK // 128)),
                  pl.BlockSpec((tm, 128), lambda i: (i, C_DK // 128)),
                  pl.BlockSpec((tm, 128), tab), pl.BlockSpec((tm, 128), tab),
                  const((1, MIX_W)), const((1, 128)), const((MIX_W, MIX_W)), const((MIX_W, MIX_W))],
        out_specs=(pl.BlockSpec((N_HEADS, tm, 128), lambda i: (0, i, 0)),
                   pl.BlockSpec((N_HEADS, tm, 128), lambda i: (0, i, 0)),
                   pl.BlockSpec((tm // KT_UNIT, 128, KT_UNIT), lambda i: (i, 0, 0)),
                   pl.BlockSpec((tm // WINDOW, 128, WINDOW), lambda i: (i, 0, 0))),
        compiler_params=_cp("parallel"),
        name="qk_prep",
    )(p, p, p, p, cos, sin, qg, kg, bd, pm)


def _pack_heads(o, tq):
    lane = lax.broadcasted_iota(jnp.int32, (tq, 128), 1)
    left = lane < HEAD_DIM
    o00, o01, o10, o11 = [o[h * tq:(h + 1) * tq] for h in range(N_HEADS)]
    out0 = jnp.where(left, o00, pltpu.roll(o01, HEAD_DIM, 1))
    out1 = jnp.where(left, pltpu.roll(o10, HEAD_DIM, 1), o11)
    return jnp.concatenate([out0, out1], axis=-1)


def _gattn_kernel(q_ref, kl_ref, vl_ref, kc_ref, vc_ref, o_ref, m_ref, acc_ref,
                  *, tq, units, n_qb_lat, n_kvb):
    qb = pl.program_id(1)

    def tree(op, xs):
        while len(xs) > 1:
            xs = [op(xs[i], xs[i + 1]) for i in range(0, len(xs) - 1, 2)] + ([xs[-1]] if len(xs) % 2 else [])
        return xs[0]

    def scores(h, kts):
        q = q_ref[h]
        cols = []
        for kt in kts:
            s = _dot(q, kt)
            cols += [s[:, 128 * c:128 * (c + 1)] for c in range(s.shape[1] // 128)]
        return cols

    def softmax(h, cols, first):
        m_blk = jnp.max(tree(jnp.maximum, cols), axis=-1, keepdims=True)
        if first:
            m_new = jnp.broadcast_to(m_blk, (tq, 128))
            alpha = None
        else:
            m_old = m_ref[h]
            m_new = jnp.maximum(m_old, m_blk)
            alpha = jnp.exp(m_old - m_new)
        m_ref[h] = m_new
        return alpha, jnp.concatenate([jnp.exp((c - m_new).astype(BF16)) for c in cols], axis=-1)

    def weighted(h, alpha, p, v):
        pv = _dot(p, v)
        acc_ref[h] = pv if alpha is None else jnp.concatenate([alpha, alpha], axis=-1) * acc_ref[h] + pv

    def block(kts, v, first):
        v = jnp.concatenate([v, jnp.ones_like(v)], axis=-1)
        cols = scores(0, kts)
        for h in range(N_HEADS):
            nxt = scores(h + 1, kts) if h + 1 < N_HEADS else None
            alpha, p = softmax(h, cols, first)
            weighted(h, alpha, p, v)
            cols = nxt

    block([kc_ref[u] for u in range(kc_ref.shape[0])], vc_ref[...], True)

    def body(j, carry):
        rows = pl.ds(pl.multiple_of(j * (units * KT_UNIT), units * KT_UNIT), units * KT_UNIT)
        block([kl_ref[j * units + u] for u in range(units)], vl_ref[rows, :], False)
        return carry

    lax.fori_loop(0, jnp.where(qb < n_qb_lat, n_kvb, 0), body, 0)
    o = [acc_ref[h, :, 0:128] / acc_ref[h, :, 128:256] for h in range(N_HEADS)]
    o_ref[...] = _pack_heads(jnp.concatenate(o, axis=0), tq).astype(BF16)


def _gattn(qp, kt, p, bsz, seq, ctx, with_ctx):
    t_lat = bsz * seq
    tq = ctx
    units = 4
    n_qb_lat = seq // tq
    n_qb = n_qb_lat + (1 if with_ctx else 0)
    cb0 = t_lat // ctx
    t_out = t_lat + (bsz * ctx if with_ctx else 0)
    assert ctx % KT_UNIT == 0 and seq % (units * KT_UNIT) == 0

    def qrow(b, i):
        return jnp.where(i < n_qb_lat, b * n_qb_lat + i, cb0 + b)

    kern = functools.partial(_gattn_kernel, tq=tq, units=units, n_qb_lat=n_qb_lat,
                             n_kvb=seq // (units * KT_UNIT))
    return pl.pallas_call(
        kern,
        out_shape=jax.ShapeDtypeStruct((t_out, MIX_W), BF16),
        grid=(bsz, n_qb),
        in_specs=[pl.BlockSpec((N_HEADS, tq, 128), lambda b, i: (0, qrow(b, i), 0)),
                  pl.BlockSpec((seq // KT_UNIT, 128, KT_UNIT), lambda b, i: (b, 0, 0)),
                  pl.BlockSpec((seq, 128), lambda b, i: (b, C_CV // 128)),
                  pl.BlockSpec((ctx // KT_UNIT, 128, KT_UNIT), lambda b, i: (cb0 + b, 0, 0)),
                  pl.BlockSpec((ctx, 128), lambda b, i: (cb0 + b, C_CV // 128))],
        out_specs=pl.BlockSpec((tq, MIX_W), lambda b, i: (qrow(b, i), 0)),
        scratch_shapes=[pltpu.VMEM((N_HEADS, tq, 128), F32), pltpu.VMEM((N_HEADS, tq, 256), F32)],
        compiler_params=_cp("parallel", "arbitrary"),
        name="global_attn",
    )(qp, kt, p, kt, p)


def _wattn_kernel(q_ref, kl_ref, vl_ref, kc_ref, vc_ref, sink_ref, o_ref, *, nb):
    w = WINDOW
    n = pl.program_id(1)
    is_lat = n < nb
    rows = 2 * w
    qi = lax.broadcasted_iota(jnp.int32, (rows, w), 0) & (w - 1)
    kj = lax.broadcasted_iota(jnp.int32, (rows, w), 1)
    band = (jnp.clip(n - 1, 0, nb - 1), jnp.clip(n, 0, nb - 1), jnp.clip(n + 1, 0, nb - 1))
    off_prev = jnp.where(is_lat & (n >= 1), 0, w)
    off_cur = jnp.where(is_lat, 0, w)
    off_next = jnp.where(n + 1 < nb, 0, w)
    masks = [kj >= qi + off_prev, kj >= off_cur, kj <= qi - off_next]
    kts = [kc_ref[u] for u in range(kc_ref.shape[0])] + [kl_ref[i] for i in band]
    n_ctx_tiles = kc_ref.shape[0]
    v_all = jnp.concatenate([vc_ref[...]] + [vl_ref[pl.ds(pl.multiple_of(i * w, w), w), :] for i in band],
                            axis=0)

    def tree(op, xs):
        while len(xs) > 1:
            xs = [op(xs[i], xs[i + 1]) for i in range(0, len(xs) - 1, 2)] + ([xs[-1]] if len(xs) % 2 else [])
        return xs[0]

    def scores(c):
        q = q_ref[2 * c:2 * c + 2].reshape(rows, 128)
        tiles = [_dot(q, kt) for kt in kts]
        return tiles[:n_ctx_tiles] + [jnp.where(mk, t, NEG_INF) for mk, t in zip(masks, tiles[n_ctx_tiles:])]

    def finish(c, tiles):
        sink = sink_ref[c * rows:(c + 1) * rows, :]
        m = jnp.maximum(jnp.max(tree(jnp.maximum, tiles), axis=-1, keepdims=True), sink)
        ps = [jnp.exp(t - m) for t in tiles]
        den = jnp.sum(tree(jnp.add, ps), axis=-1, keepdims=True) + jnp.exp(sink - m)
        return _dot(jnp.concatenate(ps, axis=-1).astype(BF16), v_all) / den

    tiles = scores(0)
    nxt = scores(1)
    o = [finish(0, tiles), finish(1, nxt)]
    o_ref[...] = _pack_heads(jnp.concatenate(o, axis=0), w).astype(BF16)


def _wattn(qp, k, p, sink, bsz, seq, ctx, with_ctx):
    t_lat = bsz * seq
    w = WINDOW
    nb = seq // w
    ncb = ctx // w
    n_qb = nb + (ncb if with_ctx else 0)
    cq0 = t_lat // w
    cb0 = t_lat // ctx
    t_out = t_lat + (bsz * ctx if with_ctx else 0)
    sink_rows = jnp.broadcast_to(jnp.repeat(sink.astype(F32), w)[:, None], (N_HEADS * w, 128))

    def qrow(b, i):
        return jnp.where(i < nb, b * nb + i, cq0 + b * ncb + (i - nb))

    return pl.pallas_call(
        functools.partial(_wattn_kernel, nb=nb),
        out_shape=jax.ShapeDtypeStruct((t_out, MIX_W), BF16),
        grid=(bsz, n_qb),
        in_specs=[pl.BlockSpec((N_HEADS, w, 128), lambda b, i: (0, qrow(b, i), 0)),
                  pl.BlockSpec((seq // w, 128, w), lambda b, i: (b, 0, 0)),
                  pl.BlockSpec((seq, 128), lambda b, i: (b, C_DV // 128)),
                  pl.BlockSpec((ctx // w, 128, w), lambda b, i: (cb0 + b, 0, 0)),
                  pl.BlockSpec((ctx, 128), lambda b, i: (cb0 + b, C_DV // 128)),
                  pl.BlockSpec((N_HEADS * w, 128), lambda b, i: (0, 0))],
        out_specs=pl.BlockSpec((w, MIX_W), lambda b, i: (qrow(b, i), 0)),
        compiler_params=_cp("parallel", "arbitrary"),
        name="window_attn",
    )(qp, k, p, k, p, sink_rows)


def _merge_kernel(yal_ref, yac_ref, ybl_ref, ybc_ref, yc_ref, yd_ref, gate_ref, x_ref, mod_ref, wbr_ref,
                  wout_ref, g2_ref, xo_ref, h2_ref, *maybe_tok_ref, n_lat_tiles):
    tm, d = x_ref.shape
    ctx_rows = jnp.full((tm, MIX_W), pl.program_id(0), jnp.int32) >= n_lat_tiles
    ya = jnp.where(ctx_rows, yac_ref[...], yal_ref[...])
    yb = jnp.where(ctx_rows, ybc_ref[...], ybl_ref[...])
    acc = None
    for n, y in enumerate((ya, yb, yc_ref[...], yd_ref[...])):
        gate = jax.nn.sigmoid(gate_ref[:, n * d:(n + 1) * d].astype(F32))
        term = gate * _dot(y, wbr_ref[n])
        acc = term if acc is None else acc + term
    x = x_ref[...] + mod_ref[0, 2:3, :] * _dot(acc.astype(BF16), wout_ref[...])
    xo_ref[...] = x
    ms = jnp.mean(x * x, axis=-1, keepdims=True)
    y = x * lax.rsqrt(ms + EPS) * g2_ref[...]
    h2 = y * (1.0 + mod_ref[0, 4:5, :]) + mod_ref[0, 3:4, :]
    h2_ref[...] = h2
    if maybe_tok_ref:
        _to_token_tiles(maybe_tok_ref[0], h2)


def _to_token_tiles(ref, x):
    rows, d = x.shape
    k = d // 128
    for c in range(k):
        ref[pl.ds(c, rows, stride=k), :] = x[:, 128 * c:128 * (c + 1)]


def _from_token_tiles(ref, rows, d):
    k = d // 128
    return jnp.concatenate([ref[pl.ds(c, rows, stride=k), :] for c in range(k)], axis=-1)


def _merge(ys, p, x, mod, wbr, wout, g2, t_out, t_lat, seq, tm, token_tiles):
    d = x.shape[1]
    k = d // 128

    def row(width):
        return pl.BlockSpec((tm, width), lambda i: (i, 0))

    def const(shape):
        return pl.BlockSpec(shape, lambda i: (0,) * len(shape))

    out_shape = [jax.ShapeDtypeStruct((t_out, d), F32), jax.ShapeDtypeStruct((t_out, d), F32)]
    out_specs = [row(d), row(d)]
    if token_tiles:
        out_shape.append(jax.ShapeDtypeStruct((t_out * k, 128), F32))
        out_specs.append(pl.BlockSpec((tm * k, 128), lambda i: (i, 0)))
    (ya_l, ya_c), (yb_l, yb_c), yc, yd = ys
    n_lat_tiles = t_lat // tm
    n_ctx_tiles = ya_c.shape[0] // tm
    lat = pl.BlockSpec((tm, MIX_W), lambda i: (jnp.minimum(i, n_lat_tiles - 1), 0))
    cx = pl.BlockSpec((tm, MIX_W), lambda i: (jnp.clip(i - n_lat_tiles, 0, n_ctx_tiles - 1), 0))
    return pl.pallas_call(
        functools.partial(_merge_kernel, n_lat_tiles=n_lat_tiles),
        out_shape=tuple(out_shape),
        grid=(t_out // tm,),
        in_specs=[lat, cx, lat, cx, row(MIX_W), row(MIX_W), row(4 * d), row(d),
                  pl.BlockSpec((1, 6, d), lambda i: (_mod_group(i * tm, t_lat, seq), 0, 0)),
                  const((4, MIX_W, d)), const((d, d)), const((1, d))],
        out_specs=tuple(out_specs),
        compiler_params=_cp("parallel"),
        name="merge",
    )(ya_l, ya_c, yb_l, yb_c, yc, yd, p, x, mod, wbr, wout, g2.reshape(1, d))


def _ffn_dense_kernel(h_ref, x_ref, mod_ref, wg_ref, wu_ref, wo_ref, o_ref, hb_ref, acc_ref):
    j = pl.program_id(1)

    @pl.when(j == 0)
    def _():
        hb_ref[...] = h_ref[...].astype(BF16)
        acc_ref[...] = jnp.zeros_like(acc_ref)

    h = hb_ref[...]
    a = _silu(_dot(h, wg_ref[...])) * _dot(h, wu_ref[...])
    acc_ref[...] += _dot(a.astype(BF16), wo_ref[...])

    @pl.when(j == pl.num_programs(1) - 1)
    def _():
        o_ref[...] = x_ref[...] + mod_ref[0, 5:6, :] * acc_ref[...]


def _ffn_dense(h2, x, mod, w_in, w_out, t_lat, seq, tm):
    t, d = x.shape
    f = w_out.shape[0]
    tf = 256
    nf = f // tf
    return pl.pallas_call(
        _ffn_dense_kernel,
        out_shape=jax.ShapeDtypeStruct((t, d), F32),
        grid=(t // tm, nf),
        in_specs=[pl.BlockSpec((tm, d), lambda i, j: (i, 0)),
                  pl.BlockSpec((tm, d), lambda i, j: (i, 0)),
                  pl.BlockSpec((1, 6, d), lambda i, j: (_mod_group(i * tm, t_lat, seq), 0, 0)),
                  pl.BlockSpec((d, tf), lambda i, j: (0, j)),
                  pl.BlockSpec((d, tf), lambda i, j: (0, nf + j)),
                  pl.BlockSpec((tf, d), lambda i, j: (j, 0))],
        out_specs=pl.BlockSpec((tm, d), lambda i, j: (i, 0)),
        scratch_shapes=[pltpu.VMEM((tm, d), BF16), pltpu.VMEM((tm, d), F32)],
        compiler_params=_cp("parallel", "arbitrary"),
        name="ffn_dense",
    )(h2, x, mod, w_in, w_in, w_out)


def _router_kernel(h_ref, w_ref, e_ref, g1_ref, g2_ref):
    h = h_ref[...]
    h_hi = h.astype(BF16)
    h_lo = (h - h_hi.astype(F32)).astype(BF16)
    logits = _dot(h_hi, w_ref[0]) + _dot(h_lo, w_ref[0]) + _dot(h_hi, w_ref[1])
    lane = lax.broadcasted_iota(jnp.int32, logits.shape, 1)
    lane_f = lane.astype(F32)
    logits = jnp.where(lane < N_EXPERTS, logits, -jnp.inf)
    m1 = jnp.max(logits, axis=-1, keepdims=True)
    i1 = jnp.min(jnp.where(logits == m1, lane_f, 128.0), axis=-1, keepdims=True)
    rest = jnp.where(lane_f == i1, -jnp.inf, logits)
    m2 = jnp.max(rest, axis=-1, keepdims=True)
    i2 = jnp.min(jnp.where(rest == m2, lane_f, 128.0), axis=-1, keepdims=True)
    e2 = jnp.exp(m2 - m1)
    g1 = 1.0 / (1.0 + e2)
    e_ref[...] = jnp.where(lane == 0, i1, jnp.where(lane == 1, i2, 0.0)).astype(jnp.int32)
    g1_ref[...] = jnp.broadcast_to(g1, g1_ref.shape)
    g2_ref[...] = jnp.broadcast_to(e2 * g1, g2_ref.shape)


def _router(h2, router, tm):
    t, d = h2.shape
    r = jnp.zeros((d, 128), F32).at[:, :N_EXPERTS].set(router.astype(F32))
    r_hi = r.astype(BF16)
    r_lo = (r - r_hi.astype(F32)).astype(BF16)
    shp = jax.ShapeDtypeStruct((t, 128), F32)
    return pl.pallas_call(
        _router_kernel,
        out_shape=(jax.ShapeDtypeStruct((t, 128), jnp.int32), shp, shp),
        grid=(t // tm,),
        in_specs=[pl.BlockSpec((tm, d), lambda i: (i, 0)),
                  pl.BlockSpec((2, d, 128), lambda i: (0, 0, 0))],
        out_specs=(pl.BlockSpec((tm, 128), lambda i: (i, 0)),) * 3,
        compiler_params=_cp("parallel"),
        name="router",
    )(h2, jnp.stack([r_hi, r_lo]))


def _experts_kernel(te_ref, tok0_ref, tokn_ref, dst_ref, h_ref, wg_ref, wu_ref, wo_ref, y_ref,
                    xbuf, ybuf, xb_ref, acc_ref, sem_in, sem_out, *, n_tiles, tm):
    i, j = pl.program_id(0), pl.program_id(1)
    last_j = pl.num_programs(1) - 1
    slot = i % 2
    other = 1 - slot
    per_step = tm // MOE_STEPS
    tr = TOK_ROWS

    def tok(ref, t):
        return ref.at[pl.ds(pl.multiple_of(t * tr, tr), tr)]

    def gather(idx_ref, r, s):
        return pltpu.make_async_copy(tok(h_ref, idx_ref[0, 0, r]), tok(xbuf.at[s], r), sem_in.at[s])

    def scatter(r, s):
        return pltpu.make_async_copy(tok(ybuf.at[s], r), tok(y_ref, dst_ref[0, 0, r]), sem_out.at[s])

    def wait_tile(copy_of_row0):
        def body(k, carry):
            for _ in range(MOE_WAITS):
                copy_of_row0().wait()
            return carry

        lax.fori_loop(0, tm // MOE_WAITS, body, 0)

    def wait_gather(s):
        wait_tile(lambda: pltpu.make_async_copy(tok(h_ref, 0), tok(xbuf.at[s], 0), sem_in.at[s]))

    def wait_scatter(s):
        wait_tile(lambda: pltpu.make_async_copy(tok(ybuf.at[s], 0), tok(y_ref, 0), sem_out.at[s]))

    def row_traffic(with_gather, with_scatter):
        def rows(base, count):
            for k in range(count):
                if with_gather:
                    gather(tokn_ref, base + k, other).start(priority=1)
                if with_scatter:
                    scatter(base + k, other).start(priority=1)

        rows(j * per_step, per_step)

        @pl.when(j == last_j)
        def _():
            rows(per_step * MOE_STEPS, tm - per_step * MOE_STEPS)

    def compute():
        x = xb_ref[...]
        a = _silu(_dot(x, wg_ref[0])) * _dot(x, wu_ref[0])
        acc_ref[...] += _dot(a.astype(BF16), wo_ref[0])

    @pl.when((i == 0) & (j == 0))
    def _():
        def body(r, carry):
            gather(tok0_ref, r, 0).start()
            return carry

        lax.fori_loop(0, tm, body, 0)

    @pl.when(j == 0)
    def _():
        wait_gather(slot)

        @pl.when(i < n_tiles)
        def _():
            xb_ref[...] = _from_token_tiles(xbuf.at[slot], tm, xb_ref.shape[1]).astype(BF16)
            acc_ref[...] = jnp.zeros_like(acc_ref)

    @pl.when(i == 0)
    def _():
        compute()
        row_traffic(True, False)

    @pl.when((i > 0) & (i < n_tiles))
    def _():
        compute()
        row_traffic(True, True)

    @pl.when(i == n_tiles)
    def _():
        row_traffic(False, True)

    @pl.when(j == last_j)
    def _():
        @pl.when((i >= 2) & (i < n_tiles))
        def _():
            wait_scatter(slot)

        @pl.when(i < n_tiles)
        def _():
            _to_token_tiles(ybuf.at[slot], acc_ref[...])

        @pl.when(i == n_tiles)
        def _():
            wait_scatter(slot)
            wait_scatter(other)


def _experts(h2_tiles, src_tok, dst_row, tile_e, w_in, w_out, n_out_rows):
    d = w_in.shape[1]
    assert d == TOK_ROWS * 128
    n_tiles = dst_row.shape[0]
    f = w_out.shape[1]
    tf = f // MOE_STEPS
    tm = MOE_TILE
    assert n_tiles >= 2 and tm % MOE_WAITS == 0 and tf % 128 == 0

    def smem(index_map):
        return pl.BlockSpec((1, 1, tm), index_map, memory_space=pltpu.SMEM)

    grid_spec = pltpu.PrefetchScalarGridSpec(
        num_scalar_prefetch=1,
        grid=(n_tiles + 1, MOE_STEPS),
        in_specs=[smem(lambda i, j, te: (0, 0, 0)),
                  smem(lambda i, j, te: (jnp.minimum(i + 1, n_tiles), 0, 0)),
                  smem(lambda i, j, te: (jnp.clip(i - 1, 0, n_tiles - 1), 0, 0)),
                  pl.BlockSpec(memory_space=pl.ANY),
                  pl.BlockSpec((1, d, tf), lambda i, j, te: (te[i], 0, j)),
                  pl.BlockSpec((1, d, tf), lambda i, j, te: (te[i], 0, MOE_STEPS + j)),
                  pl.BlockSpec((1, tf, d), lambda i, j, te: (te[i], j, 0))],
        out_specs=pl.BlockSpec(memory_space=pl.ANY),
        scratch_shapes=[pltpu.VMEM((2, tm * TOK_ROWS, 128), F32), pltpu.VMEM((2, tm * TOK_ROWS, 128), F32),
                        pltpu.VMEM((tm, d), BF16), pltpu.VMEM((tm, d), F32),
                        pltpu.SemaphoreType.DMA((2,)), pltpu.SemaphoreType.DMA((2,))])
    return pl.pallas_call(
        functools.partial(_experts_kernel, n_tiles=n_tiles, tm=tm),
        out_shape=jax.ShapeDtypeStruct((n_out_rows * TOK_ROWS, 128), F32),
        grid_spec=grid_spec,
        compiler_params=_cp("arbitrary", "arbitrary"),
        name="experts",
    )(tile_e, src_tok, src_tok, dst_row, h2_tiles, w_in, w_in, w_out)


def _combine_kernel(y1_ref, y2_ref, x_ref, mod_ref, g1_ref, g2_ref, *rest):
    o_ref = rest[-1]
    tm, d = x_ref.shape
    g1 = jnp.concatenate([g1_ref[...]] * (d // 128), axis=-1)
    g2 = jnp.concatenate([g2_ref[...]] * (d // 128), axis=-1)
    y = g1 * _from_token_tiles(y1_ref, tm, d) + g2 * _from_token_tiles(y2_ref, tm, d)
    x = x_ref[...] + mod_ref[0, 5:6, :] * y
    if len(rest) == 2:
        ms = jnp.mean(x * x, axis=-1, keepdims=True)
        x = x * lax.rsqrt(ms + EPS) * rest[0][...]
    o_ref[...] = x


def _combine(y_tiles, x, mod, g1, g2, t_lat, seq, tm, final_g):
    t, d = x.shape
    k = d // 128
    extra_specs, extra_args = [], []
    if final_g is not None:
        extra_specs, extra_args = [pl.BlockSpec((1, d), lambda i: (0, 0))], [final_g.reshape(1, d)]
    return pl.pallas_call(
        _combine_kernel,
        out_shape=jax.ShapeDtypeStruct((t, d), F32),
        grid=(t // tm,),
        in_specs=[pl.BlockSpec((tm * k, 128), lambda i: (i, 0)),
                  pl.BlockSpec((tm * k, 128), lambda i: (t // tm + i, 0)),
                  pl.BlockSpec((tm, d), lambda i: (i, 0)),
                  pl.BlockSpec((1, 6, d), lambda i: (_mod_group(i * tm, t_lat, seq), 0, 0)),
                  pl.BlockSpec((tm, 128), lambda i: (i, 0)),
                  pl.BlockSpec((tm, 128), lambda i: (i, 0))] + extra_specs,
        out_specs=pl.BlockSpec((tm, d), lambda i: (i, 0)),
        compiler_params=_cp("parallel"),
        name="moe_combine",
    )(y_tiles, y_tiles, x, mod, g1, g2, *extra_args)


def _moe(h2, h2_tiles, x, mod, router, w_in, w_out, t_lat, seq, tm, final_g=None):
    t, d = h2.shape
    e_idx, g1, g2 = _router(h2, router, tm)
    e_flat = e_idx[:, :2].reshape(-1)
    onehot = (e_flat[:, None] == jnp.arange(N_EXPERTS, dtype=jnp.int32)[None, :]).astype(jnp.int32)
    csum = jnp.cumsum(onehot, axis=0)
    counts = csum[-1]
    padded = (counts + MOE_TILE - 1) // MOE_TILE * MOE_TILE
    ends = jnp.cumsum(padded)
    pstarts = ends - padded
    dest = jnp.sum(onehot * (pstarts[None, :] + csum - 1), axis=1).astype(jnp.int32)
    n_tiles = (2 * t + MOE_TILE - 1) // MOE_TILE + N_EXPERTS
    n_rows = n_tiles * MOE_TILE
    slot_a = jnp.full((n_rows,), -1, jnp.int32).at[dest].set(jnp.arange(2 * t, dtype=jnp.int32))
    is_pad = slot_a < 0
    slot_row = jnp.where(is_pad, 2 * t - 1 + jnp.cumsum(is_pad.astype(jnp.int32)),
                         (slot_a % 2) * t + slot_a // 2)
    src_tok = jnp.where(is_pad, 0, slot_a // 2)
    src_tok = jnp.concatenate([src_tok, jnp.zeros((MOE_TILE,), jnp.int32)]).reshape(n_tiles + 1, 1, MOE_TILE)
    tile_start = jnp.arange(n_tiles + 1, dtype=jnp.int32) * MOE_TILE
    tile_e = jnp.minimum(jnp.searchsorted(ends, tile_start, side='right'), N_EXPERTS - 1).astype(jnp.int32)

    y = _experts(h2_tiles, src_tok, slot_row.reshape(n_tiles, 1, MOE_TILE), tile_e, w_in, w_out, n_rows)
    return _combine(y, x, mod, g1, g2, t_lat, seq, tm, final_g)


def _final_norm_kernel(x_ref, g_ref, o_ref):
    x = x_ref[...]
    ms = jnp.mean(x * x, axis=-1, keepdims=True)
    o_ref[...] = x * lax.rsqrt(ms + EPS) * g_ref[...]


def _final_norm(x, g, tm):
    t, d = x.shape
    return pl.pallas_call(
        _final_norm_kernel,
        out_shape=jax.ShapeDtypeStruct((t, d), F32),
        grid=(t // tm,),
        in_specs=[pl.BlockSpec((tm, d), lambda i: (i, 0)), pl.BlockSpec((1, d), lambda i: (0, 0))],
        out_specs=pl.BlockSpec((tm, d), lambda i: (i, 0)),
        compiler_params=_cp("parallel"),
        name="final_norm",
    )(x, g.reshape(1, d))


def _proj_weights(w_in):
    o = {}
    acc = 0
    for name, size in (('a_z', 256), ('a_x', 256), ('a_b', 256), ('a_c', 256), ('a_dt', 8), ('b_u', 256),
                       ('c_q', 256), ('c_k', 128), ('c_v', 128), ('d_q', 256), ('d_k', 128), ('d_v', 128),
                       ('gates', 4096)):
        o[name] = (acc, size)
        acc += size
    order = ('gates', 'a_z', 'b_u', 'c_q', 'd_q', 'c_k', 'c_v', 'd_k', 'd_v', 'a_x', 'a_b', 'a_c')
    w = jnp.concatenate([w_in[:, :, o[n][0]:o[n][0] + o[n][1]] for n in order], axis=-1).astype(BF16)
    dt0 = o['a_dt'][0]
    wdt = jnp.pad(w_in[:, :, dt0:dt0 + 8], ((0, 0), (0, 0), (0, 120))).astype(BF16)
    return w, wdt


def kernel(x, c, ctx, c_ctx, norm1_g, norm2_g, ada_w, ada_b, w_in, ssd_conv_w, ssd_conv_b, ssd_a_log,
           ssd_dt_bias, ssd_d, ssd_norm_g, s5_lam_re, s5_lam_im, s5_log_step, s5_b_re, s5_b_im, s5_c_re,
           s5_c_im, s5_d, s5_glu_w, qk_norm_g, swa_sink, w_branch, w_out, ffn_w_in, ffn_w_out, moe_router,
           moe_w_in, moe_w_out, final_norm_g):
    bsz, seq, d = x.shape
    n_ctx = ctx.shape[1]
    depth = w_in.shape[0]
    t_lat, t_ctx = bsz * seq, bsz * n_ctx
    tm = _pow2_tile(1024, seq, t_ctx)
    tm_small = _pow2_tile(512, seq, t_ctx)

    cvec = jnp.zeros((16, d), F32).at[0].set(c_ctx).at[1:1 + bsz].set(c)
    mod = _adaln(cvec, ada_w, ada_b).reshape(depth, 16, 6, d)
    wp, wdt = _proj_weights(w_in)
    wbr = w_branch.astype(BF16)
    wo = w_out.astype(BF16)
    ffn_in, ffn_out = ffn_w_in.astype(BF16), ffn_w_out.astype(BF16)
    moe_in, moe_out = moe_w_in.astype(BF16), moe_w_out.astype(BF16)

    xx = jnp.concatenate([x.reshape(t_lat, d), ctx.reshape(t_ctx, d)], axis=0)
    for l in range(depth):
        with_ctx = l < depth - 1
        t_out = t_lat + (t_ctx if with_ctx else 0)
        p, pdt = _inproj(xx, norm1_g[l], mod[l], wp[l], wdt[l], t_lat, seq, tm)
        ya = _ssd(p, pdt, ssd_conv_w[l], ssd_conv_b[l], ssd_a_log[l], ssd_dt_bias[l], ssd_d[l],
                  ssd_norm_g[l], bsz, seq, n_ctx)
        yb = _s5(p, s5_lam_re[l], s5_lam_im[l], s5_log_step[l], s5_b_re[l], s5_b_im[l], s5_c_re[l],
                 s5_c_im[l], s5_d[l], s5_glu_w[l], bsz, seq, n_ctx)
        q1, q2, k1, k2 = _prep(p, qk_norm_g[l], t_lat, seq, tm)
        yc = _gattn(q1, k1, p, bsz, seq, n_ctx, with_ctx)
        yd = _wattn(q2, k2, p, swa_sink[l], bsz, seq, n_ctx, with_ctx)
        routed = l % 2 == 1
        xx, h2, *tiles = _merge((ya, yb, yc, yd), p, xx, mod[l], wbr[l], wo[l], norm2_g[l], t_out, t_lat, seq,
                                tm_small, routed)
        if routed:
            xx = _moe(h2, tiles[0], xx, mod[l], moe_router[l // 2], moe_in[l // 2], moe_out[l // 2], t_lat,
                      seq, tm_small, None if with_ctx else final_norm_g)
        else:
            xx = _ffn_dense(h2, xx, mod[l], ffn_in[l // 2], ffn_out[l // 2], t_lat, seq, tm)
    if depth % 2 == 1:
        xx = _final_norm(xx[:t_lat], final_norm_g, tm)
    return xx.reshape(bsz, seq, d)
```

```python
import functools

import numpy as np
import jax
import jax.numpy as jnp
from jax import lax
from jax.experimental import pallas as pl
from jax.experimental.pallas import tpu as pltpu

F32 = jnp.float32
BF16 = jnp.bfloat16

EPS = 1e-6
NEG_INF = -1e30
GRID_W = 64
MIX_W = 256
HEAD_DIM = 64
N_HEADS = 4
ATTN_SCALE = HEAD_DIM ** -0.5
SSD_STATE = 128
SSD_CONV = 5
CHUNK = 128
HALO = 16
S5_GROUPS = 16
S5_GROUP = 16
S5_STATE = 64
S5_LANES = S5_GROUPS * S5_STATE
S5_STEPS = 64
ROPE_BASE = 10000.0
ROPE_FREQS = 16
WINDOW = 128
KT_UNIT = 256
N_EXPERTS = 8
MOE_TILE = 512
MOE_STEPS = 4
MOE_WAITS = 64
TOK_ROWS = 8
VMEM_LIMIT = 56 * 1024 * 1024

C_GATES, C_Z, C_BU, C_CQ, C_DQ = 0, 4096, 4352, 4608, 4864
C_CK, C_CV, C_DK, C_DV = 5120, 5248, 5376, 5504
C_AX, C_AB, C_AC = 5632, 5888, 6144
P_COLS = 6400
PROJ_TN = 1280


def _cp(*sem):
    return pltpu.CompilerParams(dimension_semantics=sem, vmem_limit_bytes=VMEM_LIMIT)


def _pow2_tile(cap, *dims):
    t = 1
    while t * 2 <= cap and all(d % (t * 2) == 0 for d in dims):
        t *= 2
    return t


def _dot(a, b):
    return jnp.dot(a, b, preferred_element_type=F32)


def _dot_nt(a, b):
    return lax.dot_general(a, b, (((1,), (1,)), ((), ())), preferred_element_type=F32)


def _dot_tn(a, b):
    return lax.dot_general(a, b, (((0,), (0,)), ((), ())), preferred_element_type=F32)


def _split3(x):
    hi = x.astype(BF16)
    r1 = x - hi.astype(F32)
    mid = r1.astype(BF16)
    lo = (r1 - mid.astype(F32)).astype(BF16)
    return hi, mid, lo


def _silu(x):
    return x * jax.nn.sigmoid(x)


def _adaln_kernel(c_ref, w_ref, b_ref, o_ref):
    c = c_ref[...]
    o_ref[0] = jnp.dot(_silu(c), w_ref[0], preferred_element_type=F32,
                       precision=lax.Precision.HIGHEST) + b_ref[0]


def _adaln(cvec, ada_w, ada_b):
    depth, d, n = ada_w.shape
    tn = 1024
    return pl.pallas_call(
        _adaln_kernel,
        out_shape=jax.ShapeDtypeStruct((depth, 16, n), F32),
        grid=(depth, n // tn),
        in_specs=[pl.BlockSpec((16, d), lambda l, j: (0, 0)),
                  pl.BlockSpec((1, d, tn), lambda l, j: (l, 0, j)),
                  pl.BlockSpec((1, 1, tn), lambda l, j: (l, 0, j))],
        out_specs=pl.BlockSpec((1, 16, tn), lambda l, j: (l, 0, j)),
        compiler_params=_cp("parallel", "parallel"),
        name="adaln",
    )(cvec, ada_w, ada_b.reshape(depth, 1, n))


def _mod_group(row0, t_lat, seq):
    return jnp.where(row0 >= t_lat, 0, 1 + row0 // seq)


def _inproj_kernel(x_ref, g_ref, mod_ref, w_ref, wdt_ref, o_ref, odt_ref, h_ref):
    @pl.when(pl.program_id(1) == 0)
    def _():
        x = x_ref[...]
        ms = jnp.mean(x * x, axis=-1, keepdims=True)
        y = x * lax.rsqrt(ms + EPS) * g_ref[...]
        h = (y * (1.0 + mod_ref[0, 1:2, :]) + mod_ref[0, 0:1, :]).astype(BF16)
        h_ref[...] = h
        odt_ref[...] = _dot(h, wdt_ref[...])

    o_ref[...] = _dot(h_ref[...], w_ref[...]).astype(BF16)


def _inproj(x, g, mod, w, wdt, t_lat, seq, tm):
    t, d = x.shape
    n = w.shape[1]
    tn = PROJ_TN
    return pl.pallas_call(
        _inproj_kernel,
        out_shape=(jax.ShapeDtypeStruct((t, n), BF16), jax.ShapeDtypeStruct((t, 128), F32)),
        grid=(t // tm, n // tn),
        in_specs=[pl.BlockSpec((tm, d), lambda i, j: (i, 0)),
                  pl.BlockSpec((1, d), lambda i, j: (0, 0)),
                  pl.BlockSpec((1, 6, d), lambda i, j: (_mod_group(i * tm, t_lat, seq), 0, 0)),
                  pl.BlockSpec((d, tn), lambda i, j: (0, j)),
                  pl.BlockSpec((d, 128), lambda i, j: (0, 0))],
        out_specs=(pl.BlockSpec((tm, tn), lambda i, j: (i, j)),
                   pl.BlockSpec((tm, 128), lambda i, j: (i, 0))),
        scratch_shapes=[pltpu.VMEM((tm, d), BF16)],
        compiler_params=_cp("parallel", "arbitrary"),
        name="inproj",
    )(x, g.reshape(1, d), mod, w, wdt)


def _ssd_consts():
    r = np.arange(CHUNK)
    tri_l = (r[None, :] <= r[:, None]).astype(np.float32)
    tri_u = tri_l.T.copy()
    shifts = np.zeros((4, CHUNK, CHUNK + 2 * HALO), np.float32)
    for n, k in enumerate((0, 1, 3, 4)):
        shifts[n, r, r + HALO + k - 2] = 1.0
    spread = np.zeros((2, 128, MIX_W), np.float32)
    for d in range(2):
        for h in range(N_HEADS):
            spread[d, N_HEADS * d + h, HEAD_DIM * h:HEAD_DIM * (h + 1)] = 1.0
    return (jnp.asarray(np.stack([tri_l, tri_u]), BF16), jnp.asarray(shifts, BF16),
            jnp.asarray(spread, BF16))


def _ssd_kernel(zl_ref, xl_ref, bl_ref, cl_ref, dtl_ref, zc_ref, xc_ref, bc_ref, cc_ref, dtc_ref,
                cw_ref, cb_ref, an_ref, bias_ref, dsk_ref, ng_ref, tri_ref, sh_ref, e_ref,
                yl_ref, yc_ref, act_l, act_c, yf_l, yf_c, stf_ref, stb_ref, *, n_lat, n_ctx):
    q = CHUNK
    lane128 = lax.broadcasted_iota(jnp.int32, (q, 128), 1)
    lane256 = lax.broadcasted_iota(jnp.int32, (q, MIX_W), 1)
    lane256r = lax.broadcasted_iota(jnp.int32, (1, MIX_W), 1)
    row_i = lax.broadcasted_iota(jnp.int32, (q, q), 0)
    col_i = lax.broadcasted_iota(jnp.int32, (q, q), 1)
    head_masks = [(lane256 >= HEAD_DIM * h) & (lane256 < HEAD_DIM * (h + 1)) for h in range(N_HEADS)]
    head_masks_r = [(lane256r >= HEAD_DIM * h) & (lane256r < HEAD_DIM * (h + 1)) for h in range(N_HEADS)]
    group_masks = [lane256 < 128, lane256 >= 128]
    lane_state = lax.broadcasted_iota(jnp.int32, (SSD_STATE, MIX_W), 1)
    state_group_masks = [lane_state < 128, lane_state >= 128]

    def conv_act(x_ref, b_ref, c_ref, n_chunks, c):
        def rows(ref, start, size):
            return ref[pl.ds(start, size), :]

        start = c * q
        if isinstance(c, int):
            p0, n0 = max(start - HALO, 0), min(start + q, n_chunks * q - HALO)
            pf, nf = float(c > 0), float(c < n_chunks - 1)
        else:
            start = pl.multiple_of(start, q)
            p0 = pl.multiple_of(jnp.maximum(start - HALO, 0), HALO)
            n0 = pl.multiple_of(jnp.minimum(start + q, n_chunks * q - HALO), HALO)
            pf, nf = (c > 0).astype(F32), (c < n_chunks - 1).astype(F32)
        parts = []
        for ref in (x_ref, b_ref, c_ref):
            prev = (rows(ref, p0, HALO).astype(F32) * pf).astype(BF16)
            nxt = (rows(ref, n0, HALO).astype(F32) * nf).astype(BF16)
            parts.append(jnp.concatenate([prev, rows(ref, start, q), nxt], axis=0))
        ext = jnp.concatenate(parts, axis=1)
        cur = ext[HALO:HALO + q].astype(F32)
        acc = cur * cw_ref[2:3, :] + cb_ref[...]
        for n, k in enumerate((0, 1, 3, 4)):
            acc = acc + _dot(sh_ref[n], ext) * cw_ref[k:k + 1, :]
        return _silu(acc)

    def conv_chunk(seg, c):
        _, x_ref, b_ref, c_ref, _, act_ref, _, _, n_chunks = seg
        start = c * q if isinstance(c, int) else pl.multiple_of(c * q, q)
        act_ref[pl.ds(start, q), :] = conv_act(x_ref, b_ref, c_ref, n_chunks, c).astype(BF16)

    def chunk(seg, c, d, second):
        z_ref, x_ref, b_ref, c_ref, dt_ref, act_ref, yf_ref, y_ref, n_chunks = seg
        st_ref = st_refs[d]
        start = c * q if isinstance(c, int) else pl.multiple_of(c * q, q)
        act = act_ref[pl.ds(start, q), :]
        xs = act[:, 0:MIX_W].astype(F32)
        bm = act[:, MIX_W:2 * MIX_W]
        cm = act[:, 2 * MIX_W:3 * MIX_W]

        dt_n = jax.nn.softplus(dt_ref[pl.ds(start, q), :] + bias_ref[...])
        la_n = dt_n * an_ref[...]
        hi, mid, lo = _split3(la_n)
        tri = tri_ref[d]
        cs_n = _dot(tri, hi) + _dot(tri, mid) + _dot(tri, lo)
        cs_t = cs_n.T
        edge = q - 1 if d == 0 else 0
        tri_mask = (col_i <= row_i) if d == 0 else (col_i >= row_i)

        tot_n = cs_n[edge:edge + 1, :]
        spread = e_ref[d]
        dt_full = _dot(dt_n.astype(BF16), spread)
        dtdte_full = _dot((dt_n * jnp.exp(tot_n - cs_n)).astype(BF16), spread)
        ecs_full = _dot(jnp.exp(cs_n).astype(BF16), spread)
        etot_n = jnp.exp(tot_n)
        tot_full = jnp.zeros((1, MIX_W), F32)
        decay = []
        for h in range(N_HEADS):
            sel = lane128 == (N_HEADS * d + h)
            cs_col = jnp.sum(jnp.where(sel, cs_n, 0.0), axis=-1, keepdims=True)
            cs_row = cs_t[N_HEADS * d + h:N_HEADS * d + h + 1, :]
            decay.append(jnp.where(tri_mask, jnp.exp(cs_col - cs_row), 0.0))
            etot = jnp.sum(jnp.where(sel[0:1, :], etot_n, 0.0), axis=-1, keepdims=True)
            tot_full = jnp.where(head_masks_r[h], etot, tot_full)

        xdt = xs * dt_full
        state = st_ref[...]
        y = jnp.zeros((q, MIX_W), F32)
        y_off = jnp.zeros((q, MIX_W), F32)
        upd = jnp.zeros((SSD_STATE, MIX_W), F32)
        xdte = xs * dtdte_full
        for g in range(2):
            bg = bm[:, 128 * g:128 * (g + 1)]
            cg = cm[:, 128 * g:128 * (g + 1)]
            cb = _dot_nt(cg, bg)
            for h in (2 * g, 2 * g + 1):
                m = (cb * decay[h]).astype(BF16)
                y = y + _dot(m, jnp.where(head_masks[h], xdt, 0.0).astype(BF16))
            y_off = y_off + _dot(cg, jnp.where(state_group_masks[g], state, 0.0).astype(BF16))
            upd = upd + _dot_tn(bg, jnp.where(group_masks[g], xdte, 0.0).astype(BF16))
        y = y + y_off * ecs_full
        st_ref[...] = state * tot_full + upd

        if not second:
            yf_ref[pl.ds(start, q), :] = y
        else:
            y = y + yf_ref[pl.ds(start, q), :] + dsk_ref[...] * xs
            y = y * _silu(z_ref[pl.ds(start, q), :].astype(F32))
            ms = jnp.mean(y * y, axis=-1, keepdims=True)
            y_ref[pl.ds(start, q), :] = (y * lax.rsqrt(ms + EPS) * ng_ref[...]).astype(BF16)

    seg_l = (zl_ref, xl_ref, bl_ref, cl_ref, dtl_ref, act_l, yf_l, yl_ref, n_lat)
    seg_c = (zc_ref, xc_ref, bc_ref, cc_ref, dtc_ref, act_c, yf_c, yc_ref, n_ctx)
    st_refs = (stf_ref, stb_ref)
    stf_ref[...] = jnp.zeros_like(stf_ref)
    stb_ref[...] = jnp.zeros_like(stb_ref)

    for c in range(n_ctx):
        conv_chunk(seg_c, c)

    def conv_body(i, carry):
        conv_chunk(seg_l, i)
        return carry

    lax.fori_loop(0, n_lat, conv_body, 0)

    for i in range(n_ctx):
        chunk(seg_c, i, 0, 2 * i >= n_ctx)
        chunk(seg_c, n_ctx - 1 - i, 1, 2 * i >= n_ctx - 1)

    def pair(second):
        def body(i, carry):
            chunk(seg_l, i, 0, second)
            chunk(seg_l, n_lat - 1 - i, 1, second)
            return carry
        return body

    lax.fori_loop(0, n_lat // 2, pair(False), 0, unroll=2)
    lax.fori_loop(n_lat // 2, n_lat, pair(True), 0, unroll=2)


def _ssd(p, pdt, conv_w, conv_b, a_log, dt_bias, d_skip, norm_g, bsz, seq, ctx):
    t = p.shape[0]
    n_lat, n_ctx = seq // CHUNK, ctx // CHUNK
    assert n_lat % 2 == 0
    cb0 = (bsz * seq) // ctx
    tri, shifts, spread = _ssd_consts()
    cw = jnp.zeros((8, 3 * MIX_W), F32).at[:SSD_CONV].set(conv_w)
    a_n = jnp.zeros((1, 128), F32).at[0, :8].set(-jnp.exp(a_log.astype(F32)).reshape(8))
    bias_n = jnp.zeros((1, 128), F32).at[0, :8].set(dt_bias.astype(F32).reshape(8))
    dsk = jnp.repeat(d_skip.astype(F32), HEAD_DIM).reshape(1, MIX_W)

    def lat(col):
        return pl.BlockSpec((seq, MIX_W), lambda b, col=col: (b, col // MIX_W))

    def cx(col):
        return pl.BlockSpec((ctx, MIX_W), lambda b, col=col: (cb0 + b, col // MIX_W))

    def full(shape):
        return pl.BlockSpec(shape, lambda b: (0,) * len(shape))

    kern = functools.partial(_ssd_kernel, n_lat=n_lat, n_ctx=n_ctx)
    yl, yc = pl.pallas_call(
        kern,
        out_shape=(jax.ShapeDtypeStruct((bsz * seq, MIX_W), BF16),
                   jax.ShapeDtypeStruct((bsz * ctx, MIX_W), BF16)),
        grid=(bsz,),
        in_specs=[lat(C_Z), lat(C_AX), lat(C_AB), lat(C_AC),
                  pl.BlockSpec((seq, 128), lambda b: (b, 0)),
                  cx(C_Z), cx(C_AX), cx(C_AB), cx(C_AC),
                  pl.BlockSpec((ctx, 128), lambda b: (cb0 + b, 0)),
                  full((8, 3 * MIX_W)), full((1, 3 * MIX_W)), full((1, 128)), full((1, 128)),
                  full((1, MIX_W)), full((1, MIX_W)), full((2, CHUNK, CHUNK)),
                  full((4, CHUNK, CHUNK + 2 * HALO)), full((2, 128, MIX_W))],
        out_specs=(pl.BlockSpec((seq, MIX_W), lambda b: (b, 0)),
                   pl.BlockSpec((ctx, MIX_W), lambda b: (b, 0))),
        scratch_shapes=[pltpu.VMEM((seq, 3 * MIX_W), BF16), pltpu.VMEM((ctx, 3 * MIX_W), BF16),
                        pltpu.VMEM((seq, MIX_W), F32), pltpu.VMEM((ctx, MIX_W), F32),
                        pltpu.VMEM((SSD_STATE, MIX_W), F32), pltpu.VMEM((SSD_STATE, MIX_W), F32)],
        compiler_params=_cp("parallel"),
        name="ssd",
    )(p, p, p, p, pdt, p, p, p, p, pdt, cw, conv_b.reshape(1, -1).astype(F32), a_n, bias_n,
      dsk, norm_g.reshape(1, MIX_W).astype(F32), tri, shifts, spread)
    return yl, yc


def _s5_discretize(lam_re, lam_im, log_step, b_re, b_im):
    step = jnp.exp(log_step.astype(F32))[:, None]
    lr = jnp.minimum(lam_re.astype(F32), -1e-4)
    li = lam_im.astype(F32)
    mag = jnp.exp(lr * step)
    ang = li * step
    ab_re, ab_im = mag * jnp.cos(ang), mag * jnp.sin(ang)
    den = lr * lr + li * li
    f_re = ((ab_re - 1.0) * lr + ab_im * li) / den
    f_im = (ab_im * lr - (ab_re - 1.0) * li) / den
    br, bi = b_re.astype(F32), b_im.astype(F32)
    bb_re = f_re[..., None] * br - f_im[..., None] * bi
    bb_im = f_re[..., None] * bi + f_im[..., None] * br
    return ab_re, ab_im, bb_re, bb_im


def _s5_mats(lam_re, lam_im, log_step, b_re, b_im, c_re, c_im):
    eye = jnp.eye(S5_GROUPS, dtype=F32)
    a_all, b_all, c_all = [], [], []
    for d in range(2):
        ab_re, ab_im, bb_re, bb_im = _s5_discretize(lam_re[d], lam_im[d], log_step[d], b_re[d], b_im[d])
        bm = [jnp.einsum('gnp,gh->gphn', m, eye).reshape(MIX_W, S5_LANES) for m in (bb_re, bb_im)]
        cm = [jnp.einsum('gpn,gh->gnhp', m.astype(F32), eye).reshape(S5_LANES, MIX_W)
              for m in (c_re[d], c_im[d])]
        b_all.append(jnp.concatenate(bm, axis=1))
        c_all.append(jnp.concatenate([cm[0], -cm[1]], axis=0))
        a_all.append(jnp.concatenate([ab_re.reshape(1, S5_LANES), ab_im.reshape(1, S5_LANES)], axis=1))
    a = jnp.broadcast_to(jnp.stack(a_all), (2, 8, 2 * S5_LANES))
    return a, jnp.stack(b_all).astype(BF16), jnp.stack(c_all).astype(BF16)


def _s5_kernel(uf_ref, ub_ref, a_ref, b_ref, c_ref, yf_ref, yb_ref, buf_ref, s_ref):
    n = S5_LANES

    @pl.when(pl.program_id(0) == 0)
    def _():
        s_ref[...] = jnp.zeros_like(s_ref)

    for d, u_ref in enumerate((uf_ref, ub_ref)):
        buf_ref[d] = _dot(u_ref[...], b_ref[d])
    state = [(s_ref[d, :, 0:n], s_ref[d, :, n:2 * n]) for d in range(2)]
    for j in range(S5_STEPS):
        for d in range(2):
            jj = j if d == 0 else S5_STEPS - 1 - j
            rows = slice(jj * 8, jj * 8 + 8)
            a_re, a_im = a_ref[d, :, 0:n], a_ref[d, :, n:2 * n]
            s_re, s_im = state[d]
            n_re = a_re * s_re - a_im * s_im + buf_ref[d, rows, 0:n]
            n_im = a_re * s_im + a_im * s_re + buf_ref[d, rows, n:2 * n]
            buf_ref[d, rows, 0:n] = n_re
            buf_ref[d, rows, n:2 * n] = n_im
            state[d] = (n_re, n_im)
    for d, y_ref in enumerate((yf_ref, yb_ref)):
        s_ref[d, :, 0:n] = state[d][0]
        s_ref[d, :, n:2 * n] = state[d][1]
        y_ref[...] = _dot(buf_ref[d].astype(BF16), c_ref[d])


def _s5_finish_kernel(yf_ref, yb_ref, u_ref, d_ref, w_ref, o_ref):
    y = yf_ref[...] + yb_ref[...] + d_ref[...] * u_ref[...].astype(F32)
    v = jax.nn.gelu(y, approximate=True).astype(BF16)
    r = _dot(v, w_ref[...])
    o_ref[...] = (r[:, 0:MIX_W] * jax.nn.sigmoid(r[:, MIX_W:2 * MIX_W])).astype(BF16)


def _s5(p, lam_re, lam_im, log_step, b_re, b_im, c_re, c_im, d_skip, glu_w, bsz, seq, ctx):
    assert bsz <= 8
    t_lat = bsz * seq
    u = p[:, C_BU:C_BU + MIX_W]
    u_l = jnp.transpose(u[:t_lat].reshape(bsz, seq, MIX_W), (1, 0, 2))
    u_c = jnp.transpose(u[t_lat:].reshape(bsz, ctx, MIX_W), (1, 0, 2))
    u_tm = jnp.concatenate([u_c, u_l], axis=0)
    if bsz < 8:
        u_tm = jnp.pad(u_tm, ((0, 0), (0, 8 - bsz), (0, 0)))
    steps = seq + ctx
    u_tm = u_tm.reshape(steps * 8, MIX_W)
    a, bmat, cmat = _s5_mats(lam_re, lam_im, log_step, b_re, b_im, c_re, c_im)
    rows = S5_STEPS * 8
    nc, ncc = steps // S5_STEPS, ctx // S5_STEPS

    def bwd(i):
        return jnp.where(i < ncc, ncc - 1 - i, nc + ncc - 1 - i)

    def whole(shape):
        return pl.BlockSpec(shape, lambda i: (0,) * len(shape))

    yshape = jax.ShapeDtypeStruct((steps * 8, MIX_W), F32)
    yf, yb = pl.pallas_call(
        _s5_kernel,
        out_shape=(yshape, yshape),
        grid=(nc,),
        in_specs=[pl.BlockSpec((rows, MIX_W), lambda i: (i, 0)),
                  pl.BlockSpec((rows, MIX_W), lambda i: (bwd(i), 0)),
                  whole((2, 8, 2 * S5_LANES)), whole((2, MIX_W, 2 * S5_LANES)),
                  whole((2, 2 * S5_LANES, MIX_W))],
        out_specs=(pl.BlockSpec((rows, MIX_W), lambda i: (i, 0)),
                   pl.BlockSpec((rows, MIX_W), lambda i: (bwd(i), 0))),
        scratch_shapes=[pltpu.VMEM((2, rows, 2 * S5_LANES), F32), pltpu.VMEM((2, 8, 2 * S5_LANES), F32)],
        compiler_params=_cp("arbitrary"),
        name="s5_scan",
    )(u_tm, u_tm, a, bmat, cmat)

    tmf = _pow2_tile(2048, steps * 8)
    o = pl.pallas_call(
        _s5_finish_kernel,
        out_shape=jax.ShapeDtypeStruct((steps * 8, MIX_W), BF16),
        grid=(steps * 8 // tmf,),
        in_specs=[pl.BlockSpec((tmf, MIX_W), lambda i: (i, 0)),
                  pl.BlockSpec((tmf, MIX_W), lambda i: (i, 0)),
                  pl.BlockSpec((tmf, MIX_W), lambda i: (i, 0)),
                  pl.BlockSpec((1, MIX_W), lambda i: (0, 0)),
                  pl.BlockSpec((MIX_W, 2 * MIX_W), lambda i: (0, 0))],
        out_specs=pl.BlockSpec((tmf, MIX_W), lambda i: (i, 0)),
        compiler_params=_cp("parallel"),
        name="s5_finish",
    )(yf, yb, u_tm, d_skip.reshape(1, MIX_W).astype(F32), glu_w.astype(BF16))
    o = o.reshape(steps, 8, MIX_W)[:, :bsz]
    o_c = jnp.transpose(o[:ctx], (1, 0, 2)).reshape(bsz * ctx, MIX_W)
    o_l = jnp.transpose(o[ctx:], (1, 0, 2)).reshape(t_lat, MIX_W)
    return o_l, o_c


def _rope_tables(seq, tm):
    rows = seq // GRID_W
    pos_r = jnp.repeat(jnp.arange(rows, dtype=F32), GRID_W)
    pos_c = jnp.tile(jnp.arange(GRID_W, dtype=F32), rows)
    inv = ROPE_BASE ** (-jnp.arange(ROPE_FREQS, dtype=F32) / ROPE_FREQS)
    ar, ac = pos_r[:, None] * inv, pos_c[:, None] * inv
    cos = jnp.concatenate([jnp.cos(ar), jnp.cos(ar), jnp.cos(ac), jnp.cos(ac)], axis=-1)
    sin = jnp.concatenate([-jnp.sin(ar), jnp.sin(ar), -jnp.sin(ac), jnp.sin(ac)], axis=-1)
    cos = jnp.concatenate([jnp.tile(cos, (1, 2)), jnp.ones((tm, 128), F32)], axis=0)
    sin = jnp.concatenate([jnp.tile(sin, (1, 2)), jnp.zeros((tm, 128), F32)], axis=0)
    return cos, sin


def _prep_consts():
    i = np.arange(MIX_W)
    bd = ((i[:, None] // HEAD_DIM) == (i[None, :] // HEAD_DIM)).astype(np.float32) / HEAD_DIM
    pm = (i[:, None] == (i[None, :] ^ ROPE_FREQS)).astype(np.float32)
    return jnp.asarray(bd, BF16), jnp.asarray(pm, BF16)


def _prep_kernel(cq_ref, dq_ref, ck_ref, dk_ref, cos_ref, sin_ref, qg_ref, kg_ref, bd_ref, pm_ref,
                 q1_ref, q2_ref, k1_ref, k2_ref):
    cos, sin = cos_ref[...], sin_ref[...]
    cos2 = jnp.concatenate([cos, cos], axis=-1)
    sin2 = jnp.concatenate([sin, sin], axis=-1)
    bd, pm = bd_ref[...], pm_ref[...]
    tm = cos.shape[0]
    lane = lax.broadcasted_iota(jnp.int32, (tm, 128), 1)

    def rms(x, g, n):
        ms = _dot((x * x).astype(BF16), bd[:n, :n])
        return x * lax.rsqrt(ms + EPS) * g

    def rope(y, c, s, n):
        return y * c + _dot(y.astype(BF16), pm[:n, :n]) * s

    def store_q(q, ref):
        q = q * ATTN_SCALE
        for kv in range(2):
            for g in range(2):
                half = q[:, 128 * kv:128 * (kv + 1)]
                if g != kv:
                    half = pltpu.roll(half, HEAD_DIM, 1)
                keep = (lane >= HEAD_DIM * kv) & (lane < HEAD_DIM * (kv + 1))
                ref[2 * kv + g] = jnp.where(keep, half, 0.0).astype(BF16)

    store_q(rope(rms(cq_ref[...].astype(F32), qg_ref[...], MIX_W), cos2, sin2, MIX_W), q1_ref)
    store_q(rope(dq_ref[...].astype(F32), cos2, sin2, MIX_W), q2_ref)
    k1 = rope(rms(ck_ref[...].astype(F32), kg_ref[...], 128), cos, sin, 128)
    for u in range(tm // KT_UNIT):
        k1_ref[u] = k1[u * KT_UNIT:(u + 1) * KT_UNIT].T.astype(BF16)
    k2 = rope(dk_ref[...].astype(F32), cos, sin, 128)
    for u in range(tm // WINDOW):
        k2_ref[u] = k2[u * WINDOW:(u + 1) * WINDOW].T.astype(BF16)


def _prep(p, qk_gain, t_lat, seq, tm):
    t = p.shape[0]
    cos, sin = _rope_tables(seq, tm)
    bd, pm = _prep_consts()
    qg = jnp.tile(qk_gain[0].astype(F32), N_HEADS).reshape(1, MIX_W)
    kg = jnp.tile(qk_gain[1].astype(F32), 2).reshape(1, 128)
    nt = seq // tm

    def tab(i):
        return (jnp.where(i * tm >= t_lat, nt, i % nt), 0)

    def const(shape):
        return pl.BlockSpec(shape, lambda i: (0,) * len(shape))

    qshape = jax.ShapeDtypeStruct((N_HEADS, t, 128), BF16)
    ktshape = jax.ShapeDtypeStruct((t // KT_UNIT, 128, KT_UNIT), BF16)
    kt2shape = jax.ShapeDtypeStruct((t // WINDOW, 128, WINDOW), BF16)
    return pl.pallas_call(
        _prep_kernel,
        out_shape=(qshape, qshape, ktshape, kt2shape),
        grid=(t // tm,),
        in_specs=[pl.BlockSpec((tm, MIX_W), lambda i: (i, C_CQ // MIX_W)),
                  pl.BlockSpec((tm, MIX_W), lambda i: (i, C_DQ // MIX_W)),
                  pl.BlockSpec((tm, 128), lambda i: (i, C_CK // 128)),
                  pl.BlockSpec((tm, 128), lambda i: (i, C_DK // 128)),
                  pl.BlockSpec((tm, 128), tab), pl.BlockSpec((tm, 128), tab),
                  const((1, MIX_W)), const((1, 128)), const((MIX_W, MIX_W)), const((MIX_W, MIX_W))],
        out_specs=(pl.BlockSpec((N_HEADS, tm, 128), lambda i: (0, i, 0)),
                   pl.BlockSpec((N_HEADS, tm, 128), lambda i: (0, i, 0)),
                   pl.BlockSpec((tm // KT_UNIT, 128, KT_UNIT), lambda i: (i, 0, 0)),
                   pl.BlockSpec((tm // WINDOW, 128, WINDOW), lambda i: (i, 0, 0))),
        compiler_params=_cp("parallel"),
        name="qk_prep",
    )(p, p, p, p, cos, sin, qg, kg, bd, pm)


def _pack_heads(o, tq):
    lane = lax.broadcasted_iota(jnp.int32, (tq, 128), 1)
    left = lane < HEAD_DIM
    o00, o01, o10, o11 = [o[h * tq:(h + 1) * tq] for h in range(N_HEADS)]
    out0 = jnp.where(left, o00, pltpu.roll(o01, HEAD_DIM, 1))
    out1 = jnp.where(left, pltpu.roll(o10, HEAD_DIM, 1), o11)
    return jnp.concatenate([out0, out1], axis=-1)


def _gattn_kernel(q_ref, kl_ref, vl_ref, kc_ref, vc_ref, o_ref, m_ref, acc_ref,
                  *, tq, units, n_qb_lat, n_kvb):
    qb = pl.program_id(1)

    def tree(op, xs):
        while len(xs) > 1:
            xs = [op(xs[i], xs[i + 1]) for i in range(0, len(xs) - 1, 2)] + ([xs[-1]] if len(xs) % 2 else [])
        return xs[0]

    def scores(h, kts):
        q = q_ref[h]
        cols = []
        for kt in kts:
            s = _dot(q, kt)
            cols += [s[:, 128 * c:128 * (c + 1)] for c in range(s.shape[1] // 128)]
        return cols

    def softmax(h, cols, first):
        m_blk = jnp.max(tree(jnp.maximum, cols), axis=-1, keepdims=True)
        if first:
            m_new = jnp.broadcast_to(m_blk, (tq, 128))
            alpha = None
        else:
            m_old = m_ref[h]
            m_new = jnp.maximum(m_old, m_blk)
            alpha = jnp.exp(m_old - m_new)
        m_ref[h] = m_new
        return alpha, jnp.concatenate([jnp.exp((c - m_new).astype(BF16)) for c in cols], axis=-1)

    def weighted(h, alpha, p, v):
        pv = _dot(p, v)
        acc_ref[h] = pv if alpha is None else jnp.concatenate([alpha, alpha], axis=-1) * acc_ref[h] + pv

    def block(kts, v, first):
        v = jnp.concatenate([v, jnp.ones_like(v)], axis=-1)
        cols = scores(0, kts)
        for h in range(N_HEADS):
            nxt = scores(h + 1, kts) if h + 1 < N_HEADS else None
            alpha, p = softmax(h, cols, first)
            weighted(h, alpha, p, v)
            cols = nxt

    block([kc_ref[u] for u in range(kc_ref.shape[0])], vc_ref[...], True)

    def body(j, carry):
        rows = pl.ds(pl.multiple_of(j * (units * KT_UNIT), units * KT_UNIT), units * KT_UNIT)
        block([kl_ref[j * units + u] for u in range(units)], vl_ref[rows, :], False)
        return carry

    lax.fori_loop(0, jnp.where(qb < n_qb_lat, n_kvb, 0), body, 0)
    o = [acc_ref[h, :, 0:128] / acc_ref[h, :, 128:256] for h in range(N_HEADS)]
    o_ref[...] = _pack_heads(jnp.concatenate(o, axis=0), tq).astype(BF16)


def _gattn(qp, kt, p, bsz, seq, ctx, with_ctx):
    t_lat = bsz * seq
    tq = ctx
    units = min(8, seq // KT_UNIT)
    n_qb_lat = seq // tq
    n_qb = n_qb_lat + (1 if with_ctx else 0)
    cb0 = t_lat // ctx
    t_out = t_lat + (bsz * ctx if with_ctx else 0)
    assert ctx % KT_UNIT == 0 and seq % (units * KT_UNIT) == 0

    def qrow(b, i):
        return jnp.where(i < n_qb_lat, b * n_qb_lat + i, cb0 + b)

    kern = functools.partial(_gattn_kernel, tq=tq, units=units, n_qb_lat=n_qb_lat,
                             n_kvb=seq // (units * KT_UNIT))
    return pl.pallas_call(
        kern,
        out_shape=jax.ShapeDtypeStruct((t_out, MIX_W), BF16),
        grid=(bsz, n_qb),
        in_specs=[pl.BlockSpec((N_HEADS, tq, 128), lambda b, i: (0, qrow(b, i), 0)),
                  pl.BlockSpec((seq // KT_UNIT, 128, KT_UNIT), lambda b, i: (b, 0, 0)),
                  pl.BlockSpec((seq, 128), lambda b, i: (b, C_CV // 128)),
                  pl.BlockSpec((ctx // KT_UNIT, 128, KT_UNIT), lambda b, i: (cb0 + b, 0, 0)),
                  pl.BlockSpec((ctx, 128), lambda b, i: (cb0 + b, C_CV // 128))],
        out_specs=pl.BlockSpec((tq, MIX_W), lambda b, i: (qrow(b, i), 0)),
        scratch_shapes=[pltpu.VMEM((N_HEADS, tq, 128), F32), pltpu.VMEM((N_HEADS, tq, 256), F32)],
        compiler_params=_cp("parallel", "arbitrary"),
        name="global_attn",
    )(qp, kt, p, kt, p)


def _wattn_kernel(q_ref, kl_ref, vl_ref, kc_ref, vc_ref, sink_ref, o_ref, *, nb):
    w = WINDOW
    n = pl.program_id(1)
    is_lat = n < nb
    rows = 2 * w
    qi = lax.broadcasted_iota(jnp.int32, (rows, w), 0) & (w - 1)
    kj = lax.broadcasted_iota(jnp.int32, (rows, w), 1)
    band = (jnp.clip(n - 1, 0, nb - 1), jnp.clip(n, 0, nb - 1), jnp.clip(n + 1, 0, nb - 1))
    off_prev = jnp.where(is_lat & (n >= 1), 0, w)
    off_cur = jnp.where(is_lat, 0, w)
    off_next = jnp.where(n + 1 < nb, 0, w)
    masks = [kj >= qi + off_prev, kj >= off_cur, kj <= qi - off_next]
    kts = [kc_ref[u] for u in range(kc_ref.shape[0])] + [kl_ref[i] for i in band]
    n_ctx_tiles = kc_ref.shape[0]
    v_all = jnp.concatenate([vc_ref[...]] + [vl_ref[pl.ds(pl.multiple_of(i * w, w), w), :] for i in band],
                            axis=0)

    def tree(op, xs):
        while len(xs) > 1:
            xs = [op(xs[i], xs[i + 1]) for i in range(0, len(xs) - 1, 2)] + ([xs[-1]] if len(xs) % 2 else [])
        return xs[0]

    def scores(c):
        q = q_ref[2 * c:2 * c + 2].reshape(rows, 128)
        tiles = [_dot(q, kt) for kt in kts]
        return tiles[:n_ctx_tiles] + [jnp.where(mk, t, NEG_INF) for mk, t in zip(masks, tiles[n_ctx_tiles:])]

    def finish(c, tiles):
        sink = sink_ref[c * rows:(c + 1) * rows, :]
        m = jnp.maximum(jnp.max(tree(jnp.maximum, tiles), axis=-1, keepdims=True), sink)
        ps = [jnp.exp(t - m) for t in tiles]
        den = jnp.sum(tree(jnp.add, ps), axis=-1, keepdims=True) + jnp.exp(sink - m)
        return _dot(jnp.concatenate(ps, axis=-1).astype(BF16), v_all) / den

    tiles = scores(0)
    nxt = scores(1)
    o = [finish(0, tiles), finish(1, nxt)]
    o_ref[...] = _pack_heads(jnp.concatenate(o, axis=0), w).astype(BF16)


def _wattn(qp, k, p, sink, bsz, seq, ctx, with_ctx):
    t_lat = bsz * seq
    w = WINDOW
    nb = seq // w
    ncb = ctx // w
    n_qb = nb + (ncb if with_ctx else 0)
    cq0 = t_lat // w
    cb0 = t_lat // ctx
    t_out = t_lat + (bsz * ctx if with_ctx else 0)
    sink_rows = jnp.broadcast_to(jnp.repeat(sink.astype(F32), w)[:, None], (N_HEADS * w, 128))

    def qrow(b, i):
        return jnp.where(i < nb, b * nb + i, cq0 + b * ncb + (i - nb))

    return pl.pallas_call(
        functools.partial(_wattn_kernel, nb=nb),
        out_shape=jax.ShapeDtypeStruct((t_out, MIX_W), BF16),
        grid=(bsz, n_qb),
        in_specs=[pl.BlockSpec((N_HEADS, w, 128), lambda b, i: (0, qrow(b, i), 0)),
                  pl.BlockSpec((seq // w, 128, w), lambda b, i: (b, 0, 0)),
                  pl.BlockSpec((seq, 128), lambda b, i: (b, C_DV // 128)),
                  pl.BlockSpec((ctx // w, 128, w), lambda b, i: (cb0 + b, 0, 0)),
                  pl.BlockSpec((ctx, 128), lambda b, i: (cb0 + b, C_DV // 128)),
                  pl.BlockSpec((N_HEADS * w, 128), lambda b, i: (0, 0))],
        out_specs=pl.BlockSpec((w, MIX_W), lambda b, i: (qrow(b, i), 0)),
        compiler_params=_cp("parallel", "arbitrary"),
        name="window_attn",
    )(qp, k, p, k, p, sink_rows)


def _merge_kernel(yal_ref, yac_ref, ybl_ref, ybc_ref, yc_ref, yd_ref, gate_ref, x_ref, mod_ref, wbr_ref,
                  wout_ref, g2_ref, xo_ref, h2_ref, *maybe_tok_ref, n_lat_tiles):
    tm, d = x_ref.shape
    ctx_rows = jnp.full((tm, MIX_W), pl.program_id(0), jnp.int32) >= n_lat_tiles
    ya = jnp.where(ctx_rows, yac_ref[...], yal_ref[...])
    yb = jnp.where(ctx_rows, ybc_ref[...], ybl_ref[...])
    acc = None
    for n, y in enumerate((ya, yb, yc_ref[...], yd_ref[...])):
        gate = jax.nn.sigmoid(gate_ref[:, n * d:(n + 1) * d].astype(F32))
        term = gate * _dot(y, wbr_ref[n])
        acc = term if acc is None else acc + term
    x = x_ref[...] + mod_ref[0, 2:3, :] * _dot(acc.astype(BF16), wout_ref[...])
    xo_ref[...] = x
    ms = jnp.mean(x * x, axis=-1, keepdims=True)
    y = x * lax.rsqrt(ms + EPS) * g2_ref[...]
    h2 = y * (1.0 + mod_ref[0, 4:5, :]) + mod_ref[0, 3:4, :]
    h2_ref[...] = h2
    if maybe_tok_ref:
        _to_token_tiles(maybe_tok_ref[0], h2)


def _to_token_tiles(ref, x):
    rows, d = x.shape
    k = d // 128
    for c in range(k):
        ref[pl.ds(c, rows, stride=k), :] = x[:, 128 * c:128 * (c + 1)]


def _from_token_tiles(ref, rows, d):
    k = d // 128
    return jnp.concatenate([ref[pl.ds(c, rows, stride=k), :] for c in range(k)], axis=-1)


def _merge(ys, p, x, mod, wbr, wout, g2, t_out, t_lat, seq, tm, token_tiles):
    d = x.shape[1]
    k = d // 128

    def row(width):
        return pl.BlockSpec((tm, width), lambda i: (i, 0))

    def const(shape):
        return pl.BlockSpec(shape, lambda i: (0,) * len(shape))

    out_shape = [jax.ShapeDtypeStruct((t_out, d), F32), jax.ShapeDtypeStruct((t_out, d), F32)]
    out_specs = [row(d), row(d)]
    if token_tiles:
        out_shape.append(jax.ShapeDtypeStruct((t_out * k, 128), F32))
        out_specs.append(pl.BlockSpec((tm * k, 128), lambda i: (i, 0)))
    (ya_l, ya_c), (yb_l, yb_c), yc, yd = ys
    n_lat_tiles = t_lat // tm
    n_ctx_tiles = ya_c.shape[0] // tm
    lat = pl.BlockSpec((tm, MIX_W), lambda i: (jnp.minimum(i, n_lat_tiles - 1), 0))
    cx = pl.BlockSpec((tm, MIX_W), lambda i: (jnp.clip(i - n_lat_tiles, 0, n_ctx_tiles - 1), 0))
    return pl.pallas_call(
        functools.partial(_merge_kernel, n_lat_tiles=n_lat_tiles),
        out_shape=tuple(out_shape),
        grid=(t_out // tm,),
        in_specs=[lat, cx, lat, cx, row(MIX_W), row(MIX_W), row(4 * d), row(d),
                  pl.BlockSpec((1, 6, d), lambda i: (_mod_group(i * tm, t_lat, seq), 0, 0)),
                  const((4, MIX_W, d)), const((d, d)), const((1, d))],
        out_specs=tuple(out_specs),
        compiler_params=_cp("parallel"),
        name="merge",
    )(ya_l, ya_c, yb_l, yb_c, yc, yd, p, x, mod, wbr, wout, g2.reshape(1, d))


def _ffn_dense_kernel(h_ref, x_ref, mod_ref, wg_ref, wu_ref, wo_ref, o_ref, hb_ref, acc_ref):
    j = pl.program_id(1)

    @pl.when(j == 0)
    def _():
        hb_ref[...] = h_ref[...].astype(BF16)
        acc_ref[...] = jnp.zeros_like(acc_ref)

    h = hb_ref[...]
    a = _silu(_dot(h, wg_ref[...])) * _dot(h, wu_ref[...])
    acc_ref[...] += _dot(a.astype(BF16), wo_ref[...])

    @pl.when(j == pl.num_programs(1) - 1)
    def _():
        o_ref[...] = x_ref[...] + mod_ref[0, 5:6, :] * acc_ref[...]


def _ffn_dense(h2, x, mod, w_in, w_out, t_lat, seq, tm):
    t, d = x.shape
    f = w_out.shape[0]
    tf = 256
    nf = f // tf
    return pl.pallas_call(
        _ffn_dense_kernel,
        out_shape=jax.ShapeDtypeStruct((t, d), F32),
        grid=(t // tm, nf),
        in_specs=[pl.BlockSpec((tm, d), lambda i, j: (i, 0)),
                  pl.BlockSpec((tm, d), lambda i, j: (i, 0)),
                  pl.BlockSpec((1, 6, d), lambda i, j: (_mod_group(i * tm, t_lat, seq), 0, 0)),
                  pl.BlockSpec((d, tf), lambda i, j: (0, j)),
                  pl.BlockSpec((d, tf), lambda i, j: (0, nf + j)),
                  pl.BlockSpec((tf, d), lambda i, j: (j, 0))],
        out_specs=pl.BlockSpec((tm, d), lambda i, j: (i, 0)),
        scratch_shapes=[pltpu.VMEM((tm, d), BF16), pltpu.VMEM((tm, d), F32)],
        compiler_params=_cp("parallel", "arbitrary"),
        name="ffn_dense",
    )(h2, x, mod, w_in, w_in, w_out)


def _router_kernel(h_ref, w_ref, e_ref, g1_ref, g2_ref):
    h = h_ref[...]
    h_hi = h.astype(BF16)
    h_lo = (h - h_hi.astype(F32)).astype(BF16)
    logits = _dot(h_hi, w_ref[0]) + _dot(h_lo, w_ref[0]) + _dot(h_hi, w_ref[1])
    lane = lax.broadcasted_iota(jnp.int32, logits.shape, 1)
    lane_f = lane.astype(F32)
    logits = jnp.where(lane < N_EXPERTS, logits, -jnp.inf)
    m1 = jnp.max(logits, axis=-1, keepdims=True)
    i1 = jnp.min(jnp.where(logits == m1, lane_f, 128.0), axis=-1, keepdims=True)
    rest = jnp.where(lane_f == i1, -jnp.inf, logits)
    m2 = jnp.max(rest, axis=-1, keepdims=True)
    i2 = jnp.min(jnp.where(rest == m2, lane_f, 128.0), axis=-1, keepdims=True)
    e2 = jnp.exp(m2 - m1)
    g1 = 1.0 / (1.0 + e2)
    e_ref[...] = jnp.where(lane == 0, i1, jnp.where(lane == 1, i2, 0.0)).astype(jnp.int32)
    g1_ref[...] = jnp.broadcast_to(g1, g1_ref.shape)
    g2_ref[...] = jnp.broadcast_to(e2 * g1, g2_ref.shape)


def _router(h2, router, tm):
    t, d = h2.shape
    r = jnp.zeros((d, 128), F32).at[:, :N_EXPERTS].set(router.astype(F32))
    r_hi = r.astype(BF16)
    r_lo = (r - r_hi.astype(F32)).astype(BF16)
    shp = jax.ShapeDtypeStruct((t, 128), F32)
    return pl.pallas_call(
        _router_kernel,
        out_shape=(jax.ShapeDtypeStruct((t, 128), jnp.int32), shp, shp),
        grid=(t // tm,),
        in_specs=[pl.BlockSpec((tm, d), lambda i: (i, 0)),
                  pl.BlockSpec((2, d, 128), lambda i: (0, 0, 0))],
        out_specs=(pl.BlockSpec((tm, 128), lambda i: (i, 0)),) * 3,
        compiler_params=_cp("parallel"),
        name="router",
    )(h2, jnp.stack([r_hi, r_lo]))


def _experts_kernel(te_ref, tok0_ref, tokn_ref, dst_ref, h_ref, wg_ref, wu_ref, wo_ref, y_ref,
                    xbuf, ybuf, xb_ref, acc_ref, sem_in, sem_out, *, n_tiles, tm):
    i, j = pl.program_id(0), pl.program_id(1)
    last_j = pl.num_programs(1) - 1
    slot = i % 2
    other = 1 - slot
    per_step = tm // MOE_STEPS
    tr = TOK_ROWS

    def tok(ref, t):
        return ref.at[pl.ds(pl.multiple_of(t * tr, tr), tr)]

    def gather(idx_ref, r, s):
        return pltpu.make_async_copy(tok(h_ref, idx_ref[0, 0, r]), tok(xbuf.at[s], r), sem_in.at[s])

    def scatter(r, s):
        return pltpu.make_async_copy(tok(ybuf.at[s], r), tok(y_ref, dst_ref[0, 0, r]), sem_out.at[s])

    def wait_tile(copy_of_row0):
        def body(k, carry):
            for _ in range(MOE_WAITS):
                copy_of_row0().wait()
            return carry

        lax.fori_loop(0, tm // MOE_WAITS, body, 0)

    def wait_gather(s):
        wait_tile(lambda: pltpu.make_async_copy(tok(h_ref, 0), tok(xbuf.at[s], 0), sem_in.at[s]))

    def wait_scatter(s):
        wait_tile(lambda: pltpu.make_async_copy(tok(ybuf.at[s], 0), tok(y_ref, 0), sem_out.at[s]))

    def row_traffic(with_gather, with_scatter):
        def rows(base, count):
            for k in range(count):
                if with_gather:
                    gather(tokn_ref, base + k, other).start(priority=1)
                if with_scatter:
                    scatter(base + k, other).start(priority=1)

        rows(j * per_step, per_step)

        @pl.when(j == last_j)
        def _():
            rows(per_step * MOE_STEPS, tm - per_step * MOE_STEPS)

    def compute():
        x = xb_ref[...]
        a = _silu(_dot(x, wg_ref[0])) * _dot(x, wu_ref[0])
        acc_ref[...] += _dot(a.astype(BF16), wo_ref[0])

    @pl.when((i == 0) & (j == 0))
    def _():
        def body(r, carry):
            gather(tok0_ref, r, 0).start()
            return carry

        lax.fori_loop(0, tm, body, 0)

    @pl.when(j == 0)
    def _():
        wait_gather(slot)

        @pl.when(i < n_tiles)
        def _():
            xb_ref[...] = _from_token_tiles(xbuf.at[slot], tm, xb_ref.shape[1]).astype(BF16)
            acc_ref[...] = jnp.zeros_like(acc_ref)

    @pl.when(i == 0)
    def _():
        compute()
        row_traffic(True, False)

    @pl.when((i > 0) & (i < n_tiles))
    def _():
        compute()
        row_traffic(True, True)

    @pl.when(i == n_tiles)
    def _():
        row_traffic(False, True)

    @pl.when(j == last_j)
    def _():
        @pl.when((i >= 2) & (i < n_tiles))
        def _():
            wait_scatter(slot)

        @pl.when(i < n_tiles)
        def _():
            _to_token_tiles(ybuf.at[slot], acc_ref[...])

        @pl.when(i == n_tiles)
        def _():
            wait_scatter(slot)
            wait_scatter(other)


def _experts(h2_tiles, src_tok, dst_row, tile_e, w_in, w_out, n_out_rows):
    d = w_in.shape[1]
    assert d == TOK_ROWS * 128
    n_tiles = dst_row.shape[0]
    f = w_out.shape[1]
    tf = f // MOE_STEPS
    tm = MOE_TILE
    assert n_tiles >= 2 and tm % MOE_WAITS == 0 and tf % 128 == 0

    def smem(index_map):
        return pl.BlockSpec((1, 1, tm), index_map, memory_space=pltpu.SMEM)

    grid_spec = pltpu.PrefetchScalarGridSpec(
        num_scalar_prefetch=1,
        grid=(n_tiles + 1, MOE_STEPS),
        in_specs=[smem(lambda i, j, te: (0, 0, 0)),
                  smem(lambda i, j, te: (jnp.minimum(i + 1, n_tiles), 0, 0)),
                  smem(lambda i, j, te: (jnp.clip(i - 1, 0, n_tiles - 1), 0, 0)),
                  pl.BlockSpec(memory_space=pl.ANY),
                  pl.BlockSpec((1, d, tf), lambda i, j, te: (te[i], 0, j)),
                  pl.BlockSpec((1, d, tf), lambda i, j, te: (te[i], 0, MOE_STEPS + j)),
                  pl.BlockSpec((1, tf, d), lambda i, j, te: (te[i], j, 0))],
        out_specs=pl.BlockSpec(memory_space=pl.ANY),
        scratch_shapes=[pltpu.VMEM((2, tm * TOK_ROWS, 128), F32), pltpu.VMEM((2, tm * TOK_ROWS, 128), F32),
                        pltpu.VMEM((tm, d), BF16), pltpu.VMEM((tm, d), F32),
                        pltpu.SemaphoreType.DMA((2,)), pltpu.SemaphoreType.DMA((2,))])
    return pl.pallas_call(
        functools.partial(_experts_kernel, n_tiles=n_tiles, tm=tm),
        out_shape=jax.ShapeDtypeStruct((n_out_rows * TOK_ROWS, 128), F32),
        grid_spec=grid_spec,
        compiler_params=_cp("arbitrary", "arbitrary"),
        name="experts",
    )(tile_e, src_tok, src_tok, dst_row, h2_tiles, w_in, w_in, w_out)


def _combine_kernel(y1_ref, y2_ref, x_ref, mod_ref, g1_ref, g2_ref, *rest):
    o_ref = rest[-1]
    tm, d = x_ref.shape
    g1 = jnp.concatenate([g1_ref[...]] * (d // 128), axis=-1)
    g2 = jnp.concatenate([g2_ref[...]] * (d // 128), axis=-1)
    y = g1 * _from_token_tiles(y1_ref, tm, d) + g2 * _from_token_tiles(y2_ref, tm, d)
    x = x_ref[...] + mod_ref[0, 5:6, :] * y
    if len(rest) == 2:
        ms = jnp.mean(x * x, axis=-1, keepdims=True)
        x = x * lax.rsqrt(ms + EPS) * rest[0][...]
    o_ref[...] = x


def _combine(y_tiles, x, mod, g1, g2, t_lat, seq, tm, final_g):
    t, d = x.shape
    k = d // 128
    extra_specs, extra_args = [], []
    if final_g is not None:
        extra_specs, extra_args = [pl.BlockSpec((1, d), lambda i: (0, 0))], [final_g.reshape(1, d)]
    return pl.pallas_call(
        _combine_kernel,
        out_shape=jax.ShapeDtypeStruct((t, d), F32),
        grid=(t // tm,),
        in_specs=[pl.BlockSpec((tm * k, 128), lambda i: (i, 0)),
                  pl.BlockSpec((tm * k, 128), lambda i: (t // tm + i, 0)),
                  pl.BlockSpec((tm, d), lambda i: (i, 0)),
                  pl.BlockSpec((1, 6, d), lambda i: (_mod_group(i * tm, t_lat, seq), 0, 0)),
                  pl.BlockSpec((tm, 128), lambda i: (i, 0)),
                  pl.BlockSpec((tm, 128), lambda i: (i, 0))] + extra_specs,
        out_specs=pl.BlockSpec((tm, d), lambda i: (i, 0)),
        compiler_params=_cp("parallel"),
        name="moe_combine",
    )(y_tiles, y_tiles, x, mod, g1, g2, *extra_args)


def _moe(h2, h2_tiles, x, mod, router, w_in, w_out, t_lat, seq, tm, final_g=None):
    t, d = h2.shape
    e_idx, g1, g2 = _router(h2, router, tm)
    e_flat = e_idx[:, :2].reshape(-1)
    onehot = (e_flat[:, None] == jnp.arange(N_EXPERTS, dtype=jnp.int32)[None, :]).astype(jnp.int32)
    csum = jnp.cumsum(onehot, axis=0)
    counts = csum[-1]
    padded = (counts + MOE_TILE - 1) // MOE_TILE * MOE_TILE
    ends = jnp.cumsum(padded)
    pstarts = ends - padded
    dest = jnp.sum(onehot * (pstarts[None, :] + csum - 1), axis=1).astype(jnp.int32)
    n_tiles = (2 * t + MOE_TILE - 1) // MOE_TILE + N_EXPERTS
    n_rows = n_tiles * MOE_TILE
    slot_a = jnp.full((n_rows,), -1, jnp.int32).at[dest].set(jnp.arange(2 * t, dtype=jnp.int32))
    is_pad = slot_a < 0
    slot_row = jnp.where(is_pad, 2 * t - 1 + jnp.cumsum(is_pad.astype(jnp.int32)),
                         (slot_a % 2) * t + slot_a // 2)
    src_tok = jnp.where(is_pad, 0, slot_a // 2)
    src_tok = jnp.concatenate([src_tok, jnp.zeros((MOE_TILE,), jnp.int32)]).reshape(n_tiles + 1, 1, MOE_TILE)
    tile_start = jnp.arange(n_tiles + 1, dtype=jnp.int32) * MOE_TILE
    tile_e = jnp.minimum(jnp.searchsorted(ends, tile_start, side='right'), N_EXPERTS - 1).astype(jnp.int32)

    y = _experts(h2_tiles, src_tok, slot_row.reshape(n_tiles, 1, MOE_TILE), tile_e, w_in, w_out, n_rows)
    return _combine(y, x, mod, g1, g2, t_lat, seq, tm, final_g)


def _final_norm_kernel(x_ref, g_ref, o_ref):
    x = x_ref[...]
    ms = jnp.mean(x * x, axis=-1, keepdims=True)
    o_ref[...] = x * lax.rsqrt(ms + EPS) * g_ref[...]


def _final_norm(x, g, tm):
    t, d = x.shape
    return pl.pallas_call(
        _final_norm_kernel,
        out_shape=jax.ShapeDtypeStruct((t, d), F32),
        grid=(t // tm,),
        in_specs=[pl.BlockSpec((tm, d), lambda i: (i, 0)), pl.BlockSpec((1, d), lambda i: (0, 0))],
        out_specs=pl.BlockSpec((tm, d), lambda i: (i, 0)),
        compiler_params=_cp("parallel"),
        name="final_norm",
    )(x, g.reshape(1, d))


def _proj_weights(w_in):
    o = {}
    acc = 0
    for name, size in (('a_z', 256), ('a_x', 256), ('a_b', 256), ('a_c', 256), ('a_dt', 8), ('b_u', 256),
                       ('c_q', 256), ('c_k', 128), ('c_v', 128), ('d_q', 256), ('d_k', 128), ('d_v', 128),
                       ('gates', 4096)):
        o[name] = (acc, size)
        acc += size
    order = ('gates', 'a_z', 'b_u', 'c_q', 'd_q', 'c_k', 'c_v', 'd_k', 'd_v', 'a_x', 'a_b', 'a_c')
    w = jnp.concatenate([w_in[:, :, o[n][0]:o[n][0] + o[n][1]] for n in order], axis=-1).astype(BF16)
    dt0 = o['a_dt'][0]
    wdt = jnp.pad(w_in[:, :, dt0:dt0 + 8], ((0, 0), (0, 0), (0, 120))).astype(BF16)
    return w, wdt


def kernel(x, c, ctx, c_ctx, norm1_g, norm2_g, ada_w, ada_b, w_in, ssd_conv_w, ssd_conv_b, ssd_a_log,
           ssd_dt_bias, ssd_d, ssd_norm_g, s5_lam_re, s5_lam_im, s5_log_step, s5_b_re, s5_b_im, s5_c_re,
           s5_c_im, s5_d, s5_glu_w, qk_norm_g, swa_sink, w_branch, w_out, ffn_w_in, ffn_w_out, moe_router,
           moe_w_in, moe_w_out, final_norm_g):
    bsz, seq, d = x.shape
    n_ctx = ctx.shape[1]
    depth = w_in.shape[0]
    t_lat, t_ctx = bsz * seq, bsz * n_ctx
    tm = _pow2_tile(1024, seq, t_ctx)
    tm_small = _pow2_tile(512, seq, t_ctx)

    cvec = jnp.zeros((16, d), F32).at[0].set(c_ctx).at[1:1 + bsz].set(c)
    mod = _adaln(cvec, ada_w, ada_b).reshape(depth, 16, 6, d)
    wp, wdt = _proj_weights(w_in)
    wbr = w_branch.astype(BF16)
    wo = w_out.astype(BF16)
    ffn_in, ffn_out = ffn_w_in.astype(BF16), ffn_w_out.astype(BF16)
    moe_in, moe_out = moe_w_in.astype(BF16), moe_w_out.astype(BF16)

    xx = jnp.concatenate([x.reshape(t_lat, d), ctx.reshape(t_ctx, d)], axis=0)
    for l in range(depth):
        with_ctx = l < depth - 1
        t_out = t_lat + (t_ctx if with_ctx else 0)
        p, pdt = _inproj(xx, norm1_g[l], mod[l], wp[l], wdt[l], t_lat, seq, tm)
        ya = _ssd(p, pdt, ssd_conv_w[l], ssd_conv_b[l], ssd_a_log[l], ssd_dt_bias[l], ssd_d[l],
                  ssd_norm_g[l], bsz, seq, n_ctx)
        yb = _s5(p, s5_lam_re[l], s5_lam_im[l], s5_log_step[l], s5_b_re[l], s5_b_im[l], s5_c_re[l],
                 s5_c_im[l], s5_d[l], s5_glu_w[l], bsz, seq, n_ctx)
        q1, q2, k1, k2 = _prep(p, qk_norm_g[l], t_lat, seq, tm)
        yc = _gattn(q1, k1, p, bsz, seq, n_ctx, with_ctx)
        yd = _wattn(q2, k2, p, swa_sink[l], bsz, seq, n_ctx, with_ctx)
        routed = l % 2 == 1
        xx, h2, *tiles = _merge((ya, yb, yc, yd), p, xx, mod[l], wbr[l], wo[l], norm2_g[l], t_out, t_lat, seq,
                                tm_small, routed)
        if routed:
            xx = _moe(h2, tiles[0], xx, mod[l], moe_router[l // 2], moe_in[l // 2], moe_out[l // 2], t_lat,
                      seq, tm_small, None if with_ctx else final_norm_g)
        else:
            xx = _ffn_dense(h2, xx, mod[l], ffn_in[l // 2], ffn_out[l // 2], t_lat, seq, tm)
    if depth % 2 == 1:
        xx = _final_norm(xx[:t_lat], final_norm_g, tm)
    return xx.reshape(bsz, seq, d)
```

```python
import functools

import numpy as np
import jax
import jax.numpy as jnp
from jax import lax
from jax.experimental import pallas as pl
from jax.experimental.pallas import tpu as pltpu

F32 = jnp.float32
BF16 = jnp.bfloat16

EPS = 1e-6
NEG_INF = -1e30
GRID_W = 64
MIX_W = 256
HEAD_DIM = 64
N_HEADS = 4
ATTN_SCALE = HEAD_DIM ** -0.5
SSD_STATE = 128
SSD_CONV = 5
CHUNK = 128
HALO = 16
S5_GROUPS = 16
S5_GROUP = 16
S5_STATE = 64
S5_LANES = S5_GROUPS * S5_STATE
S5_STEPS = 64
ROPE_BASE = 10000.0
ROPE_FREQS = 16
WINDOW = 128
KT_UNIT = 256
N_EXPERTS = 8
MOE_TILE = 512
MOE_STEPS = 7
MOE_WAITS = 64
TOK_ROWS = 8
VMEM_LIMIT = 56 * 1024 * 1024

C_GATES, C_Z, C_BU, C_CQ, C_DQ = 0, 4096, 4352, 4608, 4864
C_CK, C_CV, C_DK, C_DV = 5120, 5248, 5376, 5504
C_AX, C_AB, C_AC = 5632, 5888, 6144
P_COLS = 6400
PROJ_TN = 1280


def _cp(*sem):
    return pltpu.CompilerParams(dimension_semantics=sem, vmem_limit_bytes=VMEM_LIMIT)


def _pow2_tile(cap, *dims):
    t = 1
    while t * 2 <= cap and all(d % (t * 2) == 0 for d in dims):
        t *= 2
    return t


def _dot(a, b):
    return jnp.dot(a, b, preferred_element_type=F32)


def _dot_nt(a, b):
    return lax.dot_general(a, b, (((1,), (1,)), ((), ())), preferred_element_type=F32)


def _dot_tn(a, b):
    return lax.dot_general(a, b, (((0,), (0,)), ((), ())), preferred_element_type=F32)


def _split3(x):
    hi = x.astype(BF16)
    r1 = x - hi.astype(F32)
    mid = r1.astype(BF16)
    lo = (r1 - mid.astype(F32)).astype(BF16)
    return hi, mid, lo


def _silu(x):
    return x * jax.nn.sigmoid(x)


def _adaln_kernel(c_ref, w_ref, b_ref, o_ref):
    c = c_ref[...]
    o_ref[0] = jnp.dot(_silu(c), w_ref[0], preferred_element_type=F32,
                       precision=lax.Precision.HIGHEST) + b_ref[0]


def _adaln(cvec, ada_w, ada_b):
    depth, d, n = ada_w.shape
    tn = 1024
    return pl.pallas_call(
        _adaln_kernel,
        out_shape=jax.ShapeDtypeStruct((depth, 16, n), F32),
        grid=(depth, n // tn),
        in_specs=[pl.BlockSpec((16, d), lambda l, j: (0, 0)),
                  pl.BlockSpec((1, d, tn), lambda l, j: (l, 0, j)),
                  pl.BlockSpec((1, 1, tn), lambda l, j: (l, 0, j))],
        out_specs=pl.BlockSpec((1, 16, tn), lambda l, j: (l, 0, j)),
        compiler_params=_cp("parallel", "parallel"),
        name="adaln",
    )(cvec, ada_w, ada_b.reshape(depth, 1, n))


def _mod_group(row0, t_lat, seq):
    return jnp.where(row0 >= t_lat, 0, 1 + row0 // seq)


def _inproj_kernel(x_ref, g_ref, mod_ref, w_ref, wdt_ref, o_ref, odt_ref, h_ref):
    @pl.when(pl.program_id(1) == 0)
    def _():
        x = x_ref[...]
        ms = jnp.mean(x * x, axis=-1, keepdims=True)
        y = x * lax.rsqrt(ms + EPS) * g_ref[...]
        h = (y * (1.0 + mod_ref[0, 1:2, :]) + mod_ref[0, 0:1, :]).astype(BF16)
        h_ref[...] = h
        odt_ref[...] = _dot(h, wdt_ref[...])

    o_ref[...] = _dot(h_ref[...], w_ref[...]).astype(BF16)


def _inproj(x, g, mod, w, wdt, t_lat, seq, tm):
    t, d = x.shape
    n = w.shape[1]
    tn = PROJ_TN
    return pl.pallas_call(
        _inproj_kernel,
        out_shape=(jax.ShapeDtypeStruct((t, n), BF16), jax.ShapeDtypeStruct((t, 128), F32)),
        grid=(t // tm, n // tn),
        in_specs=[pl.BlockSpec((tm, d), lambda i, j: (i, 0)),
                  pl.BlockSpec((1, d), lambda i, j: (0, 0)),
                  pl.BlockSpec((1, 6, d), lambda i, j: (_mod_group(i * tm, t_lat, seq), 0, 0)),
                  pl.BlockSpec((d, tn), lambda i, j: (0, j)),
                  pl.BlockSpec((d, 128), lambda i, j: (0, 0))],
        out_specs=(pl.BlockSpec((tm, tn), lambda i, j: (i, j)),
                   pl.BlockSpec((tm, 128), lambda i, j: (i, 0))),
        scratch_shapes=[pltpu.VMEM((tm, d), BF16)],
        compiler_params=_cp("parallel", "arbitrary"),
        name="inproj",
    )(x, g.reshape(1, d), mod, w, wdt)


def _ssd_consts():
    r = np.arange(CHUNK)
    tri_l = (r[None, :] <= r[:, None]).astype(np.float32)
    tri_u = tri_l.T.copy()
    shifts = np.zeros((4, CHUNK, CHUNK + 2 * HALO), np.float32)
    for n, k in enumerate((0, 1, 3, 4)):
        shifts[n, r, r + HALO + k - 2] = 1.0
    spread = np.zeros((2, 128, MIX_W), np.float32)
    for d in range(2):
        for h in range(N_HEADS):
            spread[d, N_HEADS * d + h, HEAD_DIM * h:HEAD_DIM * (h + 1)] = 1.0
    return (jnp.asarray(np.stack([tri_l, tri_u]), BF16), jnp.asarray(shifts, BF16),
            jnp.asarray(spread, BF16))


def _ssd_kernel(zl_ref, xl_ref, bl_ref, cl_ref, dtl_ref, zc_ref, xc_ref, bc_ref, cc_ref, dtc_ref,
                cw_ref, cb_ref, an_ref, bias_ref, dsk_ref, ng_ref, tri_ref, sh_ref, e_ref,
                yl_ref, yc_ref, act_l, act_c, yf_l, yf_c, stf_ref, stb_ref, *, n_lat, n_ctx):
    q = CHUNK
    lane128 = lax.broadcasted_iota(jnp.int32, (q, 128), 1)
    lane256 = lax.broadcasted_iota(jnp.int32, (q, MIX_W), 1)
    lane256r = lax.broadcasted_iota(jnp.int32, (1, MIX_W), 1)
    row_i = lax.broadcasted_iota(jnp.int32, (q, q), 0)
    col_i = lax.broadcasted_iota(jnp.int32, (q, q), 1)
    head_masks = [(lane256 >= HEAD_DIM * h) & (lane256 < HEAD_DIM * (h + 1)) for h in range(N_HEADS)]
    head_masks_r = [(lane256r >= HEAD_DIM * h) & (lane256r < HEAD_DIM * (h + 1)) for h in range(N_HEADS)]
    group_masks = [lane256 < 128, lane256 >= 128]
    lane_state = lax.broadcasted_iota(jnp.int32, (SSD_STATE, MIX_W), 1)
    state_group_masks = [lane_state < 128, lane_state >= 128]

    def conv_act(x_ref, b_ref, c_ref, n_chunks, c):
        def rows(ref, start, size):
            return ref[pl.ds(start, size), :]

        start = c * q
        if isinstance(c, int):
            p0, n0 = max(start - HALO, 0), min(start + q, n_chunks * q - HALO)
            pf, nf = float(c > 0), float(c < n_chunks - 1)
        else:
            start = pl.multiple_of(start, q)
            p0 = pl.multiple_of(jnp.maximum(start - HALO, 0), HALO)
            n0 = pl.multiple_of(jnp.minimum(start + q, n_chunks * q - HALO), HALO)
            pf, nf = (c > 0).astype(F32), (c < n_chunks - 1).astype(F32)
        parts = []
        for ref in (x_ref, b_ref, c_ref):
            prev = (rows(ref, p0, HALO).astype(F32) * pf).astype(BF16)
            nxt = (rows(ref, n0, HALO).astype(F32) * nf).astype(BF16)
            parts.append(jnp.concatenate([prev, rows(ref, start, q), nxt], axis=0))
        ext = jnp.concatenate(parts, axis=1)
        cur = ext[HALO:HALO + q].astype(F32)
        acc = cur * cw_ref[2:3, :] + cb_ref[...]
        for n, k in enumerate((0, 1, 3, 4)):
            acc = acc + _dot(sh_ref[n], ext) * cw_ref[k:k + 1, :]
        return _silu(acc)

    def conv_chunk(seg, c):
        _, x_ref, b_ref, c_ref, _, act_ref, _, _, n_chunks = seg
        start = c * q if isinstance(c, int) else pl.multiple_of(c * q, q)
        act_ref[pl.ds(start, q), :] = conv_act(x_ref, b_ref, c_ref, n_chunks, c).astype(BF16)

    def chunk(seg, c, d, second):
        z_ref, x_ref, b_ref, c_ref, dt_ref, act_ref, yf_ref, y_ref, n_chunks = seg
        st_ref = st_refs[d]
        start = c * q if isinstance(c, int) else pl.multiple_of(c * q, q)
        act = act_ref[pl.ds(start, q), :]
        xs = act[:, 0:MIX_W].astype(F32)
        bm = act[:, MIX_W:2 * MIX_W]
        cm = act[:, 2 * MIX_W:3 * MIX_W]

        dt_n = jax.nn.softplus(dt_ref[pl.ds(start, q), :] + bias_ref[...])
        la_n = dt_n * an_ref[...]
        hi, mid, lo = _split3(la_n)
        tri = tri_ref[d]
        cs_n = _dot(tri, hi) + _dot(tri, mid) + _dot(tri, lo)
        cs_t = cs_n.T
        edge = q - 1 if d == 0 else 0
        tri_mask = (col_i <= row_i) if d == 0 else (col_i >= row_i)

        tot_n = cs_n[edge:edge + 1, :]
        spread = e_ref[d]
        dt_full = _dot(dt_n.astype(BF16), spread)
        dtdte_full = _dot((dt_n * jnp.exp(tot_n - cs_n)).astype(BF16), spread)
        ecs_full = _dot(jnp.exp(cs_n).astype(BF16), spread)
        etot_n = jnp.exp(tot_n)
        tot_full = jnp.zeros((1, MIX_W), F32)
        decay = []
        for h in range(N_HEADS):
            sel = lane128 == (N_HEADS * d + h)
            cs_col = jnp.sum(jnp.where(sel, cs_n, 0.0), axis=-1, keepdims=True)
            cs_row = cs_t[N_HEADS * d + h:N_HEADS * d + h + 1, :]
            decay.append(jnp.where(tri_mask, jnp.exp(cs_col - cs_row), 0.0))
            etot = jnp.sum(jnp.where(sel[0:1, :], etot_n, 0.0), axis=-1, keepdims=True)
            tot_full = jnp.where(head_masks_r[h], etot, tot_full)

        xdt = xs * dt_full
        state = st_ref[...]
        y = jnp.zeros((q, MIX_W), F32)
        y_off = jnp.zeros((q, MIX_W), F32)
        upd = jnp.zeros((SSD_STATE, MIX_W), F32)
        xdte = xs * dtdte_full
        for g in range(2):
            bg = bm[:, 128 * g:128 * (g + 1)]
            cg = cm[:, 128 * g:128 * (g + 1)]
            cb = _dot_nt(cg, bg)
            for h in (2 * g, 2 * g + 1):
                m = (cb * decay[h]).astype(BF16)
                y = y + _dot(m, jnp.where(head_masks[h], xdt, 0.0).astype(BF16))
            y_off = y_off + _dot(cg, jnp.where(state_group_masks[g], state, 0.0).astype(BF16))
            upd = upd + _dot_tn(bg, jnp.where(group_masks[g], xdte, 0.0).astype(BF16))
        y = y + y_off * ecs_full
        st_ref[...] = state * tot_full + upd

        if not second:
            yf_ref[pl.ds(start, q), :] = y
        else:
            y = y + yf_ref[pl.ds(start, q), :] + dsk_ref[...] * xs
            y = y * _silu(z_ref[pl.ds(start, q), :].astype(F32))
            ms = jnp.mean(y * y, axis=-1, keepdims=True)
            y_ref[pl.ds(start, q), :] = (y * lax.rsqrt(ms + EPS) * ng_ref[...]).astype(BF16)

    seg_l = (zl_ref, xl_ref, bl_ref, cl_ref, dtl_ref, act_l, yf_l, yl_ref, n_lat)
    seg_c = (zc_ref, xc_ref, bc_ref, cc_ref, dtc_ref, act_c, yf_c, yc_ref, n_ctx)
    st_refs = (stf_ref, stb_ref)
    stf_ref[...] = jnp.zeros_like(stf_ref)
    stb_ref[...] = jnp.zeros_like(stb_ref)

    for c in range(n_ctx):
        conv_chunk(seg_c, c)

    def conv_body(i, carry):
        conv_chunk(seg_l, i)
        return carry

    lax.fori_loop(0, n_lat, conv_body, 0)

    for i in range(n_ctx):
        chunk(seg_c, i, 0, 2 * i >= n_ctx)
        chunk(seg_c, n_ctx - 1 - i, 1, 2 * i >= n_ctx - 1)

    def pair(second):
        def body(i, carry):
            chunk(seg_l, i, 0, second)
            chunk(seg_l, n_lat - 1 - i, 1, second)
            return carry
        return body

    lax.fori_loop(0, n_lat // 2, pair(False), 0, unroll=2)
    lax.fori_loop(n_lat // 2, n_lat, pair(True), 0, unroll=2)


def _ssd(p, pdt, conv_w, conv_b, a_log, dt_bias, d_skip, norm_g, bsz, seq, ctx):
    t = p.shape[0]
    n_lat, n_ctx = seq // CHUNK, ctx // CHUNK
    assert n_lat % 2 == 0
    cb0 = (bsz * seq) // ctx
    tri, shifts, spread = _ssd_consts()
    cw = jnp.zeros((8, 3 * MIX_W), F32).at[:SSD_CONV].set(conv_w)
    a_n = jnp.zeros((1, 128), F32).at[0, :8].set(-jnp.exp(a_log.astype(F32)).reshape(8))
    bias_n = jnp.zeros((1, 128), F32).at[0, :8].set(dt_bias.astype(F32).reshape(8))
    dsk = jnp.repeat(d_skip.astype(F32), HEAD_DIM).reshape(1, MIX_W)

    def lat(col):
        return pl.BlockSpec((seq, MIX_W), lambda b, col=col: (b, col // MIX_W))

    def cx(col):
        return pl.BlockSpec((ctx, MIX_W), lambda b, col=col: (cb0 + b, col // MIX_W))

    def full(shape):
        return pl.BlockSpec(shape, lambda b: (0,) * len(shape))

    kern = functools.partial(_ssd_kernel, n_lat=n_lat, n_ctx=n_ctx)
    yl, yc = pl.pallas_call(
        kern,
        out_shape=(jax.ShapeDtypeStruct((bsz * seq, MIX_W), BF16),
                   jax.ShapeDtypeStruct((bsz * ctx, MIX_W), BF16)),
        grid=(bsz,),
        in_specs=[lat(C_Z), lat(C_AX), lat(C_AB), lat(C_AC),
                  pl.BlockSpec((seq, 128), lambda b: (b, 0)),
                  cx(C_Z), cx(C_AX), cx(C_AB), cx(C_AC),
                  pl.BlockSpec((ctx, 128), lambda b: (cb0 + b, 0)),
                  full((8, 3 * MIX_W)), full((1, 3 * MIX_W)), full((1, 128)), full((1, 128)),
                  full((1, MIX_W)), full((1, MIX_W)), full((2, CHUNK, CHUNK)),
                  full((4, CHUNK, CHUNK + 2 * HALO)), full((2, 128, MIX_W))],
        out_specs=(pl.BlockSpec((seq, MIX_W), lambda b: (b, 0)),
                   pl.BlockSpec((ctx, MIX_W), lambda b: (b, 0))),
        scratch_shapes=[pltpu.VMEM((seq, 3 * MIX_W), BF16), pltpu.VMEM((ctx, 3 * MIX_W), BF16),
                        pltpu.VMEM((seq, MIX_W), F32), pltpu.VMEM((ctx, MIX_W), F32),
                        pltpu.VMEM((SSD_STATE, MIX_W), F32), pltpu.VMEM((SSD_STATE, MIX_W), F32)],
        compiler_params=_cp("parallel"),
        name="ssd",
    )(p, p, p, p, pdt, p, p, p, p, pdt, cw, conv_b.reshape(1, -1).astype(F32), a_n, bias_n,
      dsk, norm_g.reshape(1, MIX_W).astype(F32), tri, shifts, spread)
    return yl, yc


def _s5_discretize(lam_re, lam_im, log_step, b_re, b_im):
    step = jnp.exp(log_step.astype(F32))[:, None]
    lr = jnp.minimum(lam_re.astype(F32), -1e-4)
    li = lam_im.astype(F32)
    mag = jnp.exp(lr * step)
    ang = li * step
    ab_re, ab_im = mag * jnp.cos(ang), mag * jnp.sin(ang)
    den = lr * lr + li * li
    f_re = ((ab_re - 1.0) * lr + ab_im * li) / den
    f_im = (ab_im * lr - (ab_re - 1.0) * li) / den
    br, bi = b_re.astype(F32), b_im.astype(F32)
    bb_re = f_re[..., None] * br - f_im[..., None] * bi
    bb_im = f_re[..., None] * bi + f_im[..., None] * br
    return ab_re, ab_im, bb_re, bb_im


def _s5_mats(lam_re, lam_im, log_step, b_re, b_im, c_re, c_im):
    eye = jnp.eye(S5_GROUPS, dtype=F32)
    a_all, b_all, c_all = [], [], []
    for d in range(2):
        ab_re, ab_im, bb_re, bb_im = _s5_discretize(lam_re[d], lam_im[d], log_step[d], b_re[d], b_im[d])
        bm = [jnp.einsum('gnp,gh->gphn', m, eye).reshape(MIX_W, S5_LANES) for m in (bb_re, bb_im)]
        cm = [jnp.einsum('gpn,gh->gnhp', m.astype(F32), eye).reshape(S5_LANES, MIX_W)
              for m in (c_re[d], c_im[d])]
        b_all.append(jnp.concatenate(bm, axis=1))
        c_all.append(jnp.concatenate([cm[0], -cm[1]], axis=0))
        a_all.append(jnp.concatenate([ab_re.reshape(1, S5_LANES), ab_im.reshape(1, S5_LANES)], axis=1))
    a = jnp.broadcast_to(jnp.stack(a_all), (2, 8, 2 * S5_LANES))
    return a, jnp.stack(b_all).astype(BF16), jnp.stack(c_all).astype(BF16)


def _s5_kernel(uf_ref, ub_ref, a_ref, b_ref, c_ref, yf_ref, yb_ref, buf_ref, s_ref):
    n = S5_LANES

    @pl.when(pl.program_id(0) == 0)
    def _():
        s_ref[...] = jnp.zeros_like(s_ref)

    for d, u_ref in enumerate((uf_ref, ub_ref)):
        buf_ref[d] = _dot(u_ref[...], b_ref[d])
    state = [(s_ref[d, :, 0:n], s_ref[d, :, n:2 * n]) for d in range(2)]
    for j in range(S5_STEPS):
        for d in range(2):
            jj = j if d == 0 else S5_STEPS - 1 - j
            rows = slice(jj * 8, jj * 8 + 8)
            a_re, a_im = a_ref[d, :, 0:n], a_ref[d, :, n:2 * n]
            s_re, s_im = state[d]
            n_re = a_re * s_re - a_im * s_im + buf_ref[d, rows, 0:n]
            n_im = a_re * s_im + a_im * s_re + buf_ref[d, rows, n:2 * n]
            buf_ref[d, rows, 0:n] = n_re
            buf_ref[d, rows, n:2 * n] = n_im
            state[d] = (n_re, n_im)
    for d, y_ref in enumerate((yf_ref, yb_ref)):
        s_ref[d, :, 0:n] = state[d][0]
        s_ref[d, :, n:2 * n] = state[d][1]
        y_ref[...] = _dot(buf_ref[d].astype(BF16), c_ref[d])


def _s5_finish_kernel(yf_ref, yb_ref, u_ref, d_ref, w_ref, o_ref):
    y = yf_ref[...] + yb_ref[...] + d_ref[...] * u_ref[...].astype(F32)
    v = jax.nn.gelu(y, approximate=True).astype(BF16)
    r = _dot(v, w_ref[...])
    o_ref[...] = (r[:, 0:MIX_W] * jax.nn.sigmoid(r[:, MIX_W:2 * MIX_W])).astype(BF16)


def _s5(p, lam_re, lam_im, log_step, b_re, b_im, c_re, c_im, d_skip, glu_w, bsz, seq, ctx):
    assert bsz <= 8
    t_lat = bsz * seq
    u = p[:, C_BU:C_BU + MIX_W]
    u_l = jnp.transpose(u[:t_lat].reshape(bsz, seq, MIX_W), (1, 0, 2))
    u_c = jnp.transpose(u[t_lat:].reshape(bsz, ctx, MIX_W), (1, 0, 2))
    u_tm = jnp.concatenate([u_c, u_l], axis=0)
    if bsz < 8:
        u_tm = jnp.pad(u_tm, ((0, 0), (0, 8 - bsz), (0, 0)))
    steps = seq + ctx
    u_tm = u_tm.reshape(steps * 8, MIX_W)
    a, bmat, cmat = _s5_mats(lam_re, lam_im, log_step, b_re, b_im, c_re, c_im)
    rows = S5_STEPS * 8
    nc, ncc = steps // S5_STEPS, ctx // S5_STEPS

    def bwd(i):
        return jnp.where(i < ncc, ncc - 1 - i, nc + ncc - 1 - i)

    def whole(shape):
        return pl.BlockSpec(shape, lambda i: (0,) * len(shape))

    yshape = jax.ShapeDtypeStruct((steps * 8, MIX_W), F32)
    yf, yb = pl.pallas_call(
        _s5_kernel,
        out_shape=(yshape, yshape),
        grid=(nc,),
        in_specs=[pl.BlockSpec((rows, MIX_W), lambda i: (i, 0)),
                  pl.BlockSpec((rows, MIX_W), lambda i: (bwd(i), 0)),
                  whole((2, 8, 2 * S5_LANES)), whole((2, MIX_W, 2 * S5_LANES)),
                  whole((2, 2 * S5_LANES, MIX_W))],
        out_specs=(pl.BlockSpec((rows, MIX_W), lambda i: (i, 0)),
                   pl.BlockSpec((rows, MIX_W), lambda i: (bwd(i), 0))),
        scratch_shapes=[pltpu.VMEM((2, rows, 2 * S5_LANES), F32), pltpu.VMEM((2, 8, 2 * S5_LANES), F32)],
        compiler_params=_cp("arbitrary"),
        name="s5_scan",
    )(u_tm, u_tm, a, bmat, cmat)

    tmf = _pow2_tile(2048, steps * 8)
    o = pl.pallas_call(
        _s5_finish_kernel,
        out_shape=jax.ShapeDtypeStruct((steps * 8, MIX_W), BF16),
        grid=(steps * 8 // tmf,),
        in_specs=[pl.BlockSpec((tmf, MIX_W), lambda i: (i, 0)),
                  pl.BlockSpec((tmf, MIX_W), lambda i: (i, 0)),
                  pl.BlockSpec((tmf, MIX_W), lambda i: (i, 0)),
                  pl.BlockSpec((1, MIX_W), lambda i: (0, 0)),
                  pl.BlockSpec((MIX_W, 2 * MIX_W), lambda i: (0, 0))],
        out_specs=pl.BlockSpec((tmf, MIX_W), lambda i: (i, 0)),
        compiler_params=_cp("parallel"),
        name="s5_finish",
    )(yf, yb, u_tm, d_skip.reshape(1, MIX_W).astype(F32), glu_w.astype(BF16))
    o = o.reshape(steps, 8, MIX_W)[:, :bsz]
    o_c = jnp.transpose(o[:ctx], (1, 0, 2)).reshape(bsz * ctx, MIX_W)
    o_l = jnp.transpose(o[ctx:], (1, 0, 2)).reshape(t_lat, MIX_W)
    return o_l, o_c


def _rope_tables(seq, tm):
    rows = seq // GRID_W
    pos_r = jnp.repeat(jnp.arange(rows, dtype=F32), GRID_W)
    pos_c = jnp.tile(jnp.arange(GRID_W, dtype=F32), rows)
    inv = ROPE_BASE ** (-jnp.arange(ROPE_FREQS, dtype=F32) / ROPE_FREQS)
    ar, ac = pos_r[:, None] * inv, pos_c[:, None] * inv
    cos = jnp.concatenate([jnp.cos(ar), jnp.cos(ar), jnp.cos(ac), jnp.cos(ac)], axis=-1)
    sin = jnp.concatenate([-jnp.sin(ar), jnp.sin(ar), -jnp.sin(ac), jnp.sin(ac)], axis=-1)
    cos = jnp.concatenate([jnp.tile(cos, (1, 2)), jnp.ones((tm, 128), F32)], axis=0)
    sin = jnp.concatenate([jnp.tile(sin, (1, 2)), jnp.zeros((tm, 128), F32)], axis=0)
    return cos, sin


def _prep_consts():
    i = np.arange(MIX_W)
    bd = ((i[:, None] // HEAD_DIM) == (i[None, :] // HEAD_DIM)).astype(np.float32) / HEAD_DIM
    pm = (i[:, None] == (i[None, :] ^ ROPE_FREQS)).astype(np.float32)
    return jnp.asarray(bd, BF16), jnp.asarray(pm, BF16)


def _prep_kernel(cq_ref, dq_ref, ck_ref, dk_ref, cos_ref, sin_ref, qg_ref, kg_ref, bd_ref, pm_ref,
                 q1_ref, q2_ref, k1_ref, k2_ref):
    cos, sin = cos_ref[...], sin_ref[...]
    cos2 = jnp.concatenate([cos, cos], axis=-1)
    sin2 = jnp.concatenate([sin, sin], axis=-1)
    bd, pm = bd_ref[...], pm_ref[...]
    tm = cos.shape[0]
    lane = lax.broadcasted_iota(jnp.int32, (tm, 128), 1)

    def rms(x, g, n):
        ms = _dot((x * x).astype(BF16), bd[:n, :n])
        return x * lax.rsqrt(ms + EPS) * g

    def rope(y, c, s, n):
        return y * c + _dot(y.astype(BF16), pm[:n, :n]) * s

    def store_q(q, ref):
        q = q * ATTN_SCALE
        for kv in range(2):
            for g in range(2):
                half = q[:, 128 * kv:128 * (kv + 1)]
                if g != kv:
                    half = pltpu.roll(half, HEAD_DIM, 1)
                keep = (lane >= HEAD_DIM * kv) & (lane < HEAD_DIM * (kv + 1))
                ref[2 * kv + g] = jnp.where(keep, half, 0.0).astype(BF16)

    store_q(rope(rms(cq_ref[...].astype(F32), qg_ref[...], MIX_W), cos2, sin2, MIX_W), q1_ref)
    store_q(rope(dq_ref[...].astype(F32), cos2, sin2, MIX_W), q2_ref)
    k1 = rope(rms(ck_ref[...].astype(F32), kg_ref[...], 128), cos, sin, 128)
    for u in range(tm // KT_UNIT):
        k1_ref[u] = k1[u * KT_UNIT:(u + 1) * KT_UNIT].T.astype(BF16)
    k2 = rope(dk_ref[...].astype(F32), cos, sin, 128)
    for u in range(tm // WINDOW):
        k2_ref[u] = k2[u * WINDOW:(u + 1) * WINDOW].T.astype(BF16)


def _prep(p, qk_gain, t_lat, seq, tm):
    t = p.shape[0]
    cos, sin = _rope_tables(seq, tm)
    bd, pm = _prep_consts()
    qg = jnp.tile(qk_gain[0].astype(F32), N_HEADS).reshape(1, MIX_W)
    kg = jnp.tile(qk_gain[1].astype(F32), 2).reshape(1, 128)
    nt = seq // tm

    def tab(i):
        return (jnp.where(i * tm >= t_lat, nt, i % nt), 0)

    def const(shape):
        return pl.BlockSpec(shape, lambda i: (0,) * len(shape))

    qshape = jax.ShapeDtypeStruct((N_HEADS, t, 128), BF16)
    ktshape = jax.ShapeDtypeStruct((t // KT_UNIT, 128, KT_UNIT), BF16)
    kt2shape = jax.ShapeDtypeStruct((t // WINDOW, 128, WINDOW), BF16)
    return pl.pallas_call(
        _prep_kernel,
        out_shape=(qshape, qshape, ktshape, kt2shape),
        grid=(t // tm,),
        in_specs=[pl.BlockSpec((tm, MIX_W), lambda i: (i, C_CQ // MIX_W)),
                  pl.BlockSpec((tm, MIX_W), lambda i: (i, C_DQ // MIX_W)),
                  pl.BlockSpec((tm, 128), lambda i: (i, C_CK // 128)),
                  pl.BlockSpec((tm, 128), lambda i: (i, C_DK // 128)),
                  pl.BlockSpec((tm, 128), tab), pl.BlockSpec((tm, 128), tab),
                  const((1, MIX_W)), const((1, 128)), const((MIX_W, MIX_W)), const((MIX_W, MIX_W))],
        out_specs=(pl.BlockSpec((N_HEADS, tm, 128), lambda i: (0, i, 0)),
                   pl.BlockSpec((N_HEADS, tm, 128), lambda i: (0, i, 0)),
                   pl.BlockSpec((tm // KT_UNIT, 128, KT_UNIT), lambda i: (i, 0, 0)),
                   pl.BlockSpec((tm // WINDOW, 128, WINDOW), lambda i: (i, 0, 0))),
        compiler_params=_cp("parallel"),
        name="qk_prep",
    )(p, p, p, p, cos, sin, qg, kg, bd, pm)


def _pack_heads(o, tq):
    lane = lax.broadcasted_iota(jnp.int32, (tq, 128), 1)
    left = lane < HEAD_DIM
    o00, o01, o10, o11 = [o[h * tq:(h + 1) * tq] for h in range(N_HEADS)]
    out0 = jnp.where(left, o00, pltpu.roll(o01, HEAD_DIM, 1))
    out1 = jnp.where(left, pltpu.roll(o10, HEAD_DIM, 1), o11)
    return jnp.concatenate([out0, out1], axis=-1)


def _gattn_kernel(q_ref, kl_ref, vl_ref, kc_ref, vc_ref, o_ref, m_ref, acc_ref,
                  *, tq, units, n_qb_lat, n_kvb):
    qb = pl.program_id(1)

    def tree(op, xs):
        while len(xs) > 1:
            xs = [op(xs[i], xs[i + 1]) for i in range(0, len(xs) - 1, 2)] + ([xs[-1]] if len(xs) % 2 else [])
        return xs[0]

    def scores(h, kts):
        q = q_ref[h]
        cols = []
        for kt in kts:
            s = _dot(q, kt)
            cols += [s[:, 128 * c:128 * (c + 1)] for c in range(s.shape[1] // 128)]
        return cols

    def softmax(h, cols, first):
        m_blk = jnp.max(tree(jnp.maximum, cols), axis=-1, keepdims=True)
        if first:
            m_new = jnp.broadcast_to(m_blk, (tq, 128))
            alpha = None
        else:
            m_old = m_ref[h]
            m_new = jnp.maximum(m_old, m_blk)
            alpha = jnp.exp(m_old - m_new)
        m_ref[h] = m_new
        return alpha, jnp.concatenate([jnp.exp((c - m_new).astype(BF16)) for c in cols], axis=-1)

    def weighted(h, alpha, p, v):
        pv = _dot(p, v)
        acc_ref[h] = pv if alpha is None else jnp.concatenate([alpha, alpha], axis=-1) * acc_ref[h] + pv

    def block(kts, v, first):
        v = jnp.concatenate([v, jnp.ones_like(v)], axis=-1)
        cols = scores(0, kts)
        for h in range(N_HEADS):
            nxt = scores(h + 1, kts) if h + 1 < N_HEADS else None
            alpha, p = softmax(h, cols, first)
            weighted(h, alpha, p, v)
            cols = nxt

    block([kc_ref[u] for u in range(kc_ref.shape[0])], vc_ref[...], True)

    def body(j, carry):
        rows = pl.ds(pl.multiple_of(j * (units * KT_UNIT), units * KT_UNIT), units * KT_UNIT)
        block([kl_ref[j * units + u] for u in range(units)], vl_ref[rows, :], False)
        return carry

    lax.fori_loop(0, jnp.where(qb < n_qb_lat, n_kvb, 0), body, 0)
    o = [acc_ref[h, :, 0:128] / acc_ref[h, :, 128:256] for h in range(N_HEADS)]
    o_ref[...] = _pack_heads(jnp.concatenate(o, axis=0), tq).astype(BF16)


def _gattn(qp, kt, p, bsz, seq, ctx, with_ctx):
    t_lat = bsz * seq
    tq = ctx
    units = min(8, seq // KT_UNIT)
    n_qb_lat = seq // tq
    n_qb = n_qb_lat + (1 if with_ctx else 0)
    cb0 = t_lat // ctx
    t_out = t_lat + (bsz * ctx if with_ctx else 0)
    assert ctx % KT_UNIT == 0 and seq % (units * KT_UNIT) == 0

    def qrow(b, i):
        return jnp.where(i < n_qb_lat, b * n_qb_lat + i, cb0 + b)

    kern = functools.partial(_gattn_kernel, tq=tq, units=units, n_qb_lat=n_qb_lat,
                             n_kvb=seq // (units * KT_UNIT))
    return pl.pallas_call(
        kern,
        out_shape=jax.ShapeDtypeStruct((t_out, MIX_W), BF16),
        grid=(bsz, n_qb),
        in_specs=[pl.BlockSpec((N_HEADS, tq, 128), lambda b, i: (0, qrow(b, i), 0)),
                  pl.BlockSpec((seq // KT_UNIT, 128, KT_UNIT), lambda b, i: (b, 0, 0)),
                  pl.BlockSpec((seq, 128), lambda b, i: (b, C_CV // 128)),
                  pl.BlockSpec((ctx // KT_UNIT, 128, KT_UNIT), lambda b, i: (cb0 + b, 0, 0)),
                  pl.BlockSpec((ctx, 128), lambda b, i: (cb0 + b, C_CV // 128))],
        out_specs=pl.BlockSpec((tq, MIX_W), lambda b, i: (qrow(b, i), 0)),
        scratch_shapes=[pltpu.VMEM((N_HEADS, tq, 128), F32), pltpu.VMEM((N_HEADS, tq, 256), F32)],
        compiler_params=_cp("parallel", "arbitrary"),
        name="global_attn",
    )(qp, kt, p, kt, p)


def _wattn_kernel(q_ref, kl_ref, vl_ref, kc_ref, vc_ref, sink_ref, o_ref, *, nb):
    w = WINDOW
    n = pl.program_id(1)
    is_lat = n < nb
    rows = 2 * w
    qi = lax.broadcasted_iota(jnp.int32, (rows, w), 0) & (w - 1)
    kj = lax.broadcasted_iota(jnp.int32, (rows, w), 1)
    band = (jnp.clip(n - 1, 0, nb - 1), jnp.clip(n, 0, nb - 1), jnp.clip(n + 1, 0, nb - 1))
    off_prev = jnp.where(is_lat & (n >= 1), 0, w)
    off_cur = jnp.where(is_lat, 0, w)
    off_next = jnp.where(n + 1 < nb, 0, w)
    masks = [kj >= qi + off_prev, kj >= off_cur, kj <= qi - off_next]
    kts = [kc_ref[u] for u in range(kc_ref.shape[0])] + [kl_ref[i] for i in band]
    n_ctx_tiles = kc_ref.shape[0]
    v_all = jnp.concatenate([vc_ref[...]] + [vl_ref[pl.ds(pl.multiple_of(i * w, w), w), :] for i in band],
                            axis=0)

    def tree(op, xs):
        while len(xs) > 1:
            xs = [op(xs[i], xs[i + 1]) for i in range(0, len(xs) - 1, 2)] + ([xs[-1]] if len(xs) % 2 else [])
        return xs[0]

    def scores(c):
        q = q_ref[2 * c:2 * c + 2].reshape(rows, 128)
        tiles = [_dot(q, kt) for kt in kts]
        return tiles[:n_ctx_tiles] + [jnp.where(mk, t, NEG_INF) for mk, t in zip(masks, tiles[n_ctx_tiles:])]

    def finish(c, tiles):
        sink = sink_ref[c * rows:(c + 1) * rows, :]
        m = jnp.maximum(jnp.max(tree(jnp.maximum, tiles), axis=-1, keepdims=True), sink)
        ps = [jnp.exp(t - m) for t in tiles]
        den = jnp.sum(tree(jnp.add, ps), axis=-1, keepdims=True) + jnp.exp(sink - m)
        return _dot(jnp.concatenate(ps, axis=-1).astype(BF16), v_all) / den

    tiles = scores(0)
    nxt = scores(1)
    o = [finish(0, tiles), finish(1, nxt)]
    o_ref[...] = _pack_heads(jnp.concatenate(o, axis=0), w).astype(BF16)


def _wattn(qp, k, p, sink, bsz, seq, ctx, with_ctx):
    t_lat = bsz * seq
    w = WINDOW
    nb = seq // w
    ncb = ctx // w
    n_qb = nb + (ncb if with_ctx else 0)
    cq0 = t_lat // w
    cb0 = t_lat // ctx
    t_out = t_lat + (bsz * ctx if with_ctx else 0)
    sink_rows = jnp.broadcast_to(jnp.repeat(sink.astype(F32), w)[:, None], (N_HEADS * w, 128))

    def qrow(b, i):
        return jnp.where(i < nb, b * nb + i, cq0 + b * ncb + (i - nb))

    return pl.pallas_call(
        functools.partial(_wattn_kernel, nb=nb),
        out_shape=jax.ShapeDtypeStruct((t_out, MIX_W), BF16),
        grid=(bsz, n_qb),
        in_specs=[pl.BlockSpec((N_HEADS, w, 128), lambda b, i: (0, qrow(b, i), 0)),
                  pl.BlockSpec((seq // w, 128, w), lambda b, i: (b, 0, 0)),
                  pl.BlockSpec((seq, 128), lambda b, i: (b, C_DV // 128)),
                  pl.BlockSpec((ctx // w, 128, w), lambda b, i: (cb0 + b, 0, 0)),
                  pl.BlockSpec((ctx, 128), lambda b, i: (cb0 + b, C_DV // 128)),
                  pl.BlockSpec((N_HEADS * w, 128), lambda b, i: (0, 0))],
        out_specs=pl.BlockSpec((w, MIX_W), lambda b, i: (qrow(b, i), 0)),
        compiler_params=_cp("parallel", "arbitrary"),
        name="window_attn",
    )(qp, k, p, k, p, sink_rows)


def _merge_kernel(yal_ref, yac_ref, ybl_ref, ybc_ref, yc_ref, yd_ref, gate_ref, x_ref, mod_ref, wbr_ref,
                  wout_ref, g2_ref, xo_ref, h2_ref, *maybe_tok_ref, n_lat_tiles):
    tm, d = x_ref.shape
    ctx_rows = jnp.full((tm, MIX_W), pl.program_id(0), jnp.int32) >= n_lat_tiles
    ya = jnp.where(ctx_rows, yac_ref[...], yal_ref[...])
    yb = jnp.where(ctx_rows, ybc_ref[...], ybl_ref[...])
    acc = None
    for n, y in enumerate((ya, yb, yc_ref[...], yd_ref[...])):
        gate = jax.nn.sigmoid(gate_ref[:, n * d:(n + 1) * d].astype(F32))
        term = gate * _dot(y, wbr_ref[n])
        acc = term if acc is None else acc + term
    x = x_ref[...] + mod_ref[0, 2:3, :] * _dot(acc.astype(BF16), wout_ref[...])
    xo_ref[...] = x
    ms = jnp.mean(x * x, axis=-1, keepdims=True)
    y = x * lax.rsqrt(ms + EPS) * g2_ref[...]
    h2 = y * (1.0 + mod_ref[0, 4:5, :]) + mod_ref[0, 3:4, :]
    h2_ref[...] = h2
    if maybe_tok_ref:
        _to_token_tiles(maybe_tok_ref[0], h2)


def _to_token_tiles(ref, x):
    rows, d = x.shape
    k = d // 128
    for c in range(k):
        ref[pl.ds(c, rows, stride=k), :] = x[:, 128 * c:128 * (c + 1)]


def _from_token_tiles(ref, rows, d):
    k = d // 128
    return jnp.concatenate([ref[pl.ds(c, rows, stride=k), :] for c in range(k)], axis=-1)


def _merge(ys, p, x, mod, wbr, wout, g2, t_out, t_lat, seq, tm, token_tiles):
    d = x.shape[1]
    k = d // 128

    def row(width):
        return pl.BlockSpec((tm, width), lambda i: (i, 0))

    def const(shape):
        return pl.BlockSpec(shape, lambda i: (0,) * len(shape))

    out_shape = [jax.ShapeDtypeStruct((t_out, d), F32), jax.ShapeDtypeStruct((t_out, d), F32)]
    out_specs = [row(d), row(d)]
    if token_tiles:
        out_shape.append(jax.ShapeDtypeStruct((t_out * k, 128), F32))
        out_specs.append(pl.BlockSpec((tm * k, 128), lambda i: (i, 0)))
    (ya_l, ya_c), (yb_l, yb_c), yc, yd = ys
    n_lat_tiles = t_lat // tm
    n_ctx_tiles = ya_c.shape[0] // tm
    lat = pl.BlockSpec((tm, MIX_W), lambda i: (jnp.minimum(i, n_lat_tiles - 1), 0))
    cx = pl.BlockSpec((tm, MIX_W), lambda i: (jnp.clip(i - n_lat_tiles, 0, n_ctx_tiles - 1), 0))
    return pl.pallas_call(
        functools.partial(_merge_kernel, n_lat_tiles=n_lat_tiles),
        out_shape=tuple(out_shape),
        grid=(t_out // tm,),
        in_specs=[lat, cx, lat, cx, row(MIX_W), row(MIX_W), row(4 * d), row(d),
                  pl.BlockSpec((1, 6, d), lambda i: (_mod_group(i * tm, t_lat, seq), 0, 0)),
                  const((4, MIX_W, d)), const((d, d)), const((1, d))],
        out_specs=tuple(out_specs),
        compiler_params=_cp("parallel"),
        name="merge",
    )(ya_l, ya_c, yb_l, yb_c, yc, yd, p, x, mod, wbr, wout, g2.reshape(1, d))


def _ffn_dense_kernel(h_ref, x_ref, mod_ref, wg_ref, wu_ref, wo_ref, o_ref, hb_ref, acc_ref):
    j = pl.program_id(1)

    @pl.when(j == 0)
    def _():
        hb_ref[...] = h_ref[...].astype(BF16)
        acc_ref[...] = jnp.zeros_like(acc_ref)

    h = hb_ref[...]
    a = _silu(_dot(h, wg_ref[...])) * _dot(h, wu_ref[...])
    acc_ref[...] += _dot(a.astype(BF16), wo_ref[...])

    @pl.when(j == pl.num_programs(1) - 1)
    def _():
        o_ref[...] = x_ref[...] + mod_ref[0, 5:6, :] * acc_ref[...]


def _ffn_dense(h2, x, mod, w_in, w_out, t_lat, seq, tm):
    t, d = x.shape
    f = w_out.shape[0]
    tf = 256
    nf = f // tf
    return pl.pallas_call(
        _ffn_dense_kernel,
        out_shape=jax.ShapeDtypeStruct((t, d), F32),
        grid=(t // tm, nf),
        in_specs=[pl.BlockSpec((tm, d), lambda i, j: (i, 0)),
                  pl.BlockSpec((tm, d), lambda i, j: (i, 0)),
                  pl.BlockSpec((1, 6, d), lambda i, j: (_mod_group(i * tm, t_lat, seq), 0, 0)),
                  pl.BlockSpec((d, tf), lambda i, j: (0, j)),
                  pl.BlockSpec((d, tf), lambda i, j: (0, nf + j)),
                  pl.BlockSpec((tf, d), lambda i, j: (j, 0))],
        out_specs=pl.BlockSpec((tm, d), lambda i, j: (i, 0)),
        scratch_shapes=[pltpu.VMEM((tm, d), BF16), pltpu.VMEM((tm, d), F32)],
        compiler_params=_cp("parallel", "arbitrary"),
        name="ffn_dense",
    )(h2, x, mod, w_in, w_in, w_out)


def _router_kernel(h_ref, w_ref, e_ref, g1_ref, g2_ref):
    h = h_ref[...]
    h_hi = h.astype(BF16)
    h_lo = (h - h_hi.astype(F32)).astype(BF16)
    logits = _dot(h_hi, w_ref[0]) + _dot(h_lo, w_ref[0]) + _dot(h_hi, w_ref[1])
    lane = lax.broadcasted_iota(jnp.int32, logits.shape, 1)
    lane_f = lane.astype(F32)
    logits = jnp.where(lane < N_EXPERTS, logits, -jnp.inf)
    m1 = jnp.max(logits, axis=-1, keepdims=True)
    i1 = jnp.min(jnp.where(logits == m1, lane_f, 128.0), axis=-1, keepdims=True)
    rest = jnp.where(lane_f == i1, -jnp.inf, logits)
    m2 = jnp.max(rest, axis=-1, keepdims=True)
    i2 = jnp.min(jnp.where(rest == m2, lane_f, 128.0), axis=-1, keepdims=True)
    e2 = jnp.exp(m2 - m1)
    g1 = 1.0 / (1.0 + e2)
    e_ref[...] = jnp.where(lane == 0, i1, jnp.where(lane == 1, i2, 0.0)).astype(jnp.int32)
    g1_ref[...] = jnp.broadcast_to(g1, g1_ref.shape)
    g2_ref[...] = jnp.broadcast_to(e2 * g1, g2_ref.shape)


def _router(h2, router, tm):
    t, d = h2.shape
    r = jnp.zeros((d, 128), F32).at[:, :N_EXPERTS].set(router.astype(F32))
    r_hi = r.astype(BF16)
    r_lo = (r - r_hi.astype(F32)).astype(BF16)
    shp = jax.ShapeDtypeStruct((t, 128), F32)
    return pl.pallas_call(
        _router_kernel,
        out_shape=(jax.ShapeDtypeStruct((t, 128), jnp.int32), shp, shp),
        grid=(t // tm,),
        in_specs=[pl.BlockSpec((tm, d), lambda i: (i, 0)),
                  pl.BlockSpec((2, d, 128), lambda i: (0, 0, 0))],
        out_specs=(pl.BlockSpec((tm, 128), lambda i: (i, 0)),) * 3,
        compiler_params=_cp("parallel"),
        name="router",
    )(h2, jnp.stack([r_hi, r_lo]))


def _experts_kernel(te_ref, tok0_ref, tokn_ref, dst_ref, h_ref, wg_ref, wu_ref, wo_ref, y_ref,
                    xbuf, ybuf, xb_ref, acc_ref, sem_in, sem_out, *, n_tiles, tm):
    i, j = pl.program_id(0), pl.program_id(1)
    last_j = pl.num_programs(1) - 1
    slot = i % 2
    other = 1 - slot
    per_step = tm // MOE_STEPS
    tr = TOK_ROWS

    def tok(ref, t):
        return ref.at[pl.ds(pl.multiple_of(t * tr, tr), tr)]

    def gather(idx_ref, r, s):
        return pltpu.make_async_copy(tok(h_ref, idx_ref[0, 0, r]), tok(xbuf.at[s], r), sem_in.at[s])

    def scatter(r, s):
        return pltpu.make_async_copy(tok(ybuf.at[s], r), tok(y_ref, dst_ref[0, 0, r]), sem_out.at[s])

    def wait_tile(copy_of_row0):
        def body(k, carry):
            for _ in range(MOE_WAITS):
                copy_of_row0().wait()
            return carry

        lax.fori_loop(0, tm // MOE_WAITS, body, 0)

    def wait_gather(s):
        wait_tile(lambda: pltpu.make_async_copy(tok(h_ref, 0), tok(xbuf.at[s], 0), sem_in.at[s]))

    def wait_scatter(s):
        wait_tile(lambda: pltpu.make_async_copy(tok(ybuf.at[s], 0), tok(y_ref, 0), sem_out.at[s]))

    def row_traffic(with_gather, with_scatter):
        def rows(base, count):
            for k in range(count):
                if with_gather:
                    gather(tokn_ref, base + k, other).start(priority=1)
                if with_scatter:
                    scatter(base + k, other).start(priority=1)

        rows(j * per_step, per_step)

        @pl.when(j == last_j)
        def _():
            rows(per_step * MOE_STEPS, tm - per_step * MOE_STEPS)

    def compute():
        x = xb_ref[...]
        a = _silu(_dot(x, wg_ref[0])) * _dot(x, wu_ref[0])
        acc_ref[...] += _dot(a.astype(BF16), wo_ref[0])

    @pl.when((i == 0) & (j == 0))
    def _():
        def body(r, carry):
            gather(tok0_ref, r, 0).start()
            return carry

        lax.fori_loop(0, tm, body, 0)

    @pl.when(j == 0)
    def _():
        wait_gather(slot)

        @pl.when(i < n_tiles)
        def _():
            xb_ref[...] = _from_token_tiles(xbuf.at[slot], tm, xb_ref.shape[1]).astype(BF16)
            acc_ref[...] = jnp.zeros_like(acc_ref)

    @pl.when(i == 0)
    def _():
        compute()
        row_traffic(True, False)

    @pl.when((i > 0) & (i < n_tiles))
    def _():
        compute()
        row_traffic(True, True)

    @pl.when(i == n_tiles)
    def _():
        row_traffic(False, True)

    @pl.when(j == last_j)
    def _():
        @pl.when((i >= 2) & (i < n_tiles))
        def _():
            wait_scatter(slot)

        @pl.when(i < n_tiles)
        def _():
            _to_token_tiles(ybuf.at[slot], acc_ref[...])

        @pl.when(i == n_tiles)
        def _():
            wait_scatter(slot)
            wait_scatter(other)


def _experts(h2_tiles, src_tok, dst_row, tile_e, w_in, w_out, n_out_rows):
    d = w_in.shape[1]
    assert d == TOK_ROWS * 128
    n_tiles = dst_row.shape[0]
    f = w_out.shape[1]
    tf = f // MOE_STEPS
    tm = MOE_TILE
    assert n_tiles >= 2 and tm % MOE_WAITS == 0 and tf % 128 == 0

    def smem(index_map):
        return pl.BlockSpec((1, 1, tm), index_map, memory_space=pltpu.SMEM)

    grid_spec = pltpu.PrefetchScalarGridSpec(
        num_scalar_prefetch=1,
        grid=(n_tiles + 1, MOE_STEPS),
        in_specs=[smem(lambda i, j, te: (0, 0, 0)),
                  smem(lambda i, j, te: (jnp.minimum(i + 1, n_tiles), 0, 0)),
                  smem(lambda i, j, te: (jnp.clip(i - 1, 0, n_tiles - 1), 0, 0)),
                  pl.BlockSpec(memory_space=pl.ANY),
                  pl.BlockSpec((1, d, tf), lambda i, j, te: (te[i], 0, j)),
                  pl.BlockSpec((1, d, tf), lambda i, j, te: (te[i], 0, MOE_STEPS + j)),
                  pl.BlockSpec((1, tf, d), lambda i, j, te: (te[i], j, 0))],
        out_specs=pl.BlockSpec(memory_space=pl.ANY),
        scratch_shapes=[pltpu.VMEM((2, tm * TOK_ROWS, 128), F32), pltpu.VMEM((2, tm * TOK_ROWS, 128), F32),
                        pltpu.VMEM((tm, d), BF16), pltpu.VMEM((tm, d), F32),
                        pltpu.SemaphoreType.DMA((2,)), pltpu.SemaphoreType.DMA((2,))])
    return pl.pallas_call(
        functools.partial(_experts_kernel, n_tiles=n_tiles, tm=tm),
        out_shape=jax.ShapeDtypeStruct((n_out_rows * TOK_ROWS, 128), F32),
        grid_spec=grid_spec,
        compiler_params=_cp("arbitrary", "arbitrary"),
        name="experts",
    )(tile_e, src_tok, src_tok, dst_row, h2_tiles, w_in, w_in, w_out)


def _combine_kernel(y1_ref, y2_ref, x_ref, mod_ref, g1_ref, g2_ref, *rest):
    o_ref = rest[-1]
    tm, d = x_ref.shape
    g1 = jnp.concatenate([g1_ref[...]] * (d // 128), axis=-1)
    g2 = jnp.concatenate([g2_ref[...]] * (d // 128), axis=-1)
    y = g1 * _from_token_tiles(y1_ref, tm, d) + g2 * _from_token_tiles(y2_ref, tm, d)
    x = x_ref[...] + mod_ref[0, 5:6, :] * y
    if len(rest) == 2:
        ms = jnp.mean(x * x, axis=-1, keepdims=True)
        x = x * lax.rsqrt(ms + EPS) * rest[0][...]
    o_ref[...] = x


def _combine(y_tiles, x, mod, g1, g2, t_lat, seq, tm, final_g):
    t, d = x.shape
    k = d // 128
    extra_specs, extra_args = [], []
    if final_g is not None:
        extra_specs, extra_args = [pl.BlockSpec((1, d), lambda i: (0, 0))], [final_g.reshape(1, d)]
    return pl.pallas_call(
        _combine_kernel,
        out_shape=jax.ShapeDtypeStruct((t, d), F32),
        grid=(t // tm,),
        in_specs=[pl.BlockSpec((tm * k, 128), lambda i: (i, 0)),
                  pl.BlockSpec((tm * k, 128), lambda i: (t // tm + i, 0)),
                  pl.BlockSpec((tm, d), lambda i: (i, 0)),
                  pl.BlockSpec((1, 6, d), lambda i: (_mod_group(i * tm, t_lat, seq), 0, 0)),
                  pl.BlockSpec((tm, 128), lambda i: (i, 0)),
                  pl.BlockSpec((tm, 128), lambda i: (i, 0))] + extra_specs,
        out_specs=pl.BlockSpec((tm, d), lambda i: (i, 0)),
        compiler_params=_cp("parallel"),
        name="moe_combine",
    )(y_tiles, y_tiles, x, mod, g1, g2, *extra_args)


def _moe(h2, h2_tiles, x, mod, router, w_in, w_out, t_lat, seq, tm, final_g=None):
    t, d = h2.shape
    e_idx, g1, g2 = _router(h2, router, tm)
    e_flat = e_idx[:, :2].reshape(-1)
    onehot = (e_flat[:, None] == jnp.arange(N_EXPERTS, dtype=jnp.int32)[None, :]).astype(jnp.int32)
    csum = jnp.cumsum(onehot, axis=0)
    counts = csum[-1]
    padded = (counts + MOE_TILE - 1) // MOE_TILE * MOE_TILE
    ends = jnp.cumsum(padded)
    pstarts = ends - padded
    dest = jnp.sum(onehot * (pstarts[None, :] + csum - 1), axis=1).astype(jnp.int32)
    n_tiles = (2 * t + MOE_TILE - 1) // MOE_TILE + N_EXPERTS
    n_rows = n_tiles * MOE_TILE
    slot_a = jnp.full((n_rows,), -1, jnp.int32).at[dest].set(jnp.arange(2 * t, dtype=jnp.int32))
    is_pad = slot_a < 0
    slot_row = jnp.where(is_pad, 2 * t - 1 + jnp.cumsum(is_pad.astype(jnp.int32)),
                         (slot_a % 2) * t + slot_a // 2)
    src_tok = jnp.where(is_pad, 0, slot_a // 2)
    src_tok = jnp.concatenate([src_tok, jnp.zeros((MOE_TILE,), jnp.int32)]).reshape(n_tiles + 1, 1, MOE_TILE)
    tile_start = jnp.arange(n_tiles + 1, dtype=jnp.int32) * MOE_TILE
    tile_e = jnp.minimum(jnp.searchsorted(ends, tile_start, side='right'), N_EXPERTS - 1).astype(jnp.int32)

    y = _experts(h2_tiles, src_tok, slot_row.reshape(n_tiles, 1, MOE_TILE), tile_e, w_in, w_out, n_rows)
    return _combine(y, x, mod, g1, g2, t_lat, seq, tm, final_g)


def _final_norm_kernel(x_ref, g_ref, o_ref):
    x = x_ref[...]
    ms = jnp.mean(x * x, axis=-1, keepdims=True)
    o_ref[...] = x * lax.rsqrt(ms + EPS) * g_ref[...]


def _final_norm(x, g, tm):
    t, d = x.shape
    return pl.pallas_call(
        _final_norm_kernel,
        out_shape=jax.ShapeDtypeStruct((t, d), F32),
        grid=(t // tm,),
        in_specs=[pl.BlockSpec((tm, d), lambda i: (i, 0)), pl.BlockSpec((1, d), lambda i: (0, 0))],
        out_specs=pl.BlockSpec((tm, d), lambda i: (i, 0)),
        compiler_params=_cp("parallel"),
        name="final_norm",
    )(x, g.reshape(1, d))


def _proj_weights(w_in):
    o = {}
    acc = 0
    for name, size in (('a_z', 256), ('a_x', 256), ('a_b', 256), ('a_c', 256), ('a_dt', 8), ('b_u', 256),
                       ('c_q', 256), ('c_k', 128), ('c_v', 128), ('d_q', 256), ('d_k', 128), ('d_v', 128),
                       ('gates', 4096)):
        o[name] = (acc, size)
        acc += size
    order = ('gates', 'a_z', 'b_u', 'c_q', 'd_q', 'c_k', 'c_v', 'd_k', 'd_v', 'a_x', 'a_b', 'a_c')
    w = jnp.concatenate([w_in[:, :, o[n][0]:o[n][0] + o[n][1]] for n in order], axis=-1).astype(BF16)
    dt0 = o['a_dt'][0]
    wdt = jnp.pad(w_in[:, :, dt0:dt0 + 8], ((0, 0), (0, 0), (0, 120))).astype(BF16)
    return w, wdt


def kernel(x, c, ctx, c_ctx, norm1_g, norm2_g, ada_w, ada_b, w_in, ssd_conv_w, ssd_conv_b, ssd_a_log,
           ssd_dt_bias, ssd_d, ssd_norm_g, s5_lam_re, s5_lam_im, s5_log_step, s5_b_re, s5_b_im, s5_c_re,
           s5_c_im, s5_d, s5_glu_w, qk_norm_g, swa_sink, w_branch, w_out, ffn_w_in, ffn_w_out, moe_router,
           moe_w_in, moe_w_out, final_norm_g):
    bsz, seq, d = x.shape
    n_ctx = ctx.shape[1]
    depth = w_in.shape[0]
    t_lat, t_ctx = bsz * seq, bsz * n_ctx
    tm = _pow2_tile(1024, seq, t_ctx)
    tm_small = _pow2_tile(512, seq, t_ctx)

    cvec = jnp.zeros((16, d), F32).at[0].set(c_ctx).at[1:1 + bsz].set(c)
    mod = _adaln(cvec, ada_w, ada_b).reshape(depth, 16, 6, d)
    wp, wdt = _proj_weights(w_in)
    wbr = w_branch.astype(BF16)
    wo = w_out.astype(BF16)
    ffn_in, ffn_out = ffn_w_in.astype(BF16), ffn_w_out.astype(BF16)
    moe_in, moe_out = moe_w_in.astype(BF16), moe_w_out.astype(BF16)

    xx = jnp.concatenate([x.reshape(t_lat, d), ctx.reshape(t_ctx, d)], axis=0)
    for l in range(depth):
        with_ctx = l < depth - 1
        t_out = t_lat + (t_ctx if with_ctx else 0)
        p, pdt = _inproj(xx, norm1_g[l], mod[l], wp[l], wdt[l], t_lat, seq, tm)
        ya = _ssd(p, pdt, ssd_conv_w[l], ssd_conv_b[l], ssd_a_log[l], ssd_dt_bias[l], ssd_d[l],
                  ssd_norm_g[l], bsz, seq, n_ctx)
        yb = _s5(p, s5_lam_re[l], s5_lam_im[l], s5_log_step[l], s5_b_re[l], s5_b_im[l], s5_c_re[l],
                 s5_c_im[l], s5_d[l], s5_glu_w[l], bsz, seq, n_ctx)
        q1, q2, k1, k2 = _prep(p, qk_norm_g[l], t_lat, seq, tm)
        yc = _gattn(q1, k1, p, bsz, seq, n_ctx, with_ctx)
        yd = _wattn(q2, k2, p, swa_sink[l], bsz, seq, n_ctx, with_ctx)
        routed = l % 2 == 1
        xx, h2, *tiles = _merge((ya, yb, yc, yd), p, xx, mod[l], wbr[l], wo[l], norm2_g[l], t_out, t_lat, seq,
                                tm_small, routed)
        if routed:
            xx = _moe(h2, tiles[0], xx, mod[l], moe_router[l // 2], moe_in[l // 2], moe_out[l // 2], t_lat,
                      seq, tm_small, None if with_ctx else final_norm_g)
        else:
            xx = _ffn_dense(h2, xx, mod[l], ffn_in[l // 2], ffn_out[l // 2], t_lat, seq, tm)
    if depth % 2 == 1:
        xx = _final_norm(xx[:t_lat], final_norm_g, tm)
    return xx.reshape(bsz, seq, d)
```

```python
import functools

import numpy as np
import jax
import jax.numpy as jnp
from jax import lax
from jax.experimental import pallas as pl
from jax.experimental.pallas import tpu as pltpu

F32 = jnp.float32
BF16 = jnp.bfloat16

EPS = 1e-6
NEG_INF = -1e30
GRID_W = 64
MIX_W = 256
HEAD_DIM = 64
N_HEADS = 4
ATTN_SCALE = HEAD_DIM ** -0.5
SSD_STATE = 128
SSD_CONV = 5
CHUNK = 128
HALO = 16
S5_GROUPS = 16
S5_GROUP = 16
S5_STATE = 64
S5_LANES = S5_GROUPS * S5_STATE
S5_STEPS = 64
ROPE_BASE = 10000.0
ROPE_FREQS = 16
WINDOW = 128
KT_UNIT = 256
N_EXPERTS = 8
MOE_TILE = 512
MOE_STEPS = 7
MOE_WAITS = 64
TOK_ROWS = 8
VMEM_LIMIT = 56 * 1024 * 1024

C_GATES, C_Z, C_BU, C_CQ, C_DQ = 0, 4096, 4352, 4608, 4864
C_CK, C_CV, C_DK, C_DV = 5120, 5248, 5376, 5504
C_AX, C_AB, C_AC = 5632, 5888, 6144
P_COLS = 6400
PROJ_TN = 1280


def _cp(*sem):
    return pltpu.CompilerParams(dimension_semantics=sem, vmem_limit_bytes=VMEM_LIMIT)


def _pow2_tile(cap, *dims):
    t = 1
    while t * 2 <= cap and all(d % (t * 2) == 0 for d in dims):
        t *= 2
    return t


def _dot(a, b):
    return jnp.dot(a, b, preferred_element_type=F32)


def _dot_nt(a, b):
    return lax.dot_general(a, b, (((1,), (1,)), ((), ())), preferred_element_type=F32)


def _dot_tn(a, b):
    return lax.dot_general(a, b, (((0,), (0,)), ((), ())), preferred_element_type=F32)


def _split3(x):
    hi = x.astype(BF16)
    r1 = x - hi.astype(F32)
    mid = r1.astype(BF16)
    lo = (r1 - mid.astype(F32)).astype(BF16)
    return hi, mid, lo


def _silu(x):
    return x * jax.nn.sigmoid(x)


def _adaln_kernel(c_ref, w_ref, b_ref, o_ref):
    c = c_ref[...]
    o_ref[0] = jnp.dot(_silu(c), w_ref[0], preferred_element_type=F32,
                       precision=lax.Precision.HIGHEST) + b_ref[0]


def _adaln(cvec, ada_w, ada_b):
    depth, d, n = ada_w.shape
    tn = 1024
    return pl.pallas_call(
        _adaln_kernel,
        out_shape=jax.ShapeDtypeStruct((depth, 16, n), F32),
        grid=(depth, n // tn),
        in_specs=[pl.BlockSpec((16, d), lambda l, j: (0, 0)),
                  pl.BlockSpec((1, d, tn), lambda l, j: (l, 0, j)),
                  pl.BlockSpec((1, 1, tn), lambda l, j: (l, 0, j))],
        out_specs=pl.BlockSpec((1, 16, tn), lambda l, j: (l, 0, j)),
        compiler_params=_cp("parallel", "parallel"),
        name="adaln",
    )(cvec, ada_w, ada_b.reshape(depth, 1, n))


def _mod_group(row0, t_lat, seq):
    return jnp.where(row0 >= t_lat, 0, 1 + row0 // seq)


def _inproj_kernel(x_ref, g_ref, mod_ref, w_ref, wdt_ref, o_ref, odt_ref, h_ref):
    @pl.when(pl.program_id(1) == 0)
    def _():
        x = x_ref[...]
        ms = jnp.mean(x * x, axis=-1, keepdims=True)
        y = x * lax.rsqrt(ms + EPS) * g_ref[...]
        h = (y * (1.0 + mod_ref[0, 1:2, :]) + mod_ref[0, 0:1, :]).astype(BF16)
        h_ref[...] = h
        odt_ref[...] = _dot(h, wdt_ref[...])

    o_ref[...] = _dot(h_ref[...], w_ref[...]).astype(BF16)


def _inproj(x, g, mod, w, wdt, t_lat, seq, tm):
    t, d = x.shape
    n = w.shape[1]
    tn = PROJ_TN
    return pl.pallas_call(
        _inproj_kernel,
        out_shape=(jax.ShapeDtypeStruct((t, n), BF16), jax.ShapeDtypeStruct((t, 128), F32)),
        grid=(t // tm, n // tn),
        in_specs=[pl.BlockSpec((tm, d), lambda i, j: (i, 0)),
                  pl.BlockSpec((1, d), lambda i, j: (0, 0)),
                  pl.BlockSpec((1, 6, d), lambda i, j: (_mod_group(i * tm, t_lat, seq), 0, 0)),
                  pl.BlockSpec((d, tn), lambda i, j: (0, j)),
                  pl.BlockSpec((d, 128), lambda i, j: (0, 0))],
        out_specs=(pl.BlockSpec((tm, tn), lambda i, j: (i, j)),
                   pl.BlockSpec((tm, 128), lambda i, j: (i, 0))),
        scratch_shapes=[pltpu.VMEM((tm, d), BF16)],
        compiler_params=_cp("parallel", "arbitrary"),
        name="inproj",
    )(x, g.reshape(1, d), mod, w, wdt)


def _ssd_consts():
    r = np.arange(CHUNK)
    tri_l = (r[None, :] <= r[:, None]).astype(np.float32)
    tri_u = tri_l.T.copy()
    shifts = np.zeros((4, CHUNK, CHUNK + 2 * HALO), np.float32)
    for n, k in enumerate((0, 1, 3, 4)):
        shifts[n, r, r + HALO + k - 2] = 1.0
    spread = np.zeros((2, 128, MIX_W), np.float32)
    for d in range(2):
        for h in range(N_HEADS):
            spread[d, N_HEADS * d + h, HEAD_DIM * h:HEAD_DIM * (h + 1)] = 1.0
    return (jnp.asarray(np.stack([tri_l, tri_u]), BF16), jnp.asarray(shifts, BF16),
            jnp.asarray(spread, BF16))


def _ssd_kernel(zl_ref, xl_ref, bl_ref, cl_ref, dtl_ref, zc_ref, xc_ref, bc_ref, cc_ref, dtc_ref,
                cw_ref, cb_ref, an_ref, bias_ref, dsk_ref, ng_ref, tri_ref, sh_ref, e_ref,
                yl_ref, yc_ref, act_l, act_c, yf_l, yf_c, stf_ref, stb_ref, *, n_lat, n_ctx):
    q = CHUNK
    lane128 = lax.broadcasted_iota(jnp.int32, (q, 128), 1)
    lane256 = lax.broadcasted_iota(jnp.int32, (q, MIX_W), 1)
    lane256r = lax.broadcasted_iota(jnp.int32, (1, MIX_W), 1)
    row_i = lax.broadcasted_iota(jnp.int32, (q, q), 0)
    col_i = lax.broadcasted_iota(jnp.int32, (q, q), 1)
    head_masks = [(lane256 >= HEAD_DIM * h) & (lane256 < HEAD_DIM * (h + 1)) for h in range(N_HEADS)]
    head_masks_r = [(lane256r >= HEAD_DIM * h) & (lane256r < HEAD_DIM * (h + 1)) for h in range(N_HEADS)]
    group_masks = [lane256 < 128, lane256 >= 128]
    lane_state = lax.broadcasted_iota(jnp.int32, (SSD_STATE, MIX_W), 1)
    state_group_masks = [lane_state < 128, lane_state >= 128]

    def conv_act(x_ref, b_ref, c_ref, n_chunks, c):
        def rows(ref, start, size):
            return ref[pl.ds(start, size), :]

        start = c * q
        if isinstance(c, int):
            p0, n0 = max(start - HALO, 0), min(start + q, n_chunks * q - HALO)
            pf, nf = float(c > 0), float(c < n_chunks - 1)
        else:
            start = pl.multiple_of(start, q)
            p0 = pl.multiple_of(jnp.maximum(start - HALO, 0), HALO)
            n0 = pl.multiple_of(jnp.minimum(start + q, n_chunks * q - HALO), HALO)
            pf, nf = (c > 0).astype(F32), (c < n_chunks - 1).astype(F32)
        parts = []
        for ref in (x_ref, b_ref, c_ref):
            prev = (rows(ref, p0, HALO).astype(F32) * pf).astype(BF16)
            nxt = (rows(ref, n0, HALO).astype(F32) * nf).astype(BF16)
            parts.append(jnp.concatenate([prev, rows(ref, start, q), nxt], axis=0))
        ext = jnp.concatenate(parts, axis=1)
        cur = ext[HALO:HALO + q].astype(F32)
        acc = cur * cw_ref[2:3, :] + cb_ref[...]
        for n, k in enumerate((0, 1, 3, 4)):
            acc = acc + _dot(sh_ref[n], ext) * cw_ref[k:k + 1, :]
        return _silu(acc)

    def conv_chunk(seg, c):
        _, x_ref, b_ref, c_ref, _, act_ref, _, _, n_chunks = seg
        start = c * q if isinstance(c, int) else pl.multiple_of(c * q, q)
        act_ref[pl.ds(start, q), :] = conv_act(x_ref, b_ref, c_ref, n_chunks, c).astype(BF16)

    def chunk(seg, c, d, second):
        z_ref, x_ref, b_ref, c_ref, dt_ref, act_ref, yf_ref, y_ref, n_chunks = seg
        st_ref = st_refs[d]
        start = c * q if isinstance(c, int) else pl.multiple_of(c * q, q)
        act = act_ref[pl.ds(start, q), :]
        xs = act[:, 0:MIX_W].astype(F32)
        bm = act[:, MIX_W:2 * MIX_W]
        cm = act[:, 2 * MIX_W:3 * MIX_W]

        dt_n = jax.nn.softplus(dt_ref[pl.ds(start, q), :] + bias_ref[...])
        la_n = dt_n * an_ref[...]
        hi, mid, lo = _split3(la_n)
        tri = tri_ref[d]
        cs_n = _dot(tri, hi) + _dot(tri, mid) + _dot(tri, lo)
        cs_t = cs_n.T
        edge = q - 1 if d == 0 else 0
        tri_mask = (col_i <= row_i) if d == 0 else (col_i >= row_i)

        tot_n = cs_n[edge:edge + 1, :]
        spread = e_ref[d]
        dt_full = _dot(dt_n.astype(BF16), spread)
        dtdte_full = _dot((dt_n * jnp.exp(tot_n - cs_n)).astype(BF16), spread)
        ecs_full = _dot(jnp.exp(cs_n).astype(BF16), spread)
        etot_n = jnp.exp(tot_n)
        tot_full = jnp.zeros((1, MIX_W), F32)
        decay = []
        for h in range(N_HEADS):
            sel = lane128 == (N_HEADS * d + h)
            cs_col = jnp.sum(jnp.where(sel, cs_n, 0.0), axis=-1, keepdims=True)
            cs_row = cs_t[N_HEADS * d + h:N_HEADS * d + h + 1, :]
            decay.append(jnp.where(tri_mask, jnp.exp(cs_col - cs_row), 0.0))
            etot = jnp.sum(jnp.where(sel[0:1, :], etot_n, 0.0), axis=-1, keepdims=True)
            tot_full = jnp.where(head_masks_r[h], etot, tot_full)

        xdt = xs * dt_full
        state = st_ref[...]
        y = jnp.zeros((q, MIX_W), F32)
        y_off = jnp.zeros((q, MIX_W), F32)
        upd = jnp.zeros((SSD_STATE, MIX_W), F32)
        xdte = xs * dtdte_full
        for g in range(2):
            bg = bm[:, 128 * g:128 * (g + 1)]
            cg = cm[:, 128 * g:128 * (g + 1)]
            cb = _dot_nt(cg, bg)
            for h in (2 * g, 2 * g + 1):
                m = (cb * decay[h]).astype(BF16)
                y = y + _dot(m, jnp.where(head_masks[h], xdt, 0.0).astype(BF16))
            y_off = y_off + _dot(cg, jnp.where(state_group_masks[g], state, 0.0).astype(BF16))
            upd = upd + _dot_tn(bg, jnp.where(group_masks[g], xdte, 0.0).astype(BF16))
        y = y + y_off * ecs_full
        st_ref[...] = state * tot_full + upd

        if not second:
            yf_ref[pl.ds(start, q), :] = y
        else:
            y = y + yf_ref[pl.ds(start, q), :] + dsk_ref[...] * xs
            y = y * _silu(z_ref[pl.ds(start, q), :].astype(F32))
            ms = jnp.mean(y * y, axis=-1, keepdims=True)
            y_ref[pl.ds(start, q), :] = (y * lax.rsqrt(ms + EPS) * ng_ref[...]).astype(BF16)

    seg_l = (zl_ref, xl_ref, bl_ref, cl_ref, dtl_ref, act_l, yf_l, yl_ref, n_lat)
    seg_c = (zc_ref, xc_ref, bc_ref, cc_ref, dtc_ref, act_c, yf_c, yc_ref, n_ctx)
    st_refs = (stf_ref, stb_ref)
    stf_ref[...] = jnp.zeros_like(stf_ref)
    stb_ref[...] = jnp.zeros_like(stb_ref)

    for c in range(n_ctx):
        conv_chunk(seg_c, c)

    def conv_body(i, carry):
        conv_chunk(seg_l, i)
        return carry

    lax.fori_loop(0, n_lat, conv_body, 0)

    for i in range(n_ctx):
        chunk(seg_c, i, 0, 2 * i >= n_ctx)
        chunk(seg_c, n_ctx - 1 - i, 1, 2 * i >= n_ctx - 1)

    def pair(second):
        def body(i, carry):
            chunk(seg_l, i, 0, second)
            chunk(seg_l, n_lat - 1 - i, 1, second)
            return carry
        return body

    lax.fori_loop(0, n_lat // 2, pair(False), 0, unroll=2)
    lax.fori_loop(n_lat // 2, n_lat, pair(True), 0, unroll=2)


def _ssd(p, pdt, conv_w, conv_b, a_log, dt_bias, d_skip, norm_g, bsz, seq, ctx):
    t = p.shape[0]
    n_lat, n_ctx = seq // CHUNK, ctx // CHUNK
    assert n_lat % 2 == 0
    cb0 = (bsz * seq) // ctx
    tri, shifts, spread = _ssd_consts()
    cw = jnp.zeros((8, 3 * MIX_W), F32).at[:SSD_CONV].set(conv_w)
    a_n = jnp.zeros((1, 128), F32).at[0, :8].set(-jnp.exp(a_log.astype(F32)).reshape(8))
    bias_n = jnp.zeros((1, 128), F32).at[0, :8].set(dt_bias.astype(F32).reshape(8))
    dsk = jnp.repeat(d_skip.astype(F32), HEAD_DIM).reshape(1, MIX_W)

    def lat(col):
        return pl.BlockSpec((seq, MIX_W), lambda b, col=col: (b, col // MIX_W))

    def cx(col):
        return pl.BlockSpec((ctx, MIX_W), lambda b, col=col: (cb0 + b, col // MIX_W))

    def full(shape):
        return pl.BlockSpec(shape, lambda b: (0,) * len(shape))

    kern = functools.partial(_ssd_kernel, n_lat=n_lat, n_ctx=n_ctx)
    yl, yc = pl.pallas_call(
        kern,
        out_shape=(jax.ShapeDtypeStruct((bsz * seq, MIX_W), BF16),
                   jax.ShapeDtypeStruct((bsz * ctx, MIX_W), BF16)),
        grid=(bsz,),
        in_specs=[lat(C_Z), lat(C_AX), lat(C_AB), lat(C_AC),
                  pl.BlockSpec((seq, 128), lambda b: (b, 0)),
                  cx(C_Z), cx(C_AX), cx(C_AB), cx(C_AC),
                  pl.BlockSpec((ctx, 128), lambda b: (cb0 + b, 0)),
                  full((8, 3 * MIX_W)), full((1, 3 * MIX_W)), full((1, 128)), full((1, 128)),
                  full((1, MIX_W)), full((1, MIX_W)), full((2, CHUNK, CHUNK)),
                  full((4, CHUNK, CHUNK + 2 * HALO)), full((2, 128, MIX_W))],
        out_specs=(pl.BlockSpec((seq, MIX_W), lambda b: (b, 0)),
                   pl.BlockSpec((ctx, MIX_W), lambda b: (b, 0))),
        scratch_shapes=[pltpu.VMEM((seq, 3 * MIX_W), BF16), pltpu.VMEM((ctx, 3 * MIX_W), BF16),
                        pltpu.VMEM((seq, MIX_W), F32), pltpu.VMEM((ctx, MIX_W), F32),
                        pltpu.VMEM((SSD_STATE, MIX_W), F32), pltpu.VMEM((SSD_STATE, MIX_W), F32)],
        compiler_params=_cp("parallel"),
        name="ssd",
    )(p, p, p, p, pdt, p, p, p, p, pdt, cw, conv_b.reshape(1, -1).astype(F32), a_n, bias_n,
      dsk, norm_g.reshape(1, MIX_W).astype(F32), tri, shifts, spread)
    return yl, yc


def _s5_discretize(lam_re, lam_im, log_step, b_re, b_im):
    step = jnp.exp(log_step.astype(F32))[:, None]
    lr = jnp.minimum(lam_re.astype(F32), -1e-4)
    li = lam_im.astype(F32)
    mag = jnp.exp(lr * step)
    ang = li * step
    ab_re, ab_im = mag * jnp.cos(ang), mag * jnp.sin(ang)
    den = lr * lr + li * li
    f_re = ((ab_re - 1.0) * lr + ab_im * li) / den
    f_im = (ab_im * lr - (ab_re - 1.0) * li) / den
    br, bi = b_re.astype(F32), b_im.astype(F32)
    bb_re = f_re[..., None] * br - f_im[..., None] * bi
    bb_im = f_re[..., None] * bi + f_im[..., None] * br
    return ab_re, ab_im, bb_re, bb_im


def _s5_mats(lam_re, lam_im, log_step, b_re, b_im, c_re, c_im):
    eye = jnp.eye(S5_GROUPS, dtype=F32)
    a_all, b_all, c_all = [], [], []
    for d in range(2):
        ab_re, ab_im, bb_re, bb_im = _s5_discretize(lam_re[d], lam_im[d], log_step[d], b_re[d], b_im[d])
        bm = [jnp.einsum('gnp,gh->gphn', m, eye).reshape(MIX_W, S5_LANES) for m in (bb_re, bb_im)]
        cm = [jnp.einsum('gpn,gh->gnhp', m.astype(F32), eye).reshape(S5_LANES, MIX_W)
              for m in (c_re[d], c_im[d])]
        b_all.append(jnp.concatenate(bm, axis=1))
        c_all.append(jnp.concatenate([cm[0], -cm[1]], axis=0))
        a_all.append(jnp.concatenate([ab_re.reshape(1, S5_LANES), ab_im.reshape(1, S5_LANES)], axis=1))
    a = jnp.broadcast_to(jnp.stack(a_all), (2, 8, 2 * S5_LANES))
    return a, jnp.stack(b_all).astype(BF16), jnp.stack(c_all).astype(BF16)


def _s5_kernel(uf_ref, ub_ref, a_ref, b_ref, c_ref, yf_ref, yb_ref, buf_ref, s_ref):
    n = S5_LANES

    @pl.when(pl.program_id(0) == 0)
    def _():
        s_ref[...] = jnp.zeros_like(s_ref)

    for d, u_ref in enumerate((uf_ref, ub_ref)):
        buf_ref[d] = _dot(u_ref[...], b_ref[d])
    state = [(s_ref[d, :, 0:n], s_ref[d, :, n:2 * n]) for d in range(2)]
    for j in range(S5_STEPS):
        for d in range(2):
            jj = j if d == 0 else S5_STEPS - 1 - j
            rows = slice(jj * 8, jj * 8 + 8)
            a_re, a_im = a_ref[d, :, 0:n], a_ref[d, :, n:2 * n]
            s_re, s_im = state[d]
            n_re = a_re * s_re - a_im * s_im + buf_ref[d, rows, 0:n]
            n_im = a_re * s_im + a_im * s_re + buf_ref[d, rows, n:2 * n]
            buf_ref[d, rows, 0:n] = n_re
            buf_ref[d, rows, n:2 * n] = n_im
            state[d] = (n_re, n_im)
    for d, y_ref in enumerate((yf_ref, yb_ref)):
        s_ref[d, :, 0:n] = state[d][0]
        s_ref[d, :, n:2 * n] = state[d][1]
        y_ref[...] = _dot(buf_ref[d].astype(BF16), c_ref[d])


def _s5_finish_kernel(yf_ref, yb_ref, u_ref, d_ref, w_ref, o_ref):
    y = yf_ref[...] + yb_ref[...] + d_ref[...] * u_ref[...].astype(F32)
    v = jax.nn.gelu(y, approximate=True).astype(BF16)
    r = _dot(v, w_ref[...])
    o_ref[...] = (r[:, 0:MIX_W] * jax.nn.sigmoid(r[:, MIX_W:2 * MIX_W])).astype(BF16)


def _s5(p, lam_re, lam_im, log_step, b_re, b_im, c_re, c_im, d_skip, glu_w, bsz, seq, ctx):
    assert bsz <= 8
    t_lat = bsz * seq
    u = p[:, C_BU:C_BU + MIX_W]
    u_l = jnp.transpose(u[:t_lat].reshape(bsz, seq, MIX_W), (1, 0, 2))
    u_c = jnp.transpose(u[t_lat:].reshape(bsz, ctx, MIX_W), (1, 0, 2))
    u_tm = jnp.concatenate([u_c, u_l], axis=0)
    if bsz < 8:
        u_tm = jnp.pad(u_tm, ((0, 0), (0, 8 - bsz), (0, 0)))
    steps = seq + ctx
    u_tm = u_tm.reshape(steps * 8, MIX_W)
    a, bmat, cmat = _s5_mats(lam_re, lam_im, log_step, b_re, b_im, c_re, c_im)
    rows = S5_STEPS * 8
    nc, ncc = steps // S5_STEPS, ctx // S5_STEPS

    def bwd(i):
        return jnp.where(i < ncc, ncc - 1 - i, nc + ncc - 1 - i)

    def whole(shape):
        return pl.BlockSpec(shape, lambda i: (0,) * len(shape))

    yshape = jax.ShapeDtypeStruct((steps * 8, MIX_W), F32)
    yf, yb = pl.pallas_call(
        _s5_kernel,
        out_shape=(yshape, yshape),
        grid=(nc,),
        in_specs=[pl.BlockSpec((rows, MIX_W), lambda i: (i, 0)),
                  pl.BlockSpec((rows, MIX_W), lambda i: (bwd(i), 0)),
                  whole((2, 8, 2 * S5_LANES)), whole((2, MIX_W, 2 * S5_LANES)),
                  whole((2, 2 * S5_LANES, MIX_W))],
        out_specs=(pl.BlockSpec((rows, MIX_W), lambda i: (i, 0)),
                   pl.BlockSpec((rows, MIX_W), lambda i: (bwd(i), 0))),
        scratch_shapes=[pltpu.VMEM((2, rows, 2 * S5_LANES), F32), pltpu.VMEM((2, 8, 2 * S5_LANES), F32)],
        compiler_params=_cp("arbitrary"),
        name="s5_scan",
    )(u_tm, u_tm, a, bmat, cmat)

    tmf = _pow2_tile(2048, steps * 8)
    o = pl.pallas_call(
        _s5_finish_kernel,
        out_shape=jax.ShapeDtypeStruct((steps * 8, MIX_W), BF16),
        grid=(steps * 8 // tmf,),
        in_specs=[pl.BlockSpec((tmf, MIX_W), lambda i: (i, 0)),
                  pl.BlockSpec((tmf, MIX_W), lambda i: (i, 0)),
                  pl.BlockSpec((tmf, MIX_W), lambda i: (i, 0)),
                  pl.BlockSpec((1, MIX_W), lambda i: (0, 0)),
                  pl.BlockSpec((MIX_W, 2 * MIX_W), lambda i: (0, 0))],
        out_specs=pl.BlockSpec((tmf, MIX_W), lambda i: (i, 0)),
        compiler_params=_cp("parallel"),
        name="s5_finish",
    )(yf, yb, u_tm, d_skip.reshape(1, MIX_W).astype(F32), glu_w.astype(BF16))
    o = o.reshape(steps, 8, MIX_W)[:, :bsz]
    o_c = jnp.transpose(o[:ctx], (1, 0, 2)).reshape(bsz * ctx, MIX_W)
    o_l = jnp.transpose(o[ctx:], (1, 0, 2)).reshape(t_lat, MIX_W)
    return o_l, o_c


def _rope_tables(seq, tm):
    rows = seq // GRID_W
    pos_r = jnp.repeat(jnp.arange(rows, dtype=F32), GRID_W)
    pos_c = jnp.tile(jnp.arange(GRID_W, dtype=F32), rows)
    inv = ROPE_BASE ** (-jnp.arange(ROPE_FREQS, dtype=F32) / ROPE_FREQS)
    ar, ac = pos_r[:, None] * inv, pos_c[:, None] * inv
    cos = jnp.concatenate([jnp.cos(ar), jnp.cos(ar), jnp.cos(ac), jnp.cos(ac)], axis=-1)
    sin = jnp.concatenate([-jnp.sin(ar), jnp.sin(ar), -jnp.sin(ac), jnp.sin(ac)], axis=-1)
    cos = jnp.concatenate([jnp.tile(cos, (1, 2)), jnp.ones((tm, 128), F32)], axis=0)
    sin = jnp.concatenate([jnp.tile(sin, (1, 2)), jnp.zeros((tm, 128), F32)], axis=0)
    return cos, sin


def _prep_consts():
    i = np.arange(MIX_W)
    bd = ((i[:, None] // HEAD_DIM) == (i[None, :] // HEAD_DIM)).astype(np.float32) / HEAD_DIM
    pm = (i[:, None] == (i[None, :] ^ ROPE_FREQS)).astype(np.float32)
    return jnp.asarray(bd, BF16), jnp.asarray(pm, BF16)


def _prep_kernel(cq_ref, dq_ref, ck_ref, dk_ref, cos_ref, sin_ref, qg_ref, kg_ref, bd_ref, pm_ref,
                 q1_ref, q2_ref, k1_ref, k2_ref):
    cos, sin = cos_ref[...], sin_ref[...]
    cos2 = jnp.concatenate([cos, cos], axis=-1)
    sin2 = jnp.concatenate([sin, sin], axis=-1)
    bd, pm = bd_ref[...], pm_ref[...]
    tm = cos.shape[0]
    lane = lax.broadcasted_iota(jnp.int32, (tm, 128), 1)

    def rms(x, g, n):
        ms = _dot((x * x).astype(BF16), bd[:n, :n])
        return x * lax.rsqrt(ms + EPS) * g

    def rope(y, c, s, n):
        return y * c + _dot(y.astype(BF16), pm[:n, :n]) * s

    def store_q(q, ref):
        q = q * ATTN_SCALE
        for kv in range(2):
            for g in range(2):
                half = q[:, 128 * kv:128 * (kv + 1)]
                if g != kv:
                    half = pltpu.roll(half, HEAD_DIM, 1)
                keep = (lane >= HEAD_DIM * kv) & (lane < HEAD_DIM * (kv + 1))
                ref[2 * kv + g] = jnp.where(keep, half, 0.0).astype(BF16)

    store_q(rope(rms(cq_ref[...].astype(F32), qg_ref[...], MIX_W), cos2, sin2, MIX_W), q1_ref)
    store_q(rope(dq_ref[...].astype(F32), cos2, sin2, MIX_W), q2_ref)
    k1 = rope(rms(ck_ref[...].astype(F32), kg_ref[...], 128), cos, sin, 128)
    for u in range(tm // KT_UNIT):
        k1_ref[u] = k1[u * KT_UNIT:(u + 1) * KT_UNIT].T.astype(BF16)
    k2 = rope(dk_ref[...].astype(F32), cos, sin, 128)
    for u in range(tm // WINDOW):
        k2_ref[u] = k2[u * WINDOW:(u + 1) * WINDOW].T.astype(BF16)


def _prep(p, qk_gain, t_lat, seq, tm):
    t = p.shape[0]
    cos, sin = _rope_tables(seq, tm)
    bd, pm = _prep_consts()
    qg = jnp.tile(qk_gain[0].astype(F32), N_HEADS).reshape(1, MIX_W)
    kg = jnp.tile(qk_gain[1].astype(F32), 2).reshape(1, 128)
    nt = seq // tm

    def tab(i):
        return (jnp.where(i * tm >= t_lat, nt, i % nt), 0)

    def const(shape):
        return pl.BlockSpec(shape, lambda i: (0,) * len(shape))

    qshape = jax.ShapeDtypeStruct((N_HEADS, t, 128), BF16)
    ktshape = jax.ShapeDtypeStruct((t // KT_UNIT, 128, KT_UNIT), BF16)
    kt2shape = jax.ShapeDtypeStruct((t // WINDOW, 128, WINDOW), BF16)
    return pl.pallas_call(
        _prep_kernel,
        out_shape=(qshape, qshape, ktshape, kt2shape),
        grid=(t // tm,),
        in_specs=[pl.BlockSpec((tm, MIX_W), lambda i: (i, C_CQ // MIX_W)),
                  pl.BlockSpec((tm, MIX_W), lambda i: (i, C_DQ // MIX_W)),
                  pl.BlockSpec((tm, 128), lambda i: (i, C_CK // 128)),
                  pl.BlockSpec((tm, 128), lambda i: (i, C_DK // 128)),
                  pl.BlockSpec((tm, 128), tab), pl.BlockSpec((tm, 128), tab),
                  const((1, MIX_W)), const((1, 128)), const((MIX_W, MIX_W)), const((MIX_W, MIX_W))],
        out_specs=(pl.BlockSpec((N_HEADS, tm, 128), lambda i: (0, i, 0)),
                   pl.BlockSpec((N_HEADS, tm, 128), lambda i: (0, i, 0)),
                   pl.BlockSpec((tm // KT_UNIT, 128, KT_UNIT), lambda i: (i, 0, 0)),
                   pl.BlockSpec((tm // WINDOW, 128, WINDOW), lambda i: (i, 0, 0))),
        compiler_params=_cp("parallel"),
        name="qk_prep",
    )(p, p, p, p, cos, sin, qg, kg, bd, pm)


def _pack_heads(o, tq):
    lane = lax.broadcasted_iota(jnp.int32, (tq, 128), 1)
    left = lane < HEAD_DIM
    o00, o01, o10, o11 = [o[h * tq:(h + 1) * tq] for h in range(N_HEADS)]
    out0 = jnp.where(left, o00, pltpu.roll(o01, HEAD_DIM, 1))
    out1 = jnp.where(left, pltpu.roll(o10, HEAD_DIM, 1), o11)
    return jnp.concatenate([out0, out1], axis=-1)


def _gattn_kernel(q_ref, kl_ref, vl_ref, kc_ref, vc_ref, o_ref, m_ref, acc_ref,
                  *, tq, units, n_qb_lat, n_kvb):
    qb = pl.program_id(1)

    def tree(op, xs):
        while len(xs) > 1:
            xs = [op(xs[i], xs[i + 1]) for i in range(0, len(xs) - 1, 2)] + ([xs[-1]] if len(xs) % 2 else [])
        return xs[0]

    def scores(h, kts):
        q = q_ref[h]
        cols = []
        for kt in kts:
            s = _dot(q, kt)
            cols += [s[:, 128 * c:128 * (c + 1)] for c in range(s.shape[1] // 128)]
        return cols

    def softmax(h, cols, first):
        m_blk = jnp.max(tree(jnp.maximum, cols), axis=-1, keepdims=True)
        if first:
            m_new = jnp.broadcast_to(m_blk, (tq, 128))
            alpha = None
        else:
            m_old = m_ref[h]
            m_new = jnp.maximum(m_old, m_blk)
            alpha = jnp.exp(m_old - m_new)
        m_ref[h] = m_new
        return alpha, jnp.concatenate([jnp.exp((c - m_new).astype(BF16)) for c in cols], axis=-1)

    def weighted(h, alpha, p, v):
        pv = _dot(p, v)
        acc_ref[h] = pv if alpha is None else jnp.concatenate([alpha, alpha], axis=-1) * acc_ref[h] + pv

    def block(kts, v, first):
        v = jnp.concatenate([v, jnp.ones_like(v)], axis=-1)
        cols = scores(0, kts)
        for h in range(N_HEADS):
            nxt = scores(h + 1, kts) if h + 1 < N_HEADS else None
            alpha, p = softmax(h, cols, first)
            weighted(h, alpha, p, v)
            cols = nxt

    block([kc_ref[u] for u in range(kc_ref.shape[0])], vc_ref[...], True)

    def body(j, carry):
        rows = pl.ds(pl.multiple_of(j * (units * KT_UNIT), units * KT_UNIT), units * KT_UNIT)
        block([kl_ref[j * units + u] for u in range(units)], vl_ref[rows, :], False)
        return carry

    lax.fori_loop(0, jnp.where(qb < n_qb_lat, n_kvb, 0), body, 0)
    o = [acc_ref[h, :, 0:128] / acc_ref[h, :, 128:256] for h in range(N_HEADS)]
    o_ref[...] = _pack_heads(jnp.concatenate(o, axis=0), tq).astype(BF16)


def _gattn(qp, kt, p, bsz, seq, ctx, with_ctx):
    t_lat = bsz * seq
    tq = ctx
    units = min(8, seq // KT_UNIT)
    n_qb_lat = seq // tq
    n_qb = n_qb_lat + (1 if with_ctx else 0)
    cb0 = t_lat // ctx
    t_out = t_lat + (bsz * ctx if with_ctx else 0)
    assert ctx % KT_UNIT == 0 and seq % (units * KT_UNIT) == 0

    def qrow(b, i):
        return jnp.where(i < n_qb_lat, b * n_qb_lat + i, cb0 + b)

    kern = functools.partial(_gattn_kernel, tq=tq, units=units, n_qb_lat=n_qb_lat,
                             n_kvb=seq // (units * KT_UNIT))
    return pl.pallas_call(
        kern,
        out_shape=jax.ShapeDtypeStruct((t_out, MIX_W), BF16),
        grid=(bsz, n_qb),
        in_specs=[pl.BlockSpec((N_HEADS, tq, 128), lambda b, i: (0, qrow(b, i), 0)),
                  pl.BlockSpec((seq // KT_UNIT, 128, KT_UNIT), lambda b, i: (b, 0, 0)),
                  pl.BlockSpec((seq, 128), lambda b, i: (b, C_CV // 128)),
                  pl.BlockSpec((ctx // KT_UNIT, 128, KT_UNIT), lambda b, i: (cb0 + b, 0, 0)),
                  pl.BlockSpec((ctx, 128), lambda b, i: (cb0 + b, C_CV // 128))],
        out_specs=pl.BlockSpec((tq, MIX_W), lambda b, i: (qrow(b, i), 0)),
        scratch_shapes=[pltpu.VMEM((N_HEADS, tq, 128), F32), pltpu.VMEM((N_HEADS, tq, 256), F32)],
        compiler_params=_cp("parallel", "arbitrary"),
        name="global_attn",
    )(qp, kt, p, kt, p)


def _wattn_kernel(q_ref, kl_ref, vl_ref, kc_ref, vc_ref, sink_ref, o_ref, *, nb):
    w = WINDOW
    n = pl.program_id(1)
    is_lat = n < nb
    rows = 2 * w
    qi = lax.broadcasted_iota(jnp.int32, (rows, w), 0) & (w - 1)
    kj = lax.broadcasted_iota(jnp.int32, (rows, w), 1)
    band = (jnp.clip(n - 1, 0, nb - 1), jnp.clip(n, 0, nb - 1), jnp.clip(n + 1, 0, nb - 1))
    off_prev = jnp.where(is_lat & (n >= 1), 0, w)
    off_cur = jnp.where(is_lat, 0, w)
    off_next = jnp.where(n + 1 < nb, 0, w)
    masks = [kj >= qi + off_prev, kj >= off_cur, kj <= qi - off_next]
    kts = [kc_ref[u] for u in range(kc_ref.shape[0])] + [kl_ref[i] for i in band]
    n_ctx_tiles = kc_ref.shape[0]
    v_all = jnp.concatenate([vc_ref[...]] + [vl_ref[pl.ds(pl.multiple_of(i * w, w), w), :] for i in band],
                            axis=0)

    def tree(op, xs):
        while len(xs) > 1:
            xs = [op(xs[i], xs[i + 1]) for i in range(0, len(xs) - 1, 2)] + ([xs[-1]] if len(xs) % 2 else [])
        return xs[0]

    def scores(c):
        q = q_ref[2 * c:2 * c + 2].reshape(rows, 128)
        tiles = [_dot(q, kt) for kt in kts]
        return tiles[:n_ctx_tiles] + [jnp.where(mk, t, NEG_INF) for mk, t in zip(masks, tiles[n_ctx_tiles:])]

    def finish(c, tiles):
        sink = sink_ref[c * rows:(c + 1) * rows, :]
        m = jnp.maximum(jnp.max(tree(jnp.maximum, tiles), axis=-1, keepdims=True), sink)
        ps = [jnp.exp(t - m) for t in tiles]
        den = jnp.sum(tree(jnp.add, ps), axis=-1, keepdims=True) + jnp.exp(sink - m)
        return _dot(jnp.concatenate(ps, axis=-1).astype(BF16), v_all) / den

    tiles = scores(0)
    nxt = scores(1)
    o = [finish(0, tiles), finish(1, nxt)]
    o_ref[...] = _pack_heads(jnp.concatenate(o, axis=0), w).astype(BF16)


def _wattn(qp, k, p, sink, bsz, seq, ctx, with_ctx):
    t_lat = bsz * seq
    w = WINDOW
    nb = seq // w
    ncb = ctx // w
    n_qb = nb + (ncb if with_ctx else 0)
    cq0 = t_lat // w
    cb0 = t_lat // ctx
    t_out = t_lat + (bsz * ctx if with_ctx else 0)
    sink_rows = jnp.broadcast_to(jnp.repeat(sink.astype(F32), w)[:, None], (N_HEADS * w, 128))

    def qrow(b, i):
        return jnp.where(i < nb, b * nb + i, cq0 + b * ncb + (i - nb))

    return pl.pallas_call(
        functools.partial(_wattn_kernel, nb=nb),
        out_shape=jax.ShapeDtypeStruct((t_out, MIX_W), BF16),
        grid=(bsz, n_qb),
        in_specs=[pl.BlockSpec((N_HEADS, w, 128), lambda b, i: (0, qrow(b, i), 0)),
                  pl.BlockSpec((seq // w, 128, w), lambda b, i: (b, 0, 0)),
                  pl.BlockSpec((seq, 128), lambda b, i: (b, C_DV // 128)),
                  pl.BlockSpec((ctx // w, 128, w), lambda b, i: (cb0 + b, 0, 0)),
                  pl.BlockSpec((ctx, 128), lambda b, i: (cb0 + b, C_DV // 128)),
                  pl.BlockSpec((N_HEADS * w, 128), lambda b, i: (0, 0))],
        out_specs=pl.BlockSpec((w, MIX_W), lambda b, i: (qrow(b, i), 0)),
        compiler_params=_cp("parallel", "arbitrary"),
        name="window_attn",
    )(qp, k, p, k, p, sink_rows)


def _merge_kernel(yal_ref, yac_ref, ybl_ref, ybc_ref, yc_ref, yd_ref, gate_ref, x_ref, mod_ref, wbr_ref,
                  wout_ref, g2_ref, xo_ref, h2_ref, *maybe_tok_ref, n_lat_tiles):
    tm, d = x_ref.shape
    ctx_rows = jnp.full((tm, MIX_W), pl.program_id(0), jnp.int32) >= n_lat_tiles
    ya = jnp.where(ctx_rows, yac_ref[...], yal_ref[...])
    yb = jnp.where(ctx_rows, ybc_ref[...], ybl_ref[...])
    acc = None
    for n, y in enumerate((ya, yb, yc_ref[...], yd_ref[...])):
        gate = jax.nn.sigmoid(gate_ref[:, n * d:(n + 1) * d].astype(F32))
        term = gate * _dot(y, wbr_ref[n])
        acc = term if acc is None else acc + term
    x = x_ref[...] + mod_ref[0, 2:3, :] * _dot(acc.astype(BF16), wout_ref[...])
    xo_ref[...] = x
    ms = jnp.mean(x * x, axis=-1, keepdims=True)
    y = x * lax.rsqrt(ms + EPS) * g2_ref[...]
    h2 = y * (1.0 + mod_ref[0, 4:5, :]) + mod_ref[0, 3:4, :]
    h2_ref[...] = h2
    if maybe_tok_ref:
        _to_token_tiles(maybe_tok_ref[0], h2)


def _to_token_tiles(ref, x):
    rows, d = x.shape
    k = d // 128
    for c in range(k):
        ref[pl.ds(c, rows, stride=k), :] = x[:, 128 * c:128 * (c + 1)]


def _from_token_tiles(ref, rows, d):
    k = d // 128
    return jnp.concatenate([ref[pl.ds(c, rows, stride=k), :] for c in range(k)], axis=-1)


def _merge(ys, p, x, mod, wbr, wout, g2, t_out, t_lat, seq, tm, token_tiles):
    d = x.shape[1]
    k = d // 128

    def row(width):
        return pl.BlockSpec((tm, width), lambda i: (i, 0))

    def const(shape):
        return pl.BlockSpec(shape, lambda i: (0,) * len(shape))

    out_shape = [jax.ShapeDtypeStruct((t_out, d), F32), jax.ShapeDtypeStruct((t_out, d), F32)]
    out_specs = [row(d), row(d)]
    if token_tiles:
        out_shape.append(jax.ShapeDtypeStruct((t_out * k, 128), F32))
        out_specs.append(pl.BlockSpec((tm * k, 128), lambda i: (i, 0)))
    (ya_l, ya_c), (yb_l, yb_c), yc, yd = ys
    n_lat_tiles = t_lat // tm
    n_ctx_tiles = ya_c.shape[0] // tm
    lat = pl.BlockSpec((tm, MIX_W), lambda i: (jnp.minimum(i, n_lat_tiles - 1), 0))
    cx = pl.BlockSpec((tm, MIX_W), lambda i: (jnp.clip(i - n_lat_tiles, 0, n_ctx_tiles - 1), 0))
    return pl.pallas_call(
        functools.partial(_merge_kernel, n_lat_tiles=n_lat_tiles),
        out_shape=tuple(out_shape),
        grid=(t_out // tm,),
        in_specs=[lat, cx, lat, cx, row(MIX_W), row(MIX_W), row(4 * d), row(d),
                  pl.BlockSpec((1, 6, d), lambda i: (_mod_group(i * tm, t_lat, seq), 0, 0)),
                  const((4, MIX_W, d)), const((d, d)), const((1, d))],
        out_specs=tuple(out_specs),
        compiler_params=_cp("parallel"),
        name="merge",
    )(ya_l, ya_c, yb_l, yb_c, yc, yd, p, x, mod, wbr, wout, g2.reshape(1, d))


def _ffn_dense_kernel(h_ref, x_ref, mod_ref, wg_ref, wu_ref, wo_ref, o_ref, hb_ref, acc_ref):
    j = pl.program_id(1)

    @pl.when(j == 0)
    def _():
        hb_ref[...] = h_ref[...].astype(BF16)
        acc_ref[...] = jnp.zeros_like(acc_ref)

    h = hb_ref[...]
    a = _silu(_dot(h, wg_ref[...])) * _dot(h, wu_ref[...])
    acc_ref[...] += _dot(a.astype(BF16), wo_ref[...])

    @pl.when(j == pl.num_programs(1) - 1)
    def _():
        o_ref[...] = x_ref[...] + mod_ref[0, 5:6, :] * acc_ref[...]


def _ffn_dense(h2, x, mod, w_in, w_out, t_lat, seq, tm):
    t, d = x.shape
    f = w_out.shape[0]
    tf = 256
    nf = f // tf
    return pl.pallas_call(
        _ffn_dense_kernel,
        out_shape=jax.ShapeDtypeStruct((t, d), F32),
        grid=(t // tm, nf),
        in_specs=[pl.BlockSpec((tm, d), lambda i, j: (i, 0)),
                  pl.BlockSpec((tm, d), lambda i, j: (i, 0)),
                  pl.BlockSpec((1, 6, d), lambda i, j: (_mod_group(i * tm, t_lat, seq), 0, 0)),
                  pl.BlockSpec((d, tf), lambda i, j: (0, j)),
                  pl.BlockSpec((d, tf), lambda i, j: (0, nf + j)),
                  pl.BlockSpec((tf, d), lambda i, j: (j, 0))],
        out_specs=pl.BlockSpec((tm, d), lambda i, j: (i, 0)),
        scratch_shapes=[pltpu.VMEM((tm, d), BF16), pltpu.VMEM((tm, d), F32)],
        compiler_params=_cp("parallel", "arbitrary"),
        name="ffn_dense",
    )(h2, x, mod, w_in, w_in, w_out)


def _router_kernel(h_ref, w_ref, e_ref, g1_ref, g2_ref):
    h = h_ref[...]
    h_hi = h.astype(BF16)
    h_lo = (h - h_hi.astype(F32)).astype(BF16)
    logits = _dot(h_hi, w_ref[0]) + _dot(h_lo, w_ref[0]) + _dot(h_hi, w_ref[1])
    lane = lax.broadcasted_iota(jnp.int32, logits.shape, 1)
    lane_f = lane.astype(F32)
    logits = jnp.where(lane < N_EXPERTS, logits, -jnp.inf)
    m1 = jnp.max(logits, axis=-1, keepdims=True)
    i1 = jnp.min(jnp.where(logits == m1, lane_f, 128.0), axis=-1, keepdims=True)
    rest = jnp.where(lane_f == i1, -jnp.inf, logits)
    m2 = jnp.max(rest, axis=-1, keepdims=True)
    i2 = jnp.min(jnp.where(rest == m2, lane_f, 128.0), axis=-1, keepdims=True)
    e2 = jnp.exp(m2 - m1)
    g1 = 1.0 / (1.0 + e2)
    e_ref[...] = jnp.where(lane == 0, i1, jnp.where(lane == 1, i2, 0.0)).astype(jnp.int32)
    g1_ref[...] = jnp.broadcast_to(g1, g1_ref.shape)
    g2_ref[...] = jnp.broadcast_to(e2 * g1, g2_ref.shape)


def _router(h2, router, tm):
    t, d = h2.shape
    r = jnp.zeros((d, 128), F32).at[:, :N_EXPERTS].set(router.astype(F32))
    r_hi = r.astype(BF16)
    r_lo = (r - r_hi.astype(F32)).astype(BF16)
    shp = jax.ShapeDtypeStruct((t, 128), F32)
    return pl.pallas_call(
        _router_kernel,
        out_shape=(jax.ShapeDtypeStruct((t, 128), jnp.int32), shp, shp),
        grid=(t // tm,),
        in_specs=[pl.BlockSpec((tm, d), lambda i: (i, 0)),
                  pl.BlockSpec((2, d, 128), lambda i: (0, 0, 0))],
        out_specs=(pl.BlockSpec((tm, 128), lambda i: (i, 0)),) * 3,
        compiler_params=_cp("parallel"),
        name="router",
    )(h2, jnp.stack([r_hi, r_lo]))


def _experts_kernel(te_ref, tok0_ref, tokn_ref, dst_ref, h_ref, wg_ref, wu_ref, wo_ref, y_ref,
                    xbuf, ybuf, xb_ref, acc_ref, sem_in, sem_out, *, n_tiles, tm):
    i, j = pl.program_id(0), pl.program_id(1)
    last_j = pl.num_programs(1) - 1
    slot = i % 2
    other = 1 - slot
    per_step = tm // MOE_STEPS
    tr = TOK_ROWS

    def tok(ref, t):
        return ref.at[pl.ds(pl.multiple_of(t * tr, tr), tr)]

    def gather(idx_ref, r, s):
        return pltpu.make_async_copy(tok(h_ref, idx_ref[0, 0, r]), tok(xbuf.at[s], r), sem_in.at[s])

    def scatter(r, s):
        return pltpu.make_async_copy(tok(ybuf.at[s], r), tok(y_ref, dst_ref[0, 0, r]), sem_out.at[s])

    def wait_tile(copy_of_row0):
        def body(k, carry):
            for _ in range(MOE_WAITS):
                copy_of_row0().wait()
            return carry

        lax.fori_loop(0, tm // MOE_WAITS, body, 0)

    def wait_gather(s):
        wait_tile(lambda: pltpu.make_async_copy(tok(h_ref, 0), tok(xbuf.at[s], 0), sem_in.at[s]))

    def wait_scatter(s):
        wait_tile(lambda: pltpu.make_async_copy(tok(ybuf.at[s], 0), tok(y_ref, 0), sem_out.at[s]))

    def row_traffic(with_gather, with_scatter):
        def rows(base, count):
            for k in range(count):
                if with_gather:
                    gather(tokn_ref, base + k, other).start(priority=k % 2)
                if with_scatter:
                    scatter(base + k, other).start(priority=(k + 1) % 2)

        rows(j * per_step, per_step)

        @pl.when(j == last_j)
        def _():
            rows(per_step * MOE_STEPS, tm - per_step * MOE_STEPS)

    def compute():
        x = xb_ref[...]
        a = _silu(_dot(x, wg_ref[0])) * _dot(x, wu_ref[0])
        acc_ref[...] += _dot(a.astype(BF16), wo_ref[0])

    @pl.when((i == 0) & (j == 0))
    def _():
        def body(r, carry):
            gather(tok0_ref, r, 0).start()
            return carry

        lax.fori_loop(0, tm, body, 0)

    @pl.when(j == 0)
    def _():
        wait_gather(slot)

        @pl.when(i < n_tiles)
        def _():
            xb_ref[...] = _from_token_tiles(xbuf.at[slot], tm, xb_ref.shape[1]).astype(BF16)
            acc_ref[...] = jnp.zeros_like(acc_ref)

    @pl.when(i == 0)
    def _():
        compute()
        row_traffic(True, False)

    @pl.when((i > 0) & (i < n_tiles))
    def _():
        compute()
        row_traffic(True, True)

    @pl.when(i == n_tiles)
    def _():
        row_traffic(False, True)

    @pl.when(j == last_j)
    def _():
        @pl.when((i >= 2) & (i < n_tiles))
        def _():
            wait_scatter(slot)

        @pl.when(i < n_tiles)
        def _():
            _to_token_tiles(ybuf.at[slot], acc_ref[...])

        @pl.when(i == n_tiles)
        def _():
            wait_scatter(slot)
            wait_scatter(other)


def _experts(h2_tiles, src_tok, dst_row, tile_e, w_in, w_out, n_out_rows):
    d = w_in.shape[1]
    assert d == TOK_ROWS * 128
    n_tiles = dst_row.shape[0]
    f = w_out.shape[1]
    tf = f // MOE_STEPS
    tm = MOE_TILE
    assert n_tiles >= 2 and tm % MOE_WAITS == 0 and tf % 128 == 0

    def smem(index_map):
        return pl.BlockSpec((1, 1, tm), index_map, memory_space=pltpu.SMEM)

    grid_spec = pltpu.PrefetchScalarGridSpec(
        num_scalar_prefetch=1,
        grid=(n_tiles + 1, MOE_STEPS),
        in_specs=[smem(lambda i, j, te: (0, 0, 0)),
                  smem(lambda i, j, te: (jnp.minimum(i + 1, n_tiles), 0, 0)),
                  smem(lambda i, j, te: (jnp.clip(i - 1, 0, n_tiles - 1), 0, 0)),
                  pl.BlockSpec(memory_space=pl.ANY),
                  pl.BlockSpec((1, d, tf), lambda i, j, te: (te[i], 0, j)),
                  pl.BlockSpec((1, d, tf), lambda i, j, te: (te[i], 0, MOE_STEPS + j)),
                  pl.BlockSpec((1, tf, d), lambda i, j, te: (te[i], j, 0))],
        out_specs=pl.BlockSpec(memory_space=pl.ANY),
        scratch_shapes=[pltpu.VMEM((2, tm * TOK_ROWS, 128), F32), pltpu.VMEM((2, tm * TOK_ROWS, 128), F32),
                        pltpu.VMEM((tm, d), BF16), pltpu.VMEM((tm, d), F32),
                        pltpu.SemaphoreType.DMA((2,)), pltpu.SemaphoreType.DMA((2,))])
    return pl.pallas_call(
        functools.partial(_experts_kernel, n_tiles=n_tiles, tm=tm),
        out_shape=jax.ShapeDtypeStruct((n_out_rows * TOK_ROWS, 128), F32),
        grid_spec=grid_spec,
        compiler_params=_cp("arbitrary", "arbitrary"),
        name="experts",
    )(tile_e, src_tok, src_tok, dst_row, h2_tiles, w_in, w_in, w_out)


def _combine_kernel(y1_ref, y2_ref, x_ref, mod_ref, g1_ref, g2_ref, *rest):
    o_ref = rest[-1]
    tm, d = x_ref.shape
    g1 = jnp.concatenate([g1_ref[...]] * (d // 128), axis=-1)
    g2 = jnp.concatenate([g2_ref[...]] * (d // 128), axis=-1)
    y = g1 * _from_token_tiles(y1_ref, tm, d) + g2 * _from_token_tiles(y2_ref, tm, d)
    x = x_ref[...] + mod_ref[0, 5:6, :] * y
    if len(rest) == 2:
        ms = jnp.mean(x * x, axis=-1, keepdims=True)
        x = x * lax.rsqrt(ms + EPS) * rest[0][...]
    o_ref[...] = x


def _combine(y_tiles, x, mod, g1, g2, t_lat, seq, tm, final_g):
    t, d = x.shape
    k = d // 128
    extra_specs, extra_args = [], []
    if final_g is not None:
        extra_specs, extra_args = [pl.BlockSpec((1, d), lambda i: (0, 0))], [final_g.reshape(1, d)]
    return pl.pallas_call(
        _combine_kernel,
        out_shape=jax.ShapeDtypeStruct((t, d), F32),
        grid=(t // tm,),
        in_specs=[pl.BlockSpec((tm * k, 128), lambda i: (i, 0)),
                  pl.BlockSpec((tm * k, 128), lambda i: (t // tm + i, 0)),
                  pl.BlockSpec((tm, d), lambda i: (i, 0)),
                  pl.BlockSpec((1, 6, d), lambda i: (_mod_group(i * tm, t_lat, seq), 0, 0)),
                  pl.BlockSpec((tm, 128), lambda i: (i, 0)),
                  pl.BlockSpec((tm, 128), lambda i: (i, 0))] + extra_specs,
        out_specs=pl.BlockSpec((tm, d), lambda i: (i, 0)),
        compiler_params=_cp("parallel"),
        name="moe_combine",
    )(y_tiles, y_tiles, x, mod, g1, g2, *extra_args)


def _moe(h2, h2_tiles, x, mod, router, w_in, w_out, t_lat, seq, tm, final_g=None):
    t, d = h2.shape
    e_idx, g1, g2 = _router(h2, router, tm)
    e_flat = e_idx[:, :2].reshape(-1)
    onehot = (e_flat[:, None] == jnp.arange(N_EXPERTS, dtype=jnp.int32)[None, :]).astype(jnp.int32)
    csum = jnp.cumsum(onehot, axis=0)
    counts = csum[-1]
    padded = (counts + MOE_TILE - 1) // MOE_TILE * MOE_TILE
    ends = jnp.cumsum(padded)
    pstarts = ends - padded
    dest = jnp.sum(onehot * (pstarts[None, :] + csum - 1), axis=1).astype(jnp.int32)
    n_tiles = (2 * t + MOE_TILE - 1) // MOE_TILE + N_EXPERTS
    n_rows = n_tiles * MOE_TILE
    slot_a = jnp.full((n_rows,), -1, jnp.int32).at[dest].set(jnp.arange(2 * t, dtype=jnp.int32))
    is_pad = slot_a < 0
    slot_row = jnp.where(is_pad, 2 * t - 1 + jnp.cumsum(is_pad.astype(jnp.int32)),
                         (slot_a % 2) * t + slot_a // 2)
    src_tok = jnp.where(is_pad, 0, slot_a // 2)
    src_tok = jnp.concatenate([src_tok, jnp.zeros((MOE_TILE,), jnp.int32)]).reshape(n_tiles + 1, 1, MOE_TILE)
    tile_start = jnp.arange(n_tiles + 1, dtype=jnp.int32) * MOE_TILE
    tile_e = jnp.minimum(jnp.searchsorted(ends, tile_start, side='right'), N_EXPERTS - 1).astype(jnp.int32)

    y = _experts(h2_tiles, src_tok, slot_row.reshape(n_tiles, 1, MOE_TILE), tile_e, w_in, w_out, n_rows)
    return _combine(y, x, mod, g1, g2, t_lat, seq, tm, final_g)


def _final_norm_kernel(x_ref, g_ref, o_ref):
    x = x_ref[...]
    ms = jnp.mean(x * x, axis=-1, keepdims=True)
    o_ref[...] = x * lax.rsqrt(ms + EPS) * g_ref[...]


def _final_norm(x, g, tm):
    t, d = x.shape
    return pl.pallas_call(
        _final_norm_kernel,
        out_shape=jax.ShapeDtypeStruct((t, d), F32),
        grid=(t // tm,),
        in_specs=[pl.BlockSpec((tm, d), lambda i: (i, 0)), pl.BlockSpec((1, d), lambda i: (0, 0))],
        out_specs=pl.BlockSpec((tm, d), lambda i: (i, 0)),
        compiler_params=_cp("parallel"),
        name="final_norm",
    )(x, g.reshape(1, d))


def _proj_weights(w_in):
    o = {}
    acc = 0
    for name, size in (('a_z', 256), ('a_x', 256), ('a_b', 256), ('a_c', 256), ('a_dt', 8), ('b_u', 256),
                       ('c_q', 256), ('c_k', 128), ('c_v', 128), ('d_q', 256), ('d_k', 128), ('d_v', 128),
                       ('gates', 4096)):
        o[name] = (acc, size)
        acc += size
    order = ('gates', 'a_z', 'b_u', 'c_q', 'd_q', 'c_k', 'c_v', 'd_k', 'd_v', 'a_x', 'a_b', 'a_c')
    w = jnp.concatenate([w_in[:, :, o[n][0]:o[n][0] + o[n][1]] for n in order], axis=-1).astype(BF16)
    dt0 = o['a_dt'][0]
    wdt = jnp.pad(w_in[:, :, dt0:dt0 + 8], ((0, 0), (0, 0), (0, 120))).astype(BF16)
    return w, wdt


def kernel(x, c, ctx, c_ctx, norm1_g, norm2_g, ada_w, ada_b, w_in, ssd_conv_w, ssd_conv_b, ssd_a_log,
           ssd_dt_bias, ssd_d, ssd_norm_g, s5_lam_re, s5_lam_im, s5_log_step, s5_b_re, s5_b_im, s5_c_re,
           s5_c_im, s5_d, s5_glu_w, qk_norm_g, swa_sink, w_branch, w_out, ffn_w_in, ffn_w_out, moe_router,
           moe_w_in, moe_w_out, final_norm_g):
    bsz, seq, d = x.shape
    n_ctx = ctx.shape[1]
    depth = w_in.shape[0]
    t_lat, t_ctx = bsz * seq, bsz * n_ctx
    tm = _pow2_tile(1024, seq, t_ctx)
    tm_small = _pow2_tile(512, seq, t_ctx)

    cvec = jnp.zeros((16, d), F32).at[0].set(c_ctx).at[1:1 + bsz].set(c)
    mod = _adaln(cvec, ada_w, ada_b).reshape(depth, 16, 6, d)
    wp, wdt = _proj_weights(w_in)
    wbr = w_branch.astype(BF16)
    wo = w_out.astype(BF16)
    ffn_in, ffn_out = ffn_w_in.astype(BF16), ffn_w_out.astype(BF16)
    moe_in, moe_out = moe_w_in.astype(BF16), moe_w_out.astype(BF16)

    xx = jnp.concatenate([x.reshape(t_lat, d), ctx.reshape(t_ctx, d)], axis=0)
    for l in range(depth):
        with_ctx = l < depth - 1
        t_out = t_lat + (t_ctx if with_ctx else 0)
        p, pdt = _inproj(xx, norm1_g[l], mod[l], wp[l], wdt[l], t_lat, seq, tm)
        ya = _ssd(p, pdt, ssd_conv_w[l], ssd_conv_b[l], ssd_a_log[l], ssd_dt_bias[l], ssd_d[l],
                  ssd_norm_g[l], bsz, seq, n_ctx)
        yb = _s5(p, s5_lam_re[l], s5_lam_im[l], s5_log_step[l], s5_b_re[l], s5_b_im[l], s5_c_re[l],
                 s5_c_im[l], s5_d[l], s5_glu_w[l], bsz, seq, n_ctx)
        q1, q2, k1, k2 = _prep(p, qk_norm_g[l], t_lat, seq, tm)
        yc = _gattn(q1, k1, p, bsz, seq, n_ctx, with_ctx)
        yd = _wattn(q2, k2, p, swa_sink[l], bsz, seq, n_ctx, with_ctx)
        routed = l % 2 == 1
        xx, h2, *tiles = _merge((ya, yb, yc, yd), p, xx, mod[l], wbr[l], wo[l], norm2_g[l], t_out, t_lat, seq,
                                tm_small, routed)
        if routed:
            xx = _moe(h2, tiles[0], xx, mod[l], moe_router[l // 2], moe_in[l // 2], moe_out[l // 2], t_lat,
                      seq, tm_small, None if with_ctx else final_norm_g)
        else:
            xx = _ffn_dense(h2, xx, mod[l], ffn_in[l // 2], ffn_out[l // 2], t_lat, seq, tm)
    if depth % 2 == 1:
        xx = _final_norm(xx[:t_lat], final_norm_g, tm)
    return xx.reshape(bsz, seq, d)
```
